```python
import jax, jax.numpy as jnp
from jax import lax
import numpy as np

D_MODEL = 4096
BATCH = 4
SEQ = 2048
DEPTH = 1
DEC_BATCH = 16
DEC_SEQ = 64
PAST_LEN = 2048

CHUNK = 64
RET_HEADS = 8
RET_DK = 256
RET_DV = 256
RET_QK = RET_HEADS * RET_DK
RET_V = RET_HEADS * RET_DV
ROPE_BASE = 10000.0
SWA_HEADS = 32
SWA_KV_HEADS = 4
SWA_HD = 64
SWA_GROUP = SWA_HEADS // SWA_KV_HEADS
SWA_Q = SWA_HEADS * SWA_HD
SWA_KV = SWA_KV_HEADS * SWA_HD
WINDOW = 128
WINDOW_CHUNKS = WINDOW // CHUNK
D_FF = 11008
CONV_W = 3
EPS = 1e-6
NEG_INF = -1e30
IN_SIZES = (RET_QK, RET_QK, RET_V, RET_V, SWA_Q, SWA_KV, SWA_KV, D_MODEL, D_MODEL)
N_IN = 2 * RET_QK + 2 * RET_V + SWA_Q + 2 * SWA_KV + 2 * D_MODEL

kernel_name = "hybrid_retention_swa_sink_convffn_adaln_step"


def rmsnorm(x, g):
    xf = x.astype(jnp.float32)
    y = xf * lax.rsqrt(jnp.mean(xf * xf, axis=-1, keepdims=True) + EPS)
    return (y * g.astype(jnp.float32)).astype(x.dtype)


def rope(x, pos):
    d = x.shape[-1]
    half = d // 2
    inv = ROPE_BASE ** (-jnp.arange(half, dtype=jnp.float32) * (2.0 / d))
    ang = pos.astype(jnp.float32)[:, None] * inv[None, :]
    cos = jnp.cos(ang)[None, :, None, :]
    sin = jnp.sin(ang)[None, :, None, :]
    xf = x.astype(jnp.float32)
    x1, x2 = xf[..., :half], xf[..., half:]
    return jnp.concatenate([x1 * cos - x2 * sin, x2 * cos + x1 * sin], axis=-1)


def ret_log_decay():
    h = jnp.arange(RET_HEADS, dtype=jnp.float32)
    return jnp.log1p(-jnp.exp2(-5.0 - h))


def retention_chunk(S, qkv):
    q, k, v = qkv
    C = q.shape[1]
    lg = ret_log_decay()
    idx = jnp.arange(C, dtype=jnp.float32)
    diff = idx[:, None] - idx[None, :]
    dmask = jnp.where(diff >= 0, jnp.exp(lg[:, None, None] * jnp.maximum(diff, 0.0)), 0.0)
    scores = jnp.einsum('bihd,bjhd->bhij', q, k) * dmask[None]
    y = jnp.einsum('bhij,bjhe->bihe', scores, v)
    y = y + jnp.einsum('bihd,bhde->bihe', q, S) * jnp.exp(lg[None, :] * (idx[:, None] + 1.0))[None, :, :, None]
    k_dec = k * jnp.exp(lg[None, :] * (C - 1.0 - idx[:, None]))[None, :, :, None]
    S = S * jnp.exp(lg * C)[None, :, None, None] + jnp.einsum('bjhd,bjhe->bhde', k_dec, v)
    return S, y


def retention(q, k, v, S0):
    B, T = q.shape[:2]
    C = min(CHUNK, T)
    n = T // C
    def blocks(a):
        return a.astype(jnp.float32).reshape(B, n, C, a.shape[2], a.shape[3]).transpose(1, 0, 2, 3, 4)
    S, ys = lax.scan(retention_chunk, S0.astype(jnp.float32), (blocks(q), blocks(k), blocks(v)))
    y = ys.transpose(1, 0, 2, 3, 4).reshape(B, T, RET_HEADS, RET_DV)
    return y, S


def head_group_norm(y):
    yf = y.astype(jnp.float32)
    mu = jnp.mean(yf, axis=-1, keepdims=True)
    var = jnp.mean(jnp.square(yf - mu), axis=-1, keepdims=True)
    return (yf - mu) * lax.rsqrt(var + EPS)


def sink_softmax(s, sink):
    m = jnp.maximum(jnp.max(s, axis=-1, keepdims=True), sink)
    e = jnp.exp(s - m)
    return e / (jnp.sum(e, axis=-1, keepdims=True) + jnp.exp(sink - m))


def swa_banded(q, k, v, sinks):
    B, T = q.shape[:2]
    n = T // CHUNK
    pad = WINDOW_CHUNKS * CHUNK
    kp = jnp.pad(k, ((0, 0), (pad, 0), (0, 0), (0, 0))).reshape(B, n + WINDOW_CHUNKS, CHUNK, SWA_KV_HEADS, SWA_HD)
    vp = jnp.pad(v, ((0, 0), (pad, 0), (0, 0), (0, 0))).reshape(B, n + WINDOW_CHUNKS, CHUNK, SWA_KV_HEADS, SWA_HD)
    kb = jnp.concatenate([kp[:, i:i + n] for i in range(WINDOW_CHUNKS + 1)], axis=2)
    vb = jnp.concatenate([vp[:, i:i + n] for i in range(WINDOW_CHUNKS + 1)], axis=2)
    blk = jnp.arange(n)[:, None] + jnp.repeat(jnp.arange(WINDOW_CHUNKS + 1), CHUNK)[None, :] - WINDOW_CHUNKS
    valid = blk >= 0
    qb = q.reshape(B, n, CHUNK, SWA_KV_HEADS, SWA_GROUP, SWA_HD)
    s = jnp.einsum('bnqkgd,bnskd->bnkgqs', qb, kb).astype(jnp.float32) * (SWA_HD ** -0.5)
    s = jnp.where(valid[None, :, None, None, None, :], s, NEG_INF)
    sk = sinks.astype(jnp.float32).reshape(SWA_KV_HEADS, SWA_GROUP)[None, None, :, :, None, None]
    p = sink_softmax(s, sk).astype(v.dtype)
    o = jnp.einsum('bnkgqs,bnskd->bnqkgd', p, vb)
    return o.reshape(B, T, SWA_Q)


def swa_with_cache(q, k, v, k_cache, v_cache, sinks):
    B, L = q.shape[:2]
    kk = jnp.concatenate([k_cache.astype(k.dtype), k], axis=1)
    vv = jnp.concatenate([v_cache.astype(v.dtype), v], axis=1)
    s = jnp.einsum('bqkgd,bskd->bkgqs', q, kk).astype(jnp.float32) * (SWA_HD ** -0.5)
    sk = sinks.astype(jnp.float32).reshape(SWA_KV_HEADS, SWA_GROUP)[None, :, :, None, None]
    p = sink_softmax(s, sk).astype(v.dtype)
    o = jnp.einsum('bkgqs,bskd->bqkgd', p, vv)
    return o.reshape(B, L, SWA_Q)


def conv_ffn(h, conv_state, w_up, w_conv, b_conv, w_down):
    T = h.shape[1]
    uv = h @ w_up
    u, v = uv[..., :D_FF], uv[..., D_FF:]
    up = jnp.concatenate([conv_state.astype(u.dtype), u], axis=1)
    uc = sum(w_conv[i] * up[:, i:i + T] for i in range(CONV_W)) + b_conv
    out = (jax.nn.gelu(uc) * v) @ w_down
    return out, up[:, -(CONV_W - 1):]


def hybrid_layer(x, c, pos, ret_S, swa_kc, swa_vc, conv_state,
                 g_mix, g_ffn, w_ada, b_ada, w_in, swa_sinks, w_ret_o, w_swa_o, w_out,
                 w_up, w_conv, b_conv, w_down):
    B, T, _ = x.shape
    mod = jax.nn.silu(c) @ w_ada + b_ada
    sh1, sc1, gt1, sh2, sc2, gt2 = [m[:, None, :] for m in jnp.split(mod, 6, axis=-1)]
    h = rmsnorm(x, g_mix) * (1.0 + sc1) + sh1
    proj = h @ w_in
    offs = np.cumsum(IN_SIZES)[:-1].tolist()
    rq, rk, rv, rg, sq, sk, sv, ga, gb = jnp.split(proj, offs, axis=-1)
    rq = rope(rq.reshape(B, T, RET_HEADS, RET_DK), pos)
    rk = rope(rk.reshape(B, T, RET_HEADS, RET_DK), pos) * (RET_DK ** -0.5)
    rv = rv.reshape(B, T, RET_HEADS, RET_DV)
    y_ret, S_new = retention(rq, rk, rv, ret_S)
    y_ret = head_group_norm(y_ret).reshape(B, T, RET_V).astype(x.dtype)
    y_ret = (jax.nn.silu(rg) * y_ret) @ w_ret_o
    sq = sq.reshape(B, T, SWA_KV_HEADS, SWA_GROUP, SWA_HD)
    sk = sk.reshape(B, T, SWA_KV_HEADS, SWA_HD)
    sv = sv.reshape(B, T, SWA_KV_HEADS, SWA_HD)
    if swa_kc is None:
        y_swa = swa_banded(sq, sk, sv, swa_sinks)
        keep = min(WINDOW, T)
        k_new, v_new = sk[:, -keep:], sv[:, -keep:]
    else:
        y_swa = swa_with_cache(sq, sk, sv, swa_kc, swa_vc, swa_sinks)
        k_new, v_new = sk, sv
    y_swa = y_swa @ w_swa_o
    merged = jax.nn.sigmoid(ga) * y_ret + jax.nn.sigmoid(gb) * y_swa
    x = x + gt1 * (merged @ w_out)
    h2 = rmsnorm(x, g_ffn) * (1.0 + sc2) + sh2
    f, conv_new = conv_ffn(h2, conv_state, w_up, w_conv, b_conv, w_down)
    x = x + gt2 * f
    return x, S_new.astype(x.dtype), k_new, v_new, conv_new


def setup_inputs(seed: int = 0) -> dict:
    key = jax.random.key(seed)
    ks = jax.random.split(key, 24)
    f32 = jnp.float32
    swa_rows = min(WINDOW, PAST_LEN)
    nrm = lambda k, shape, s: jax.random.normal(k, shape, f32) * s
    return {
        "x_prompt": nrm(ks[0], (BATCH, SEQ, D_MODEL), 1.0),
        "x_sample": nrm(ks[1], (DEC_BATCH, DEC_SEQ, D_MODEL), 1.0),
        "cache_swa_k": nrm(ks[2], (DEPTH, DEC_BATCH, swa_rows, SWA_KV_HEADS, SWA_HD), 1.0),
        "cache_swa_v": nrm(ks[3], (DEPTH, DEC_BATCH, swa_rows, SWA_KV_HEADS, SWA_HD), 1.0),
        "state_ret": nrm(ks[4], (DEPTH, DEC_BATCH, RET_HEADS, RET_DK, RET_DV), 0.1),
        "state_conv": nrm(ks[5], (DEPTH, DEC_BATCH, CONV_W - 1, D_FF), 1.0),
        "c_prompt": nrm(ks[6], (BATCH, D_MODEL), 1.0),
        "c_sample": nrm(ks[7], (DEC_BATCH, D_MODEL), 1.0),
        "g_mix": 1.0 + nrm(ks[8], (DEPTH, D_MODEL), 0.02),
        "g_ffn": 1.0 + nrm(ks[9], (DEPTH, D_MODEL), 0.02),
        "w_ada": nrm(ks[10], (DEPTH, D_MODEL, 6 * D_MODEL), 0.5 * D_MODEL ** -0.5),
        "b_ada": nrm(ks[11], (DEPTH, 6 * D_MODEL), 0.02),
        "w_in": nrm(ks[12], (DEPTH, D_MODEL, N_IN), D_MODEL ** -0.5),
        "swa_sinks": nrm(ks[13], (DEPTH, SWA_HEADS), 0.5),
        "w_ret_o": nrm(ks[14], (DEPTH, RET_V, D_MODEL), RET_V ** -0.5),
        "w_swa_o": nrm(ks[15], (DEPTH, SWA_Q, D_MODEL), SWA_Q ** -0.5),
        "w_out": nrm(ks[16], (DEPTH, D_MODEL, D_MODEL), D_MODEL ** -0.5),
        "w_up": nrm(ks[17], (DEPTH, D_MODEL, 2 * D_FF), D_MODEL ** -0.5),
        "w_conv": nrm(ks[18], (DEPTH, CONV_W, D_FF), CONV_W ** -0.5),
        "b_conv": nrm(ks[19], (DEPTH, D_FF), 0.02),
        "w_down": nrm(ks[20], (DEPTH, D_FF, D_MODEL), D_FF ** -0.5),
        "g_final": 1.0 + nrm(ks[21], (D_MODEL,), 0.02),
    }


def reference(x_prompt, x_sample, cache_swa_k, cache_swa_v, state_ret, state_conv, c_prompt, c_sample,
              g_mix, g_ffn, w_ada, b_ada, w_in, swa_sinks, w_ret_o, w_swa_o, w_out,
              w_up, w_conv, b_conv, w_down, g_final):
    Bp, Tp = x_prompt.shape[:2]
    pos_p = jnp.arange(Tp, dtype=jnp.int32)
    pos_s = PAST_LEN + jnp.arange(x_sample.shape[1], dtype=jnp.int32)
    xp, xs = x_prompt, x_sample
    rp, kp, vp, cp, rs, ks_, vs, cs = [], [], [], [], [], [], [], []
    for l in range(DEPTH):
        lw = (g_mix[l], g_ffn[l], w_ada[l], b_ada[l], w_in[l], swa_sinks[l], w_ret_o[l], w_swa_o[l], w_out[l],
              w_up[l], w_conv[l], b_conv[l], w_down[l])
        S0 = jnp.zeros((Bp, RET_HEADS, RET_DK, RET_DV), jnp.float32)
        conv0 = jnp.zeros((Bp, CONV_W - 1, D_FF), xp.dtype)
        xp, s_p, k_p, v_p, c_p = hybrid_layer(xp, c_prompt, pos_p, S0, None, None, conv0, *lw)
        xs, s_s, k_s, v_s, c_s = hybrid_layer(xs, c_sample, pos_s, state_ret[l], cache_swa_k[l], cache_swa_v[l],
                                              state_conv[l], *lw)
        rp.append(s_p); kp.append(k_p); vp.append(v_p); cp.append(c_p)
        rs.append(s_s); ks_.append(k_s); vs.append(v_s); cs.append(c_s)
    y_prompt = rmsnorm(xp, g_final)
    y_sample = rmsnorm(xs, g_final)
    new_ret_prompt = jnp.stack(rp)
    new_swa_k_prompt = jnp.stack(kp)
    new_swa_v_prompt = jnp.stack(vp)
    new_conv_prompt = jnp.stack(cp)
    new_ret_sample = jnp.stack(rs)
    new_swa_k_sample = jnp.stack(ks_)
    new_swa_v_sample = jnp.stack(vs)
    new_conv_sample = jnp.stack(cs)
    return (y_prompt, y_sample, new_ret_prompt, new_swa_k_prompt, new_swa_v_prompt, new_conv_prompt,
            new_ret_sample, new_swa_k_sample, new_swa_v_sample, new_conv_sample)
```

```python
import functools

import jax
import jax.numpy as jnp
import numpy as np
from jax import lax
from jax.experimental import pallas as pl
from jax.experimental.pallas import tpu as pltpu

_F32 = jnp.float32
_BF16 = jnp.bfloat16

CHUNK = 64
RET_HEADS = 8
RET_DK = 256
RET_DV = 256
RET_QK = RET_HEADS * RET_DK
RET_V = RET_HEADS * RET_DV
ROPE_BASE = 10000.0
SWA_HEADS = 32
SWA_KV_HEADS = 4
SWA_HD = 64
SWA_GROUP = SWA_HEADS // SWA_KV_HEADS
SWA_PAIRS = SWA_GROUP // 2
SWA_Q = SWA_HEADS * SWA_HD
SWA_KV = SWA_KV_HEADS * SWA_HD
WINDOW = 128
WINDOW_CHUNKS = WINDOW // CHUNK
CONV_W = 3
EPS = 1e-6
PAST_LEN = 2048

_V7X_VMEM_BYTES = 64 * 1024 * 1024
_LANES = 128
_SUBLANES = 8

_TM = 1024
_TM_DOWN = 512
_TM_NORM = 256
_TN = 2 * SWA_KV
_TN_ADA = 512
_RET_CHUNK = 256


def _cparams(sem, vmem_estimate):
    limit = min(int(vmem_estimate * 1.15) + (4 << 20), _V7X_VMEM_BYTES - (6 << 20))
    return pltpu.CompilerParams(dimension_semantics=sem, vmem_limit_bytes=limit)


def _nbytes(shape, dtype):
    return int(np.prod(shape)) * jnp.dtype(dtype).itemsize


def _row_tile(m, t, cap):
    tm = min(cap, m)
    assert m % tm == 0 and (t % tm == 0 or tm % t == 0), (m, t, tm)
    return tm


def _seq_block(tm, t):
    return max(1, tm // t), max(1, t // tm)


def _mod_spec(tm, t, tn, col):
    spt, tps = _seq_block(tm, t)
    return pl.BlockSpec((spt, 1, tn), lambda i, j: (i // tps, 0, col(i, j)))


def _per_seq(val, mod, fn):
    spt = mod.shape[0]
    if spt == 1:
        return fn(val, mod[0])
    tm, n = val.shape
    return fn(val.reshape(spt, tm // spt, n), mod).reshape(tm, n)


def _ada_kernel(c_ref, w_ref, b_ref, o_ref):
    c = c_ref[...]
    a = (c * jax.nn.sigmoid(c)).astype(_BF16)
    o_ref[...] = jnp.dot(a, w_ref[...].astype(_BF16), preferred_element_type=_F32) + b_ref[...]


def _ada(c, w_ada, b_ada):
    rows, d = c.shape
    n = w_ada.shape[1]
    tn = _TN_ADA
    est = 2 * (_nbytes((rows, d), _F32) + _nbytes((d, tn), _F32) + _nbytes((rows, tn), _F32)) + _nbytes((d, tn), _BF16)
    return pl.pallas_call(
        _ada_kernel,
        grid=(n // tn,),
        in_specs=[pl.BlockSpec((rows, d), lambda j: (0, 0)),
                  pl.BlockSpec((d, tn), lambda j: (0, j)),
                  pl.BlockSpec((1, tn), lambda j: (0, j))],
        out_specs=pl.BlockSpec((rows, tn), lambda j: (0, j)),
        out_shape=jax.ShapeDtypeStruct((rows, n), _F32),
        compiler_params=_cparams(("arbitrary",), est),
        name="ada_mod",
    )(c, w_ada, b_ada.reshape(1, n))


def _rms(x, g):
    y = x * lax.rsqrt(jnp.mean(x * x, axis=-1, keepdims=True) + EPS)
    return y * g


def _norm_mod_kernel(x_ref, g_ref, sc_ref, sh_ref, o_ref):
    y = _rms(x_ref[...], g_ref[...])
    h = _per_seq(y, sc_ref[...], lambda v, sc: v * (1.0 + sc))
    h = _per_seq(h, sh_ref[...], lambda v, sh: v + sh)
    o_ref[...] = h.astype(o_ref.dtype)


def _norm_mod(x, g, sc, sh, t):
    m, d = x.shape
    tm = _row_tile(m, t, _TM_NORM)
    est = 2 * (_nbytes((tm, d), _F32) + _nbytes((tm, d), _BF16)) + 3 * _nbytes((tm, d), _F32)
    return pl.pallas_call(
        _norm_mod_kernel,
        grid=(m // tm, 1),
        in_specs=[pl.BlockSpec((tm, d), lambda i, j: (i, 0)),
                  pl.BlockSpec((1, d), lambda i, j: (0, 0)),
                  _mod_spec(tm, t, d, lambda i, j: 0),
                  _mod_spec(tm, t, d, lambda i, j: 0)],
        out_specs=pl.BlockSpec((tm, d), lambda i, j: (i, 0)),
        out_shape=jax.ShapeDtypeStruct((m, d), _BF16),
        compiler_params=_cparams(("arbitrary", "arbitrary"), est),
        name="norm_mod",
    )(x, g.reshape(1, d), sc, sh)


def _final_norm_kernel(x_ref, g_ref, o_ref):
    o_ref[...] = _rms(x_ref[...], g_ref[...])


def _final_norm(x, g):
    m, d = x.shape
    tm = min(_TM_NORM, m)
    est = 4 * _nbytes((tm, d), _F32) + 2 * _nbytes((tm, d), _F32)
    return pl.pallas_call(
        _final_norm_kernel,
        grid=(m // tm,),
        in_specs=[pl.BlockSpec((tm, d), lambda i: (i, 0)),
                  pl.BlockSpec((1, d), lambda i: (0, 0))],
        out_specs=pl.BlockSpec((tm, d), lambda i: (i, 0)),
        out_shape=jax.ShapeDtypeStruct((m, d), _F32),
        compiler_params=_cparams(("arbitrary",), est),
        name="final_norm",
    )(x, g.reshape(1, d))


_N_RET_TILES = RET_QK // _TN
_N_SQ_TILES = SWA_Q // _TN
_J_V = 2 * _N_RET_TILES
_J_RG = 3 * _N_RET_TILES
_J_SQ = 4 * _N_RET_TILES
_J_SKV = _J_SQ + _N_SQ_TILES
_J_GAB = _J_SKV + 1


def _in_proj_kernel(h_ref, w_ref, cos_ref, sin_ref,
                    qkv_ref, rg_ref, sq_ref, kv2_ref, skvf_ref, gab_ref):
    j = pl.program_id(1)
    acc = jnp.dot(h_ref[...], w_ref[...], preferred_element_type=_F32)
    half = RET_DK // 2

    @pl.when(j < _J_V)
    def _rope():
        c = cos_ref[...]
        s = sin_ref[...]
        scale = jnp.where(j >= _N_RET_TILES, RET_DK ** -0.5, 1.0).astype(_F32)
        for hh in range(_TN // RET_DK):
            lo = hh * RET_DK
            x1 = acc[:, lo:lo + half]
            x2 = acc[:, lo + half:lo + RET_DK]
            qkv_ref[:, lo:lo + half] = ((x1 * c - x2 * s) * scale).astype(_BF16)
            qkv_ref[:, lo + half:lo + RET_DK] = ((x2 * c + x1 * s) * scale).astype(_BF16)

    @pl.when((j >= _J_V) & (j < _J_RG))
    def _ret_v():
        qkv_ref[...] = acc.astype(_BF16)

    @pl.when((j >= _J_RG) & (j < _J_SQ))
    def _ret_gate():
        rg_ref[...] = acc

    @pl.when((j >= _J_SQ) & (j < _J_SKV))
    def _swa_q():
        sq_ref[...] = acc.astype(_BF16)

    @pl.when(j == _J_SKV)
    def _swa_kv():
        skvf_ref[...] = acc
        low = lax.broadcasted_iota(jnp.int32, (acc.shape[0], _LANES), 1) < SWA_HD
        for kv in range(SWA_KV_HEADS):
            for part, base in ((0, 0), (1, SWA_KV)):
                lo = base + _LANES * (kv // 2)
                slab = acc[:, lo:lo + _LANES]
                if kv % 2 == 0:
                    even = jnp.where(low, slab, 0.0)
                    odd = pltpu.roll(even, SWA_HD, 1)
                else:
                    odd = jnp.where(low, 0.0, slab)
                    even = pltpu.roll(odd, SWA_HD, 1)
                col = 2 * part * _LANES
                kv2_ref[kv, :, col:col + _LANES] = even.astype(_BF16)
                kv2_ref[kv, :, col + _LANES:col + 2 * _LANES] = odd.astype(_BF16)

    @pl.when(j >= _J_GAB)
    def _gates():
        gab_ref[...] = acc


def _rope_tables(t, pos0, tm):
    half = RET_DK // 2
    inv = ROPE_BASE ** (-jnp.arange(half, dtype=_F32) * (2.0 / RET_DK))
    pos = (pos0 + jnp.arange(t, dtype=jnp.int32)).astype(_F32)
    ang = pos[:, None] * inv[None, :]
    reps = max(1, tm // t)
    return jnp.tile(jnp.cos(ang), (reps, 1)), jnp.tile(jnp.sin(ang), (reps, 1))


def _in_proj(h, w_in, t, pos0):
    m, d = h.shape
    n_in = w_in.shape[1]
    tm = _row_tile(m, t, _TM)
    tn = _TN
    nj = n_in // tn
    assert n_in % tn == 0 and nj - _J_GAB == 2 * d // tn, (n_in, d)
    cos, sin = _rope_tables(t, pos0, tm)
    tab_blocks = cos.shape[0] // tm
    half = RET_DK // 2

    def clamp(lo, n):
        return lambda i, j: (i, jnp.clip(j - lo, 0, n - 1))

    out_shape = (
        jax.ShapeDtypeStruct((m, 3 * RET_QK), _BF16),
        jax.ShapeDtypeStruct((m, RET_V), _F32),
        jax.ShapeDtypeStruct((m, SWA_Q), _BF16),
        jax.ShapeDtypeStruct((SWA_KV_HEADS, m, 4 * _LANES), _BF16),
        jax.ShapeDtypeStruct((m, 2 * SWA_KV), _F32),
        jax.ShapeDtypeStruct((m, 2 * d), _F32),
    )
    out_specs = (
        pl.BlockSpec((tm, tn), clamp(0, _J_RG)),
        pl.BlockSpec((tm, tn), clamp(_J_RG, _N_RET_TILES)),
        pl.BlockSpec((tm, tn), clamp(_J_SQ, _N_SQ_TILES)),
        pl.BlockSpec((SWA_KV_HEADS, tm, 4 * _LANES), lambda i, j: (0, i, 0)),
        pl.BlockSpec((tm, tn), lambda i, j: (i, 0)),
        pl.BlockSpec((tm, tn), clamp(_J_GAB, nj - _J_GAB)),
    )
    est = 2 * (_nbytes((tm, d), _BF16) + _nbytes((d, tn), _BF16) + 2 * _nbytes((tm, half), _F32)
               + 3 * _nbytes((tm, tn), _BF16) + 3 * _nbytes((tm, tn), _F32)) + 2 * _nbytes((tm, tn), _F32)
    return pl.pallas_call(
        _in_proj_kernel,
        grid=(m // tm, nj),
        in_specs=[pl.BlockSpec((tm, d), lambda i, j: (i, 0)),
                  pl.BlockSpec((d, tn), lambda i, j: (0, j)),
                  pl.BlockSpec((tm, half), lambda i, j: (i % tab_blocks, 0)),
                  pl.BlockSpec((tm, half), lambda i, j: (i % tab_blocks, 0))],
        out_specs=out_specs,
        out_shape=out_shape,
        compiler_params=_cparams(("arbitrary", "arbitrary"), est),
        name="in_proj",
    )(h, w_in, cos, sin)


def _retention_kernel(*refs, hb, cc, n_chunks, has_state):
    if has_state:
        lg_ref, q_ref, k_ref, v_ref, rg_ref, s0_ref, a_ref, sout_ref, s_scr = refs
    else:
        lg_ref, q_ref, k_ref, v_ref, rg_ref, a_ref, sout_ref, s_scr = refs
    row = lax.broadcasted_iota(jnp.int32, (cc, cc), 0)
    col = lax.broadcasted_iota(jnp.int32, (cc, cc), 1)
    diff = (row - col).astype(_F32)
    idx = lax.broadcasted_iota(jnp.int32, (cc, 1), 0).astype(_F32)
    for hh in range(hb):
        lg = lg_ref[hh][:, :1]
        dmask = jnp.where(diff >= 0, jnp.exp(lg * jnp.maximum(diff, 0.0)), 0.0)
        dq = jnp.exp(lg * (idx + 1.0))
        dk = jnp.exp(lg * (cc - 1.0 - idx))
        ds = jnp.exp(lg * float(cc))
        cols = slice(hh * RET_DK, (hh + 1) * RET_DK)
        if has_state:
            s_scr[...] = s0_ref[0, hh]
        else:
            s_scr[...] = jnp.zeros_like(s_scr)

        def body(c, carry):
            r = pl.ds(pl.multiple_of(c * cc, cc), cc)
            q = q_ref[r, cols]
            k = k_ref[r, cols]
            v = v_ref[r, cols]
            state = s_scr[...]
            scores = lax.dot_general(q, k, (((1,), (1,)), ((), ())), preferred_element_type=_F32) * dmask
            y = jnp.dot(scores.astype(_BF16), v, preferred_element_type=_F32)
            y = y + jnp.dot(q, state.astype(_BF16), preferred_element_type=_F32) * dq
            k_dec = (k.astype(_F32) * dk).astype(_BF16)
            s_scr[...] = state * ds + lax.dot_general(
                k_dec, v, (((0,), (0,)), ((), ())), preferred_element_type=_F32)
            mu = jnp.mean(y, axis=-1, keepdims=True)
            dev = y - mu
            var = jnp.mean(dev * dev, axis=-1, keepdims=True)
            yn = dev * lax.rsqrt(var + EPS)
            g = rg_ref[r, cols]
            a_ref[r, cols] = (g * jax.nn.sigmoid(g) * yn).astype(_BF16)
            return carry

        lax.fori_loop(0, n_chunks, body, 0)
        sout_ref[0, hh] = s_scr[...]


def _retention(qkv, rg, state, b, t):
    m = qkv.shape[0]
    cc = min(_RET_CHUNK, t)
    assert t % cc == 0 and cc % min(CHUNK, t) == 0
    hb = RET_HEADS if t * RET_QK * 2 <= (1 << 20) else 1
    nhb = RET_HEADS // hb
    wcols = hb * RET_DK
    h = jnp.arange(RET_HEADS, dtype=_F32)
    lg = jnp.log1p(-jnp.exp2(-5.0 - h))
    lg = jnp.broadcast_to(lg[:, None, None], (RET_HEADS, 1, _LANES))
    has_state = state is not None
    in_specs = [pl.BlockSpec((hb, 1, _LANES), lambda bi, hi: (hi, 0, 0)),
                pl.BlockSpec((t, wcols), lambda bi, hi: (bi, hi)),
                pl.BlockSpec((t, wcols), lambda bi, hi: (bi, nhb + hi)),
                pl.BlockSpec((t, wcols), lambda bi, hi: (bi, 2 * nhb + hi)),
                pl.BlockSpec((t, wcols), lambda bi, hi: (bi, hi))]
    args = [lg, qkv, qkv, qkv, rg]
    s_spec = pl.BlockSpec((1, hb, RET_DK, RET_DV), lambda bi, hi: (bi, hi, 0, 0))
    if has_state:
        in_specs.append(s_spec)
        args.append(state)
    est = 2 * (4 * _nbytes((t, wcols), _BF16) + _nbytes((t, wcols), _F32)
               + 2 * _nbytes((hb, RET_DK, RET_DV), _F32)) + 8 * _nbytes((cc, max(cc, RET_DV)), _F32)
    kern = functools.partial(_retention_kernel, hb=hb, cc=cc, n_chunks=t // cc, has_state=has_state)
    return pl.pallas_call(
        kern,
        grid=(b, nhb),
        in_specs=in_specs,
        out_specs=(pl.BlockSpec((t, wcols), lambda bi, hi: (bi, hi)), s_spec),
        out_shape=(jax.ShapeDtypeStruct((m, RET_V), _BF16),
                   jax.ShapeDtypeStruct((b, RET_HEADS, RET_DK, RET_DV), _F32)),
        scratch_shapes=[pltpu.VMEM((RET_DK, RET_DV), _F32)],
        compiler_params=_cparams(("arbitrary", "arbitrary"), est),
        name="retention",
    )(*args)


def _swa_core(qst, k_even, k_odd, v_even, v_odd, sink_even, sink_odd):
    def probs(kx, sink):
        s = lax.dot_general(qst, kx, (((1,), (1,)), ((), ())), preferred_element_type=_F32) * (SWA_HD ** -0.5)
        mx = jnp.maximum(jnp.max(s, axis=-1, keepdims=True), sink)
        e = jnp.exp(s - mx)
        return (e / (jnp.sum(e, axis=-1, keepdims=True) + jnp.exp(sink - mx))).astype(_BF16)

    p_even = probs(k_even, sink_even)
    p_odd = probs(k_odd, sink_odd)
    return (jnp.dot(p_even, v_even, preferred_element_type=_F32)
            + jnp.dot(p_odd, v_odd, preferred_element_type=_F32))


def _swa_prompt_kernel(q_ref, kv_ref, sink_ref, o_ref, *, n_chunks):
    sink_even = sink_ref[0, 0]
    sink_odd = sink_ref[0, 1]
    L = CHUNK

    def chunk(j, nwin):
        r = pl.ds(j * L, L) if isinstance(j, int) else pl.ds(pl.multiple_of(j * L, L), L)
        start = (j - (nwin - 1)) * L
        kr = pl.ds(start, nwin * L) if isinstance(j, int) else pl.ds(pl.multiple_of(start, L), nwin * L)
        qst = jnp.concatenate([q_ref[r, p * _LANES:(p + 1) * _LANES] for p in range(SWA_PAIRS)], axis=0)
        o = _swa_core(qst,
                      kv_ref[0, kr, 0:_LANES], kv_ref[0, kr, _LANES:2 * _LANES],
                      kv_ref[0, kr, 2 * _LANES:3 * _LANES], kv_ref[0, kr, 3 * _LANES:4 * _LANES],
                      sink_even, sink_odd)
        for p in range(SWA_PAIRS):
            o_ref[r, p * _LANES:(p + 1) * _LANES] = o[p * L:(p + 1) * L].astype(o_ref.dtype)

    for j in range(min(WINDOW_CHUNKS, n_chunks)):
        chunk(j, j + 1)
    if n_chunks > WINDOW_CHUNKS:
        def body(j, carry):
            chunk(j, WINDOW_CHUNKS + 1)
            return carry
        lax.fori_loop(WINDOW_CHUNKS, n_chunks, body, 0)


def _sink_columns(sinks, rows):
    s = sinks.astype(_F32).reshape(SWA_KV_HEADS, SWA_PAIRS, 2).transpose(0, 2, 1)
    return jnp.repeat(s, rows, axis=-1)[..., None]


def _swa_prompt(sq, kv2, sinks, b, t):
    m = sq.shape[0]
    gw = SWA_GROUP * SWA_HD
    sink_cols = _sink_columns(sinks, CHUNK)
    est = 2 * (2 * _nbytes((t, gw), _BF16) + _nbytes((t, 4 * _LANES), _BF16)) + (4 << 20)
    return pl.pallas_call(
        functools.partial(_swa_prompt_kernel, n_chunks=t // CHUNK),
        grid=(b, SWA_KV_HEADS),
        in_specs=[pl.BlockSpec((t, gw), lambda bi, kv: (bi, kv)),
                  pl.BlockSpec((1, t, 4 * _LANES), lambda bi, kv: (kv, bi, 0)),
                  pl.BlockSpec((1, 2, SWA_PAIRS * CHUNK, 1), lambda bi, kv: (kv, 0, 0, 0))],
        out_specs=pl.BlockSpec((t, gw), lambda bi, kv: (bi, kv)),
        out_shape=jax.ShapeDtypeStruct((m, SWA_Q), _BF16),
        compiler_params=_cparams(("arbitrary", "arbitrary"), est),
        name="swa_prompt",
    )(sq, kv2, sink_cols)


def _swa_cache_kernel(q_ref, kv_ref, kc_ref, vc_ref, sink_ref, o_ref, *, L):
    gw = SWA_GROUP * SWA_HD
    low = lax.broadcasted_iota(jnp.int32, (kc_ref.shape[1], _LANES), 1) < SWA_HD

    def slabs(cache, kv):
        slab = cache[:, _LANES * (kv // 2):_LANES * (kv // 2 + 1)]
        if kv % 2 == 0:
            even = jnp.where(low, slab, 0.0)
            odd = pltpu.roll(even, SWA_HD, 1)
        else:
            odd = jnp.where(low, 0.0, slab)
            even = pltpu.roll(odd, SWA_HD, 1)
        return even.astype(_BF16), odd.astype(_BF16)

    kc = kc_ref[0]
    vc = vc_ref[0]
    for kv in range(SWA_KV_HEADS):
        kc_even, kc_odd = slabs(kc, kv)
        vc_even, vc_odd = slabs(vc, kv)
        new = [kv_ref[kv, :, c * _LANES:(c + 1) * _LANES] for c in range(4)]
        qst = jnp.concatenate(
            [q_ref[:, kv * gw + p * _LANES:kv * gw + (p + 1) * _LANES] for p in range(SWA_PAIRS)], axis=0)
        o = _swa_core(qst,
                      jnp.concatenate([kc_even, new[0]], axis=0), jnp.concatenate([kc_odd, new[1]], axis=0),
                      jnp.concatenate([vc_even, new[2]], axis=0), jnp.concatenate([vc_odd, new[3]], axis=0),
                      sink_ref[kv, 0], sink_ref[kv, 1])
        for p in range(SWA_PAIRS):
            lo = kv * gw + p * _LANES
            o_ref[:, lo:lo + _LANES] = o[p * L:(p + 1) * L].astype(o_ref.dtype)


def _swa_cache(sq, kv2, k_cache, v_cache, sinks, b, t):
    m = sq.shape[0]
    rows = k_cache.shape[1]
    kc = k_cache.reshape(b, rows, SWA_KV)
    vc = v_cache.reshape(b, rows, SWA_KV)
    sink_cols = _sink_columns(sinks, t)
    est = 2 * (2 * _nbytes((t, SWA_Q), _BF16) + _nbytes((SWA_KV_HEADS, t, 4 * _LANES), _BF16)
               + 2 * _nbytes((rows, SWA_KV), _F32)) + (4 << 20)
    return pl.pallas_call(
        functools.partial(_swa_cache_kernel, L=t),
        grid=(b,),
        in_specs=[pl.BlockSpec((t, SWA_Q), lambda bi: (bi, 0)),
                  pl.BlockSpec((SWA_KV_HEADS, t, 4 * _LANES), lambda bi: (0, bi, 0)),
                  pl.BlockSpec((1, rows, SWA_KV), lambda bi: (bi, 0, 0)),
                  pl.BlockSpec((1, rows, SWA_KV), lambda bi: (bi, 0, 0)),
                  pl.BlockSpec((SWA_KV_HEADS, 2, SWA_PAIRS * t, 1), lambda bi: (0, 0, 0, 0))],
        out_specs=pl.BlockSpec((t, SWA_Q), lambda bi: (bi, 0)),
        out_shape=jax.ShapeDtypeStruct((m, SWA_Q), _BF16),
        compiler_params=_cparams(("arbitrary",), est),
        name="swa_cache",
    )(sq, kv2, kc, vc, sink_cols)


def _merge_kernel(a_ref, o_ref, wr_ref, ws_ref, ga_ref, gb_ref, out_ref):
    yr = jnp.dot(a_ref[...], wr_ref[...], preferred_element_type=_F32)
    ys = jnp.dot(o_ref[...], ws_ref[...], preferred_element_type=_F32)
    out_ref[...] = (jax.nn.sigmoid(ga_ref[...]) * yr + jax.nn.sigmoid(gb_ref[...]) * ys).astype(out_ref.dtype)


def _merge(a_ret, o_swa, gab, w_ret_o, w_swa_o):
    m = a_ret.shape[0]
    d = w_ret_o.shape[1]
    tm = min(_TM, m)
    tn = _TN
    nd = d // tn
    est = 2 * (_nbytes((tm, RET_V), _BF16) + _nbytes((tm, SWA_Q), _BF16) + _nbytes((RET_V, tn), _BF16)
               + _nbytes((SWA_Q, tn), _BF16) + 2 * _nbytes((tm, tn), _F32) + _nbytes((tm, tn), _BF16)
               ) + 4 * _nbytes((tm, tn), _F32)
    return pl.pallas_call(
        _merge_kernel,
        grid=(m // tm, nd),
        in_specs=[pl.BlockSpec((tm, RET_V), lambda i, j: (i, 0)),
                  pl.BlockSpec((tm, SWA_Q), lambda i, j: (i, 0)),
                  pl.BlockSpec((RET_V, tn), lambda i, j: (0, j)),
                  pl.BlockSpec((SWA_Q, tn), lambda i, j: (0, j)),
                  pl.BlockSpec((tm, tn), lambda i, j: (i, j)),
                  pl.BlockSpec((tm, tn), lambda i, j: (i, nd + j))],
        out_specs=pl.BlockSpec((tm, tn), lambda i, j: (i, j)),
        out_shape=jax.ShapeDtypeStruct((m, d), _BF16),
        compiler_params=_cparams(("arbitrary", "arbitrary"), est),
        name="merge",
    )(a_ret, o_swa, w_ret_o, w_swa_o, gab, gab)


def _proj_residual_kernel(a_ref, w_ref, x_ref, gt_ref, o_ref):
    f = jnp.dot(a_ref[...], w_ref[...], preferred_element_type=_F32)
    o_ref[...] = x_ref[...] + _per_seq(f, gt_ref[...], lambda v, gt: gt * v)


def _proj_residual(a, w, x, gt, t, tm_cap, name):
    m, k = a.shape
    d = w.shape[1]
    tm = _row_tile(m, t, tm_cap)
    tn = _TN
    est = 2 * (_nbytes((tm, k), _BF16) + _nbytes((k, tn), _BF16) + 2 * _nbytes((tm, tn), _F32)
               ) + 2 * _nbytes((tm, tn), _F32)
    return pl.pallas_call(
        _proj_residual_kernel,
        grid=(m // tm, d // tn),
        in_specs=[pl.BlockSpec((tm, k), lambda i, j: (i, 0)),
                  pl.BlockSpec((k, tn), lambda i, j: (0, j)),
                  pl.BlockSpec((tm, tn), lambda i, j: (i, j)),
                  _mod_spec(tm, t, tn, lambda i, j: j)],
        out_specs=pl.BlockSpec((tm, tn), lambda i, j: (i, j)),
        out_shape=jax.ShapeDtypeStruct((m, d), _F32),
        compiler_params=_cparams(("arbitrary", "arbitrary"), est),
        name=name,
    )(a, w, x, gt)


def _up_kernel(h_ref, wu_ref, wv_ref, wc_ref, bc_ref, cs_ref, g_ref, tail_ref, carry_ref, *, tps):
    i = pl.program_id(0)
    j = pl.program_id(1)
    h = h_ref[...]
    u = jnp.dot(h, wu_ref[...], preferred_element_type=_F32)
    v = jnp.dot(h, wv_ref[...], preferred_element_type=_F32)
    tm, tn = u.shape
    cs = cs_ref[...]
    spt = cs.shape[0]
    rows = tm // spt
    c0 = cs[:, 0:1, :]
    c1 = cs[:, 1:2, :]
    if tps > 1:
        @pl.when(i == 0)
        def _init():
            carry_ref[j] = jnp.zeros((_SUBLANES, tn), _F32)
        first = (i % tps) == 0
        prev = carry_ref[j]
        c0 = jnp.where(first, c0, prev[_SUBLANES - 2:_SUBLANES - 1][None])
        c1 = jnp.where(first, c1, prev[_SUBLANES - 1:_SUBLANES][None])
        carry_ref[j] = u[tm - _SUBLANES:tm]
    wc = wc_ref[...]
    bc = bc_ref[...]
    if spt == 1:
        r = lax.broadcasted_iota(jnp.int32, (tm, tn), 0)
        p1 = jnp.where(r == 0, c1[0], pltpu.roll(u, 1, 0))
        p2 = jnp.where(r == 0, c0[0], jnp.where(r == 1, c1[0], pltpu.roll(u, 2, 0)))
        uc = wc[0:1] * p2 + wc[1:2] * p1 + wc[2:3] * u + bc
        g_ref[...] = (jax.nn.gelu(uc) * v).astype(g_ref.dtype)
        tail_ref[0, 0] = u[tm - _SUBLANES:tm]
    else:
        u3 = u.reshape(spt, rows, tn)
        r = lax.broadcasted_iota(jnp.int32, (spt, rows, tn), 1)
        p1 = jnp.where(r == 0, c1, pltpu.roll(u3, 1, 1))
        p2 = jnp.where(r == 0, c0, jnp.where(r == 1, c1, pltpu.roll(u3, 2, 1)))
        uc = wc[0:1][None] * p2 + wc[1:2][None] * p1 + wc[2:3][None] * u3 + bc[None]
        g_ref[...] = (jax.nn.gelu(uc) * v.reshape(spt, rows, tn)).reshape(tm, tn).astype(g_ref.dtype)
        tail_ref[0] = u3[:, rows - _SUBLANES:rows, :]


def _up_conv_geglu(h, w_up, w_conv, b_conv, conv_state, t):
    m, d = h.shape
    fp = w_up.shape[1] // 2
    tm = _row_tile(m, t, _TM)
    tn = _TN
    nf = fp // tn
    spt, tps = _seq_block(tm, t)
    rows = tm // spt
    assert rows >= _SUBLANES and conv_state.shape[1] == CONV_W - 1
    est = 2 * (_nbytes((tm, d), _BF16) + 2 * _nbytes((d, tn), _BF16) + _nbytes((tm, tn), _BF16)
               + _nbytes((spt, _SUBLANES, tn), _F32)) + _nbytes((nf, _SUBLANES, tn), _F32) + 8 * _nbytes((tm, tn), _F32)
    return pl.pallas_call(
        functools.partial(_up_kernel, tps=tps),
        grid=(m // tm, nf),
        in_specs=[pl.BlockSpec((tm, d), lambda i, j: (i, 0)),
                  pl.BlockSpec((d, tn), lambda i, j: (0, j)),
                  pl.BlockSpec((d, tn), lambda i, j: (0, nf + j)),
                  pl.BlockSpec((CONV_W, tn), lambda i, j: (0, j)),
                  pl.BlockSpec((1, tn), lambda i, j: (0, j)),
                  pl.BlockSpec((spt, CONV_W - 1, tn), lambda i, j: (i // tps, 0, j))],
        out_specs=(pl.BlockSpec((tm, tn), lambda i, j: (i, j)),
                   pl.BlockSpec((1, spt, _SUBLANES, tn), lambda i, j: (i, 0, 0, j))),
        out_shape=(jax.ShapeDtypeStruct((m, fp), _BF16),
                   jax.ShapeDtypeStruct((m // tm, spt, _SUBLANES, fp), _F32)),
        scratch_shapes=[pltpu.VMEM((nf, _SUBLANES, tn), _F32)],
        compiler_params=_cparams(("arbitrary", "arbitrary"), est),
        name="up_conv_geglu",
    )(h, w_up, w_up, w_conv, b_conv.reshape(1, fp), conv_state)


def _layer(x, mod, pos0, ret_state, swa_kc, swa_vc, conv_state, w):
    b, t, d = x.shape
    m = b * t
    x2 = x.reshape(m, d)
    sh1, sc1, gt1, sh2, sc2, gt2 = [v.reshape(b, 1, d) for v in jnp.split(mod, 6, axis=-1)]

    h = _norm_mod(x2, w["g_mix"], sc1, sh1, t)
    qkv, rg, sq, kv2, skv_f, gab = _in_proj(h, w["w_in"], t, pos0)
    a_ret, s_new = _retention(qkv, rg, ret_state, b, t)
    if swa_kc is None:
        o_swa = _swa_prompt(sq, kv2, w["swa_sinks"], b, t)
        keep = min(WINDOW, t)
    else:
        o_swa = _swa_cache(sq, kv2, swa_kc, swa_vc, w["swa_sinks"], b, t)
        keep = t
    merged = _merge(a_ret, o_swa, gab, w["w_ret_o"], w["w_swa_o"])
    x1 = _proj_residual(merged, w["w_out"], x2, gt1, t, _TM, "out_proj")

    h2 = _norm_mod(x1, w["g_ffn"], sc2, sh2, t)
    f = w["b_conv"].shape[0]
    fp = w["w_up"].shape[1] // 2
    if conv_state is None:
        cs = jnp.zeros((b, CONV_W - 1, fp), _F32)
    else:
        cs = jnp.pad(conv_state, ((0, 0), (0, 0), (0, fp - f)))
    g, tails = _up_conv_geglu(h2, w["w_up"], w["w_conv_p"], w["b_conv_p"], cs, t)
    x3 = _proj_residual(g, w["w_down"], x1, gt2, t, _TM_DOWN, "down_proj")
    y = _final_norm(x3, w["g_final"])

    skv = skv_f.reshape(b, t, 2, SWA_KV_HEADS, SWA_HD)[:, t - keep:]
    k_new = skv[:, :, 0]
    v_new = skv[:, :, 1]
    segs = tails.shape[0] * tails.shape[1] // b
    tails = tails.reshape(b, segs, _SUBLANES, fp)
    conv_new = tails[:, segs - 1, _SUBLANES - (CONV_W - 1):, :f]
    return y.reshape(b, t, d), s_new, k_new, v_new, conv_new


def kernel(x_prompt, x_sample, cache_swa_k, cache_swa_v, state_ret, state_conv, c_prompt, c_sample,
           g_mix, g_ffn, w_ada, b_ada, w_in, swa_sinks, w_ret_o, w_swa_o, w_out,
           w_up, w_conv, b_conv, w_down, g_final):
    depth = g_mix.shape[0]
    assert depth == 1, "single-layer stack"
    bp = x_prompt.shape[0]
    bs = x_sample.shape[0]
    d = x_prompt.shape[-1]
    f = w_conv.shape[-1]
    fp = -(-f // _TN) * _TN
    l = 0

    c_all = jnp.concatenate([c_prompt, c_sample], axis=0)
    rows = -(-c_all.shape[0] // 16) * 16
    c_all = jnp.pad(c_all, ((0, rows - c_all.shape[0]), (0, 0)))
    mod = _ada(c_all, w_ada[l], b_ada[l])

    pad_f = lambda a, axis: jnp.pad(a, [(0, fp - f) if ax == axis else (0, 0) for ax in range(a.ndim)])
    w = {
        "g_mix": g_mix[l], "g_ffn": g_ffn[l], "g_final": g_final, "swa_sinks": swa_sinks[l],
        "w_in": w_in[l].astype(_BF16),
        "w_ret_o": w_ret_o[l].astype(_BF16),
        "w_swa_o": w_swa_o[l].astype(_BF16),
        "w_out": w_out[l].astype(_BF16),
        "w_up": jnp.concatenate([pad_f(w_up[l][:, :f], 1), pad_f(w_up[l][:, f:], 1)], axis=1).astype(_BF16),
        "w_down": pad_f(w_down[l], 0).astype(_BF16),
        "w_conv_p": pad_f(w_conv[l], 1), "b_conv_p": pad_f(b_conv[l], 0), "b_conv": b_conv[l],
    }

    yp, sp, kp, vp, cp = _layer(x_prompt, mod[:bp], 0, None, None, None, None, w)
    ys, ss, ks, vs, cs = _layer(x_sample, mod[bp:bp + bs], PAST_LEN, state_ret[l], cache_swa_k[l],
                                cache_swa_v[l], state_conv[l], w)
    stack = lambda a: a[None]
    return (yp, ys, stack(sp), stack(kp), stack(vp), stack(cp), stack(ss), stack(ks), stack(vs), stack(cs))
```

```python
import functools

import jax
import jax.numpy as jnp
import numpy as np
from jax import lax
from jax.experimental import pallas as pl
from jax.experimental.pallas import tpu as pltpu

_F32 = jnp.float32
_BF16 = jnp.bfloat16

CHUNK = 64
RET_HEADS = 8
RET_DK = 256
RET_DV = 256
RET_QK = RET_HEADS * RET_DK
RET_V = RET_HEADS * RET_DV
ROPE_BASE = 10000.0
SWA_HEADS = 32
SWA_KV_HEADS = 4
SWA_HD = 64
SWA_GROUP = SWA_HEADS // SWA_KV_HEADS
SWA_PAIRS = SWA_GROUP // 2
SWA_Q = SWA_HEADS * SWA_HD
SWA_KV = SWA_KV_HEADS * SWA_HD
WINDOW = 128
WINDOW_CHUNKS = WINDOW // CHUNK
CONV_W = 3
EPS = 1e-6
NEG_INF = -1e30
PAST_LEN = 2048

_V7X_VMEM_BYTES = 64 * 1024 * 1024
_LANES = 128
_SUBLANES = 8

_TM = 1024
_TM_DOWN = 512
_TM_NORM = 256
_TN = 2 * SWA_KV
_TN_ADA = 512
_RET_CHUNK = 256
_SWA_QB = 2
_TN_FF = 256


def _cparams(sem, vmem_estimate):
    limit = min(int(vmem_estimate * 1.15) + (4 << 20), _V7X_VMEM_BYTES - (6 << 20))
    return pltpu.CompilerParams(dimension_semantics=sem, vmem_limit_bytes=limit)


def _nbytes(shape, dtype):
    return int(np.prod(shape)) * jnp.dtype(dtype).itemsize


def _row_tile(m, t, cap):
    tm = min(cap, m)
    assert m % tm == 0 and (t % tm == 0 or tm % t == 0), (m, t, tm)
    return tm


def _seq_block(tm, t):
    return max(1, tm // t), max(1, t // tm)


def _mod_spec(tm, t, tn, col):
    spt, tps = _seq_block(tm, t)
    return pl.BlockSpec((spt, 1, tn), lambda i, j: (i // tps, 0, col(i, j)))


def _per_seq(val, mod, fn):
    spt = mod.shape[0]
    if spt == 1:
        return fn(val, mod[0])
    tm, n = val.shape
    return fn(val.reshape(spt, tm // spt, n), mod).reshape(tm, n)


def _ada_kernel(c_ref, w_ref, b_ref, o_ref):
    c = c_ref[...]
    a = (c * jax.nn.sigmoid(c)).astype(_BF16)
    o_ref[...] = jnp.dot(a, w_ref[...].astype(_BF16), preferred_element_type=_F32) + b_ref[...]


def _ada(c, w_ada, b_ada):
    rows, d = c.shape
    n = w_ada.shape[1]
    tn = _TN_ADA
    est = 2 * (_nbytes((rows, d), _F32) + _nbytes((d, tn), _F32) + _nbytes((rows, tn), _F32)) + _nbytes((d, tn), _BF16)
    return pl.pallas_call(
        _ada_kernel,
        grid=(n // tn,),
        in_specs=[pl.BlockSpec((rows, d), lambda j: (0, 0)),
                  pl.BlockSpec((d, tn), lambda j: (0, j)),
                  pl.BlockSpec((1, tn), lambda j: (0, j))],
        out_specs=pl.BlockSpec((rows, tn), lambda j: (0, j)),
        out_shape=jax.ShapeDtypeStruct((rows, n), _F32),
        compiler_params=_cparams(("arbitrary",), est),
        name="ada_mod",
    )(c, w_ada, b_ada.reshape(1, n))


def _rms(x, g):
    y = x * lax.rsqrt(jnp.mean(x * x, axis=-1, keepdims=True) + EPS)
    return y * g


def _norm_mod_kernel(x_ref, g_ref, sc_ref, sh_ref, o_ref):
    y = _rms(x_ref[...], g_ref[...])
    h = _per_seq(y, sc_ref[...], lambda v, sc: v * (1.0 + sc))
    h = _per_seq(h, sh_ref[...], lambda v, sh: v + sh)
    o_ref[...] = h.astype(o_ref.dtype)


def _norm_mod(x, g, sc, sh, t):
    m, d = x.shape
    tm = _row_tile(m, t, _TM_NORM)
    est = 2 * (_nbytes((tm, d), _F32) + _nbytes((tm, d), _BF16)) + 3 * _nbytes((tm, d), _F32)
    return pl.pallas_call(
        _norm_mod_kernel,
        grid=(m // tm, 1),
        in_specs=[pl.BlockSpec((tm, d), lambda i, j: (i, 0)),
                  pl.BlockSpec((1, d), lambda i, j: (0, 0)),
                  _mod_spec(tm, t, d, lambda i, j: 0),
                  _mod_spec(tm, t, d, lambda i, j: 0)],
        out_specs=pl.BlockSpec((tm, d), lambda i, j: (i, 0)),
        out_shape=jax.ShapeDtypeStruct((m, d), _BF16),
        compiler_params=_cparams(("arbitrary", "arbitrary"), est),
        name="norm_mod",
    )(x, g.reshape(1, d), sc, sh)


def _final_norm_kernel(x_ref, g_ref, o_ref):
    o_ref[...] = _rms(x_ref[...], g_ref[...])


def _final_norm(x, g):
    m, d = x.shape
    tm = min(_TM_NORM, m)
    est = 4 * _nbytes((tm, d), _F32) + 2 * _nbytes((tm, d), _F32)
    return pl.pallas_call(
        _final_norm_kernel,
        grid=(m // tm,),
        in_specs=[pl.BlockSpec((tm, d), lambda i: (i, 0)),
                  pl.BlockSpec((1, d), lambda i: (0, 0))],
        out_specs=pl.BlockSpec((tm, d), lambda i: (i, 0)),
        out_shape=jax.ShapeDtypeStruct((m, d), _F32),
        compiler_params=_cparams(("arbitrary",), est),
        name="final_norm",
    )(x, g.reshape(1, d))


_N_RET_TILES = RET_QK // _TN
_N_SQ_TILES = SWA_Q // _TN
_J_V = 2 * _N_RET_TILES
_J_RG = 3 * _N_RET_TILES
_J_SQ = 4 * _N_RET_TILES
_J_SKV = _J_SQ + _N_SQ_TILES
_J_GAB = _J_SKV + 1


def _pair_slabs(slab, parity, fill):
    low = lax.broadcasted_iota(jnp.int32, slab.shape, 1) < SWA_HD
    if parity == 0:
        even = jnp.where(low, slab, 0.0)
        odd = pltpu.roll(even, SWA_HD, 1)
    else:
        odd = jnp.where(low, 0.0, slab)
        even = pltpu.roll(odd, SWA_HD, 1)
    if fill == 0.0:
        return even, odd
    return jnp.where(low, even, fill), jnp.where(low, fill, odd)


def _in_proj_kernel(h_ref, w_ref, cos_ref, sin_ref,
                    qkv_ref, rg_ref, sq_ref, kv2_ref, skvf_ref, gab_ref):
    j = pl.program_id(1)
    acc = jnp.dot(h_ref[...], w_ref[...], preferred_element_type=_F32)
    half = RET_DK // 2

    @pl.when(j < _J_V)
    def _rope():
        c = cos_ref[...]
        s = sin_ref[...]
        scale = jnp.where(j >= _N_RET_TILES, RET_DK ** -0.5, 1.0).astype(_F32)
        for hh in range(_TN // RET_DK):
            lo = hh * RET_DK
            x1 = acc[:, lo:lo + half]
            x2 = acc[:, lo + half:lo + RET_DK]
            qkv_ref[:, lo:lo + half] = ((x1 * c - x2 * s) * scale).astype(_BF16)
            qkv_ref[:, lo + half:lo + RET_DK] = ((x2 * c + x1 * s) * scale).astype(_BF16)

    @pl.when((j >= _J_V) & (j < _J_RG))
    def _ret_v():
        qkv_ref[...] = acc.astype(_BF16)

    @pl.when((j >= _J_RG) & (j < _J_SQ))
    def _ret_gate():
        rg_ref[...] = acc

    @pl.when((j >= _J_SQ) & (j < _J_SKV))
    def _swa_q():
        sq_ref[...] = acc.astype(_BF16)

    @pl.when(j == _J_SKV)
    def _swa_kv():
        skvf_ref[...] = acc
        for kv in range(SWA_KV_HEADS):
            for part, base in ((0, 0), (1, SWA_KV)):
                lo = base + _LANES * (kv // 2)
                even, odd = _pair_slabs(acc[:, lo:lo + _LANES], kv % 2, fill=float(part))
                col = 2 * part * _LANES
                kv2_ref[kv, :, col:col + _LANES] = even.astype(_BF16)
                kv2_ref[kv, :, col + _LANES:col + 2 * _LANES] = odd.astype(_BF16)

    @pl.when(j >= _J_GAB)
    def _gates():
        gab_ref[...] = acc


def _rope_tables(t, pos0, tm):
    half = RET_DK // 2
    inv = ROPE_BASE ** (-jnp.arange(half, dtype=_F32) * (2.0 / RET_DK))
    pos = (pos0 + jnp.arange(t, dtype=jnp.int32)).astype(_F32)
    ang = pos[:, None] * inv[None, :]
    reps = max(1, tm // t)
    return jnp.tile(jnp.cos(ang), (reps, 1)), jnp.tile(jnp.sin(ang), (reps, 1))


def _in_proj(h, w_in, t, pos0):
    m, d = h.shape
    n_in = w_in.shape[1]
    tm = _row_tile(m, t, _TM)
    tn = _TN
    nj = n_in // tn
    assert n_in % tn == 0 and nj - _J_GAB == 2 * d // tn, (n_in, d)
    cos, sin = _rope_tables(t, pos0, tm)
    tab_blocks = cos.shape[0] // tm
    half = RET_DK // 2

    def clamp(lo, n):
        return lambda i, j: (i, jnp.clip(j - lo, 0, n - 1))

    out_shape = (
        jax.ShapeDtypeStruct((m, 3 * RET_QK), _BF16),
        jax.ShapeDtypeStruct((m, RET_V), _F32),
        jax.ShapeDtypeStruct((m, SWA_Q), _BF16),
        jax.ShapeDtypeStruct((SWA_KV_HEADS, m, 4 * _LANES), _BF16),
        jax.ShapeDtypeStruct((m, 2 * SWA_KV), _F32),
        jax.ShapeDtypeStruct((m, 2 * d), _F32),
    )
    out_specs = (
        pl.BlockSpec((tm, tn), clamp(0, _J_RG)),
        pl.BlockSpec((tm, tn), clamp(_J_RG, _N_RET_TILES)),
        pl.BlockSpec((tm, tn), clamp(_J_SQ, _N_SQ_TILES)),
        pl.BlockSpec((SWA_KV_HEADS, tm, 4 * _LANES), lambda i, j: (0, i, 0)),
        pl.BlockSpec((tm, tn), lambda i, j: (i, 0)),
        pl.BlockSpec((tm, tn), clamp(_J_GAB, nj - _J_GAB)),
    )
    est = 2 * (_nbytes((tm, d), _BF16) + _nbytes((d, tn), _BF16) + 2 * _nbytes((tm, half), _F32)
               + 3 * _nbytes((tm, tn), _BF16) + 3 * _nbytes((tm, tn), _F32)) + 2 * _nbytes((tm, tn), _F32)
    return pl.pallas_call(
        _in_proj_kernel,
        grid=(m // tm, nj),
        in_specs=[pl.BlockSpec((tm, d), lambda i, j: (i, 0)),
                  pl.BlockSpec((d, tn), lambda i, j: (0, j)),
                  pl.BlockSpec((tm, half), lambda i, j: (i % tab_blocks, 0)),
                  pl.BlockSpec((tm, half), lambda i, j: (i % tab_blocks, 0))],
        out_specs=out_specs,
        out_shape=out_shape,
        compiler_params=_cparams(("arbitrary", "arbitrary"), est),
        name="in_proj",
    )(h, w_in, cos, sin)


def _retention_kernel(*refs, hb, cc, n_chunks, has_state):
    if has_state:
        lg_ref, q_ref, k_ref, v_ref, rg_ref, s0_ref, a_ref, sout_ref, s_scr = refs
    else:
        lg_ref, q_ref, k_ref, v_ref, rg_ref, a_ref, sout_ref, s_scr = refs
    row = lax.broadcasted_iota(jnp.int32, (cc, cc), 0)
    col = lax.broadcasted_iota(jnp.int32, (cc, cc), 1)
    diff = (row - col).astype(_F32)
    idx = lax.broadcasted_iota(jnp.int32, (cc, 1), 0).astype(_F32)
    for hh in range(hb):
        lg = lg_ref[hh][:, :1]
        dmask = jnp.where(diff >= 0, jnp.exp(lg * jnp.maximum(diff, 0.0)), 0.0)
        dq = jnp.exp(lg * (idx + 1.0))
        dk = jnp.exp(lg * (cc - 1.0 - idx))
        ds = jnp.exp(lg * float(cc))
        cols = slice(hh * RET_DK, (hh + 1) * RET_DK)
        if has_state:
            s_scr[...] = s0_ref[0, hh]
        else:
            s_scr[...] = jnp.zeros_like(s_scr)

        def body(c, carry):
            r = pl.ds(pl.multiple_of(c * cc, cc), cc)
            q = q_ref[r, cols]
            k = k_ref[r, cols]
            v = v_ref[r, cols]
            state = s_scr[...]
            scores = lax.dot_general(q, k, (((1,), (1,)), ((), ())), preferred_element_type=_F32) * dmask
            y = jnp.dot(scores.astype(_BF16), v, preferred_element_type=_F32)
            y = y + jnp.dot(q, state.astype(_BF16), preferred_element_type=_F32) * dq
            k_dec = (k.astype(_F32) * dk).astype(_BF16)
            s_scr[...] = state * ds + lax.dot_general(
                k_dec, v, (((0,), (0,)), ((), ())), preferred_element_type=_F32)
            mu = jnp.mean(y, axis=-1, keepdims=True)
            dev = y - mu
            var = jnp.mean(dev * dev, axis=-1, keepdims=True)
            yn = dev * lax.rsqrt(var + EPS)
            g = rg_ref[r, cols]
            a_ref[r, cols] = (g * jax.nn.sigmoid(g) * yn).astype(_BF16)
            return carry

        lax.fori_loop(0, n_chunks, body, 0)
        sout_ref[0, hh] = s_scr[...]


def _retention(qkv, rg, state, b, t):
    m = qkv.shape[0]
    cc = min(_RET_CHUNK, t)
    assert t % cc == 0 and cc % min(CHUNK, t) == 0
    hb = RET_HEADS if t * RET_QK * 2 <= (1 << 20) else 1
    nhb = RET_HEADS // hb
    wcols = hb * RET_DK
    h = jnp.arange(RET_HEADS, dtype=_F32)
    lg = jnp.log1p(-jnp.exp2(-5.0 - h))
    lg = jnp.broadcast_to(lg[:, None, None], (RET_HEADS, 1, _LANES))
    has_state = state is not None
    in_specs = [pl.BlockSpec((hb, 1, _LANES), lambda bi, hi: (hi, 0, 0)),
                pl.BlockSpec((t, wcols), lambda bi, hi: (bi, hi)),
                pl.BlockSpec((t, wcols), lambda bi, hi: (bi, nhb + hi)),
                pl.BlockSpec((t, wcols), lambda bi, hi: (bi, 2 * nhb + hi)),
                pl.BlockSpec((t, wcols), lambda bi, hi: (bi, hi))]
    args = [lg, qkv, qkv, qkv, rg]
    s_spec = pl.BlockSpec((1, hb, RET_DK, RET_DV), lambda bi, hi: (bi, hi, 0, 0))
    if has_state:
        in_specs.append(s_spec)
        args.append(state)
    est = 2 * (4 * _nbytes((t, wcols), _BF16) + _nbytes((t, wcols), _F32)
               + 2 * _nbytes((hb, RET_DK, RET_DV), _F32)) + 8 * _nbytes((cc, max(cc, RET_DV)), _F32)
    kern = functools.partial(_retention_kernel, hb=hb, cc=cc, n_chunks=t // cc, has_state=has_state)
    return pl.pallas_call(
        kern,
        grid=(b, nhb),
        in_specs=in_specs,
        out_specs=(pl.BlockSpec((t, wcols), lambda bi, hi: (bi, hi)), s_spec),
        out_shape=(jax.ShapeDtypeStruct((m, RET_V), _BF16),
                   jax.ShapeDtypeStruct((b, RET_HEADS, RET_DK, RET_DV), _F32)),
        scratch_shapes=[pltpu.VMEM((RET_DK, RET_DV), _F32)],
        compiler_params=_cparams(("arbitrary", "arbitrary"), est),
        name="retention",
    )(*args)


def _swa_core(qst, k_even, k_odd, v_even, v_odd, sink_even, sink_odd, valid):
    def weights(kx, sink):
        s = lax.dot_general(qst, kx, (((1,), (1,)), ((), ())), preferred_element_type=_F32) * (SWA_HD ** -0.5)
        if valid is not None:
            s = jnp.where(valid, s, NEG_INF)
        mx = jnp.maximum(jnp.max(s, axis=-1, keepdims=True), sink)
        return jnp.exp(s - mx).astype(_BF16), jnp.exp(sink - mx)

    e_even, t_even = weights(k_even, sink_even)
    e_odd, t_odd = weights(k_odd, sink_odd)
    o_even = jnp.dot(e_even, v_even, preferred_element_type=_F32)
    o_odd = jnp.dot(e_odd, v_odd, preferred_element_type=_F32)
    low = lax.broadcasted_iota(jnp.int32, o_even.shape, 1) < SWA_HD
    num = jnp.where(low, o_even, o_odd)
    den = jnp.where(low, pltpu.roll(o_even, SWA_HD, 1) + t_even, pltpu.roll(o_odd, SWA_HD, 1) + t_odd)
    return num / den


def _swa_prompt_kernel(q_ref, kv_ref, sink_ref, o_ref, *, n_blocks, unroll):
    sink_even = sink_ref[0, 0]
    sink_odd = sink_ref[0, 1]
    rows = _SWA_QB * CHUNK
    win = _SWA_QB + WINDOW_CHUNKS

    def valid_mask(n_key_chunks, first):
        shape = (SWA_PAIRS * rows, n_key_chunks * CHUNK)
        qc = (lax.broadcasted_iota(jnp.int32, shape, 0) // CHUNK) % _SWA_QB
        kc = lax.broadcasted_iota(jnp.int32, shape, 1) // CHUNK
        if first:
            return kc <= qc
        return (kc >= qc) & (kc <= qc + WINDOW_CHUNKS)

    def block(r, kr, valid):
        qst = jnp.concatenate([q_ref[r, p * _LANES:(p + 1) * _LANES] for p in range(SWA_PAIRS)], axis=0)
        o = _swa_core(qst,
                      kv_ref[0, kr, 0:_LANES], kv_ref[0, kr, _LANES:2 * _LANES],
                      kv_ref[0, kr, 2 * _LANES:3 * _LANES], kv_ref[0, kr, 3 * _LANES:4 * _LANES],
                      sink_even, sink_odd, valid)
        for p in range(SWA_PAIRS):
            o_ref[r, p * _LANES:(p + 1) * _LANES] = o[p * rows:(p + 1) * rows].astype(o_ref.dtype)

    block(pl.ds(0, rows), pl.ds(0, rows), valid_mask(_SWA_QB, True))
    if n_blocks > 1:
        valid = valid_mask(win, False)

        def body(b, carry):
            r = pl.ds(pl.multiple_of(b * rows, rows), rows)
            kr = pl.ds(pl.multiple_of(b * rows - WINDOW_CHUNKS * CHUNK, CHUNK), win * CHUNK)
            block(r, kr, valid)
            return carry

        lax.fori_loop(1, n_blocks, body, 0, unroll=unroll)


def _sink_columns(sinks, rows):
    s = sinks.astype(_F32).reshape(SWA_KV_HEADS, SWA_PAIRS, 2).transpose(0, 2, 1)
    return jnp.repeat(s, rows, axis=-1)[..., None]


def _swa_prompt(sq, kv2, sinks, b, t):
    m = sq.shape[0]
    gw = SWA_GROUP * SWA_HD
    rows = _SWA_QB * CHUNK
    assert t % rows == 0 and WINDOW_CHUNKS == _SWA_QB
    n_blocks = t // rows
    unroll = next(u for u in (3, 2, 1) if (n_blocks - 1) % u == 0)
    sink_cols = _sink_columns(sinks, rows)
    est = 2 * (2 * _nbytes((t, gw), _BF16) + _nbytes((t, 4 * _LANES), _BF16)) + (8 << 20)
    return pl.pallas_call(
        functools.partial(_swa_prompt_kernel, n_blocks=n_blocks, unroll=unroll),
        grid=(b, SWA_KV_HEADS),
        in_specs=[pl.BlockSpec((t, gw), lambda bi, kv: (bi, kv)),
                  pl.BlockSpec((1, t, 4 * _LANES), lambda bi, kv: (kv, bi, 0)),
                  pl.BlockSpec((1, 2, SWA_PAIRS * rows, 1), lambda bi, kv: (kv, 0, 0, 0))],
        out_specs=pl.BlockSpec((t, gw), lambda bi, kv: (bi, kv)),
        out_shape=jax.ShapeDtypeStruct((m, SWA_Q), _BF16),
        compiler_params=_cparams(("arbitrary", "arbitrary"), est),
        name="swa_prompt",
    )(sq, kv2, sink_cols)


def _swa_cache_kernel(q_ref, kv_ref, kc_ref, vc_ref, sink_ref, o_ref, *, L):
    gw = SWA_GROUP * SWA_HD
    kc = kc_ref[0]
    vc = vc_ref[0]
    for kv in range(SWA_KV_HEADS):
        lanes = slice(_LANES * (kv // 2), _LANES * (kv // 2 + 1))
        kc_even, kc_odd = _pair_slabs(kc[:, lanes], kv % 2, fill=0.0)
        vc_even, vc_odd = _pair_slabs(vc[:, lanes], kv % 2, fill=1.0)
        new = [kv_ref[kv, :, c * _LANES:(c + 1) * _LANES] for c in range(4)]
        qst = jnp.concatenate(
            [q_ref[:, kv * gw + p * _LANES:kv * gw + (p + 1) * _LANES] for p in range(SWA_PAIRS)], axis=0)
        o = _swa_core(qst,
                      jnp.concatenate([kc_even.astype(_BF16), new[0]], axis=0),
                      jnp.concatenate([kc_odd.astype(_BF16), new[1]], axis=0),
                      jnp.concatenate([vc_even.astype(_BF16), new[2]], axis=0),
                      jnp.concatenate([vc_odd.astype(_BF16), new[3]], axis=0),
                      sink_ref[kv, 0], sink_ref[kv, 1], None)
        for p in range(SWA_PAIRS):
            lo = kv * gw + p * _LANES
            o_ref[:, lo:lo + _LANES] = o[p * L:(p + 1) * L].astype(o_ref.dtype)


def _swa_cache(sq, kv2, k_cache, v_cache, sinks, b, t):
    m = sq.shape[0]
    rows = k_cache.shape[1]
    kc = k_cache.reshape(b, rows, SWA_KV)
    vc = v_cache.reshape(b, rows, SWA_KV)
    sink_cols = _sink_columns(sinks, t)
    est = 2 * (2 * _nbytes((t, SWA_Q), _BF16) + _nbytes((SWA_KV_HEADS, t, 4 * _LANES), _BF16)
               + 2 * _nbytes((rows, SWA_KV), _F32)) + (4 << 20)
    return pl.pallas_call(
        functools.partial(_swa_cache_kernel, L=t),
        grid=(b,),
        in_specs=[pl.BlockSpec((t, SWA_Q), lambda bi: (bi, 0)),
                  pl.BlockSpec((SWA_KV_HEADS, t, 4 * _LANES), lambda bi: (0, bi, 0)),
                  pl.BlockSpec((1, rows, SWA_KV), lambda bi: (bi, 0, 0)),
                  pl.BlockSpec((1, rows, SWA_KV), lambda bi: (bi, 0, 0)),
                  pl.BlockSpec((SWA_KV_HEADS, 2, SWA_PAIRS * t, 1), lambda bi: (0, 0, 0, 0))],
        out_specs=pl.BlockSpec((t, SWA_Q), lambda bi: (bi, 0)),
        out_shape=jax.ShapeDtypeStruct((m, SWA_Q), _BF16),
        compiler_params=_cparams(("arbitrary",), est),
        name="swa_cache",
    )(sq, kv2, kc, vc, sink_cols)


def _merge_kernel(a_ref, o_ref, wr_ref, ws_ref, ga_ref, gb_ref, out_ref):
    yr = jnp.dot(a_ref[...], wr_ref[...], preferred_element_type=_F32)
    ys = jnp.dot(o_ref[...], ws_ref[...], preferred_element_type=_F32)
    out_ref[...] = (jax.nn.sigmoid(ga_ref[...]) * yr + jax.nn.sigmoid(gb_ref[...]) * ys).astype(out_ref.dtype)


def _merge(a_ret, o_swa, gab, w_ret_o, w_swa_o):
    m = a_ret.shape[0]
    d = w_ret_o.shape[1]
    tm = min(_TM, m)
    tn = _TN
    nd = d // tn
    est = 2 * (_nbytes((tm, RET_V), _BF16) + _nbytes((tm, SWA_Q), _BF16) + _nbytes((RET_V, tn), _BF16)
               + _nbytes((SWA_Q, tn), _BF16) + 2 * _nbytes((tm, tn), _F32) + _nbytes((tm, tn), _BF16)
               ) + 4 * _nbytes((tm, tn), _F32)
    return pl.pallas_call(
        _merge_kernel,
        grid=(m // tm, nd),
        in_specs=[pl.BlockSpec((tm, RET_V), lambda i, j: (i, 0)),
                  pl.BlockSpec((tm, SWA_Q), lambda i, j: (i, 0)),
                  pl.BlockSpec((RET_V, tn), lambda i, j: (0, j)),
                  pl.BlockSpec((SWA_Q, tn), lambda i, j: (0, j)),
                  pl.BlockSpec((tm, tn), lambda i, j: (i, j)),
                  pl.BlockSpec((tm, tn), lambda i, j: (i, nd + j))],
        out_specs=pl.BlockSpec((tm, tn), lambda i, j: (i, j)),
        out_shape=jax.ShapeDtypeStruct((m, d), _BF16),
        compiler_params=_cparams(("arbitrary", "arbitrary"), est),
        name="merge",
    )(a_ret, o_swa, w_ret_o, w_swa_o, gab, gab)


def _proj_residual_kernel(a_ref, w_ref, x_ref, gt_ref, o_ref):
    f = jnp.dot(a_ref[...], w_ref[...], preferred_element_type=_F32)
    o_ref[...] = x_ref[...] + _per_seq(f, gt_ref[...], lambda v, gt: gt * v)


def _proj_residual(a, w, x, gt, t, tm_cap, name):
    m, k = a.shape
    d = w.shape[1]
    tm = _row_tile(m, t, tm_cap)
    tn = _TN
    est = 2 * (_nbytes((tm, k), _BF16) + _nbytes((k, tn), _BF16) + 2 * _nbytes((tm, tn), _F32)
               ) + 2 * _nbytes((tm, tn), _F32)
    return pl.pallas_call(
        _proj_residual_kernel,
        grid=(m // tm, d // tn),
        in_specs=[pl.BlockSpec((tm, k), lambda i, j: (i, 0)),
                  pl.BlockSpec((k, tn), lambda i, j: (0, j)),
                  pl.BlockSpec((tm, tn), lambda i, j: (i, j)),
                  _mod_spec(tm, t, tn, lambda i, j: j)],
        out_specs=pl.BlockSpec((tm, tn), lambda i, j: (i, j)),
        out_shape=jax.ShapeDtypeStruct((m, d), _F32),
        compiler_params=_cparams(("arbitrary", "arbitrary"), est),
        name=name,
    )(a, w, x, gt)


def _up_kernel(h_ref, wu_ref, wv_ref, wc_ref, bc_ref, cs_ref, g_ref, tail_ref, carry_ref, *, tps):
    i = pl.program_id(0)
    j = pl.program_id(1)
    h = h_ref[...]
    u = jnp.dot(h, wu_ref[...], preferred_element_type=_F32)
    v = jnp.dot(h, wv_ref[...], preferred_element_type=_F32)
    tm, tn = u.shape
    cs = cs_ref[...]
    spt = cs.shape[0]
    rows = tm // spt
    c0 = cs[:, 0:1, :]
    c1 = cs[:, 1:2, :]
    if tps > 1:
        @pl.when(i == 0)
        def _init():
            carry_ref[j] = jnp.zeros((_SUBLANES, tn), _F32)
        first = (i % tps) == 0
        prev = carry_ref[j]
        c0 = jnp.where(first, c0, prev[_SUBLANES - 2:_SUBLANES - 1][None])
        c1 = jnp.where(first, c1, prev[_SUBLANES - 1:_SUBLANES][None])
        carry_ref[j] = u[tm - _SUBLANES:tm]
    wc = wc_ref[...]
    bc = bc_ref[...]
    if spt == 1:
        r = lax.broadcasted_iota(jnp.int32, (tm, tn), 0)
        p1 = jnp.where(r == 0, c1[0], pltpu.roll(u, 1, 0))
        p2 = jnp.where(r == 0, c0[0], jnp.where(r == 1, c1[0], pltpu.roll(u, 2, 0)))
        uc = wc[0:1] * p2 + wc[1:2] * p1 + wc[2:3] * u + bc
        g_ref[...] = (jax.nn.gelu(uc) * v).astype(g_ref.dtype)
        tail_ref[0, 0] = u[tm - _SUBLANES:tm]
    else:
        u3 = u.reshape(spt, rows, tn)
        r = lax.broadcasted_iota(jnp.int32, (spt, rows, tn), 1)
        p1 = jnp.where(r == 0, c1, pltpu.roll(u3, 1, 1))
        p2 = jnp.where(r == 0, c0, jnp.where(r == 1, c1, pltpu.roll(u3, 2, 1)))
        uc = wc[0:1][None] * p2 + wc[1:2][None] * p1 + wc[2:3][None] * u3 + bc[None]
        g_ref[...] = (jax.nn.gelu(uc) * v.reshape(spt, rows, tn)).reshape(tm, tn).astype(g_ref.dtype)
        tail_ref[0] = u3[:, rows - _SUBLANES:rows, :]


def _up_conv_geglu(h, w_up, w_conv, b_conv, conv_state, t):
    m, d = h.shape
    fp = w_up.shape[1] // 2
    tm = _row_tile(m, t, _TM)
    tn = _TN_FF
    assert fp % tn == 0, (fp, tn)
    nf = fp // tn
    spt, tps = _seq_block(tm, t)
    rows = tm // spt
    assert rows >= _SUBLANES and conv_state.shape[1] == CONV_W - 1
    est = 2 * (_nbytes((tm, d), _BF16) + 2 * _nbytes((d, tn), _BF16) + _nbytes((tm, tn), _BF16)
               + _nbytes((spt, _SUBLANES, tn), _F32)) + _nbytes((nf, _SUBLANES, tn), _F32) + 8 * _nbytes((tm, tn), _F32)
    return pl.pallas_call(
        functools.partial(_up_kernel, tps=tps),
        grid=(m // tm, nf),
        in_specs=[pl.BlockSpec((tm, d), lambda i, j: (i, 0)),
                  pl.BlockSpec((d, tn), lambda i, j: (0, j)),
                  pl.BlockSpec((d, tn), lambda i, j: (0, nf + j)),
                  pl.BlockSpec((CONV_W, tn), lambda i, j: (0, j)),
                  pl.BlockSpec((1, tn), lambda i, j: (0, j)),
                  pl.BlockSpec((spt, CONV_W - 1, tn), lambda i, j: (i // tps, 0, j))],
        out_specs=(pl.BlockSpec((tm, tn), lambda i, j: (i, j)),
                   pl.BlockSpec((1, spt, _SUBLANES, tn), lambda i, j: (i, 0, 0, j))),
        out_shape=(jax.ShapeDtypeStruct((m, fp), _BF16),
                   jax.ShapeDtypeStruct((m // tm, spt, _SUBLANES, fp), _F32)),
        scratch_shapes=[pltpu.VMEM((nf, _SUBLANES, tn), _F32)],
        compiler_params=_cparams(("arbitrary", "arbitrary"), est),
        name="up_conv_geglu",
    )(h, w_up, w_up, w_conv, b_conv.reshape(1, fp), conv_state)


def _layer(x, mod, pos0, ret_state, swa_kc, swa_vc, conv_state, w):
    b, t, d = x.shape
    m = b * t
    x2 = x.reshape(m, d)
    sh1, sc1, gt1, sh2, sc2, gt2 = [v.reshape(b, 1, d) for v in jnp.split(mod, 6, axis=-1)]

    h = _norm_mod(x2, w["g_mix"], sc1, sh1, t)
    qkv, rg, sq, kv2, skv_f, gab = _in_proj(h, w["w_in"], t, pos0)
    a_ret, s_new = _retention(qkv, rg, ret_state, b, t)
    if swa_kc is None:
        o_swa = _swa_prompt(sq, kv2, w["swa_sinks"], b, t)
        keep = min(WINDOW, t)
    else:
        o_swa = _swa_cache(sq, kv2, swa_kc, swa_vc, w["swa_sinks"], b, t)
        keep = t
    merged = _merge(a_ret, o_swa, gab, w["w_ret_o"], w["w_swa_o"])
    x1 = _proj_residual(merged, w["w_out"], x2, gt1, t, _TM, "out_proj")

    h2 = _norm_mod(x1, w["g_ffn"], sc2, sh2, t)
    f = w["b_conv"].shape[0]
    cs = jnp.zeros((b, CONV_W - 1, f), _F32) if conv_state is None else conv_state
    g, tails = _up_conv_geglu(h2, w["w_up"], w["w_conv"], w["b_conv"], cs, t)
    x3 = _proj_residual(g, w["w_down"], x1, gt2, t, _TM_DOWN, "down_proj")
    y = _final_norm(x3, w["g_final"])

    skv = skv_f.reshape(b, t, 2, SWA_KV_HEADS, SWA_HD)[:, t - keep:]
    k_new = skv[:, :, 0]
    v_new = skv[:, :, 1]
    segs = tails.shape[0] * tails.shape[1] // b
    tails = tails.reshape(b, segs, _SUBLANES, f)
    conv_new = tails[:, segs - 1, _SUBLANES - (CONV_W - 1):, :]
    return y.reshape(b, t, d), s_new, k_new, v_new, conv_new


def kernel(x_prompt, x_sample, cache_swa_k, cache_swa_v, state_ret, state_conv, c_prompt, c_sample,
           g_mix, g_ffn, w_ada, b_ada, w_in, swa_sinks, w_ret_o, w_swa_o, w_out,
           w_up, w_conv, b_conv, w_down, g_final):
    depth = g_mix.shape[0]
    assert depth == 1, "single-layer stack"
    bp = x_prompt.shape[0]
    bs = x_sample.shape[0]
    l = 0

    c_all = jnp.concatenate([c_prompt, c_sample], axis=0)
    rows = -(-c_all.shape[0] // 16) * 16
    c_all = jnp.pad(c_all, ((0, rows - c_all.shape[0]), (0, 0)))
    mod = _ada(c_all, w_ada[l], b_ada[l])

    w = {
        "g_mix": g_mix[l], "g_ffn": g_ffn[l], "g_final": g_final, "swa_sinks": swa_sinks[l],
        "w_in": w_in[l].astype(_BF16),
        "w_ret_o": w_ret_o[l].astype(_BF16),
        "w_swa_o": w_swa_o[l].astype(_BF16),
        "w_out": w_out[l].astype(_BF16),
        "w_up": w_up[l].astype(_BF16),
        "w_down": w_down[l].astype(_BF16),
        "w_conv": w_conv[l], "b_conv": b_conv[l],
    }

    yp, sp, kp, vp, cp = _layer(x_prompt, mod[:bp], 0, None, None, None, None, w)
    ys, ss, ks, vs, cs = _layer(x_sample, mod[bp:bp + bs], PAST_LEN, state_ret[l], cache_swa_k[l],
                                cache_swa_v[l], state_conv[l], w)
    stack = lambda a: a[None]
    return (yp, ys, stack(sp), stack(kp), stack(vp), stack(cp), stack(ss), stack(ks), stack(vs), stack(cs))
```

```python
import functools

import jax
import jax.numpy as jnp
import numpy as np
from jax import lax
from jax.experimental import pallas as pl
from jax.experimental.pallas import tpu as pltpu

_F32 = jnp.float32
_BF16 = jnp.bfloat16

CHUNK = 64
RET_HEADS = 8
RET_DK = 256
RET_DV = 256
RET_QK = RET_HEADS * RET_DK
RET_V = RET_HEADS * RET_DV
ROPE_BASE = 10000.0
SWA_HEADS = 32
SWA_KV_HEADS = 4
SWA_HD = 64
SWA_GROUP = SWA_HEADS // SWA_KV_HEADS
SWA_PAIRS = SWA_GROUP // 2
SWA_Q = SWA_HEADS * SWA_HD
SWA_KV = SWA_KV_HEADS * SWA_HD
WINDOW = 128
WINDOW_CHUNKS = WINDOW // CHUNK
CONV_W = 3
EPS = 1e-6
NEG_INF = -1e30
PAST_LEN = 2048

_V7X_VMEM_BYTES = 64 * 1024 * 1024
_LANES = 128
_SUBLANES = 8

_TM = 1024
_TM_DOWN = 512
_TM_NORM = 256
_TN = 2 * SWA_KV
_TN_EMIT = 256
_TN_ADA = 512
_RET_CHUNK = 256
_SWA_QB = 2
_TN_FF = 256


def _cparams(sem, vmem_estimate):
    limit = min(int(vmem_estimate * 1.15) + (4 << 20), _V7X_VMEM_BYTES - (6 << 20))
    return pltpu.CompilerParams(dimension_semantics=sem, vmem_limit_bytes=limit)


def _nbytes(shape, dtype):
    return int(np.prod(shape)) * jnp.dtype(dtype).itemsize


def _row_tile(m, t, cap):
    tm = min(cap, m)
    assert m % tm == 0 and (t % tm == 0 or tm % t == 0), (m, t, tm)
    return tm


def _seq_block(tm, t):
    return max(1, tm // t), max(1, t // tm)


def _mod_spec(tm, t, tn, col):
    spt, tps = _seq_block(tm, t)
    return pl.BlockSpec((spt, 1, tn), lambda i, j: (i // tps, 0, col(i, j)))


def _per_seq(val, mod, fn):
    spt = mod.shape[0]
    if spt == 1:
        return fn(val, mod[0])
    tm, n = val.shape
    return fn(val.reshape(spt, tm // spt, n), mod).reshape(tm, n)


def _ada_kernel(c_ref, w_ref, b_ref, o_ref):
    c = c_ref[...]
    a = (c * jax.nn.sigmoid(c)).astype(_BF16)
    o_ref[...] = jnp.dot(a, w_ref[...].astype(_BF16), preferred_element_type=_F32) + b_ref[...]


def _ada(c, w_ada, b_ada):
    rows, d = c.shape
    n = w_ada.shape[1]
    tn = _TN_ADA
    est = 2 * (_nbytes((rows, d), _F32) + _nbytes((d, tn), _F32) + _nbytes((rows, tn), _F32)) + _nbytes((d, tn), _BF16)
    return pl.pallas_call(
        _ada_kernel,
        grid=(n // tn,),
        in_specs=[pl.BlockSpec((rows, d), lambda j: (0, 0)),
                  pl.BlockSpec((d, tn), lambda j: (0, j)),
                  pl.BlockSpec((1, tn), lambda j: (0, j))],
        out_specs=pl.BlockSpec((rows, tn), lambda j: (0, j)),
        out_shape=jax.ShapeDtypeStruct((rows, n), _F32),
        compiler_params=_cparams(("arbitrary",), est),
        name="ada_mod",
    )(c, w_ada, b_ada.reshape(1, n))


def _rms(x, g):
    y = x * lax.rsqrt(jnp.mean(x * x, axis=-1, keepdims=True) + EPS)
    return y * g


def _norm_mod_kernel(x_ref, g_ref, sc_ref, sh_ref, o_ref):
    y = _rms(x_ref[...], g_ref[...])
    h = _per_seq(y, sc_ref[...], lambda v, sc: v * (1.0 + sc))
    h = _per_seq(h, sh_ref[...], lambda v, sh: v + sh)
    o_ref[...] = h.astype(o_ref.dtype)


def _norm_mod(x, g, sc, sh, t):
    m, d = x.shape
    tm = _row_tile(m, t, _TM_NORM)
    est = 2 * (_nbytes((tm, d), _F32) + _nbytes((tm, d), _BF16)) + 3 * _nbytes((tm, d), _F32)
    return pl.pallas_call(
        _norm_mod_kernel,
        grid=(m // tm, 1),
        in_specs=[pl.BlockSpec((tm, d), lambda i, j: (i, 0)),
                  pl.BlockSpec((1, d), lambda i, j: (0, 0)),
                  _mod_spec(tm, t, d, lambda i, j: 0),
                  _mod_spec(tm, t, d, lambda i, j: 0)],
        out_specs=pl.BlockSpec((tm, d), lambda i, j: (i, 0)),
        out_shape=jax.ShapeDtypeStruct((m, d), _BF16),
        compiler_params=_cparams(("arbitrary", "arbitrary"), est),
        name="norm_mod",
    )(x, g.reshape(1, d), sc, sh)


def _final_norm_kernel(x_ref, g_ref, o_ref):
    o_ref[...] = _rms(x_ref[...], g_ref[...])


def _final_norm(x, g):
    m, d = x.shape
    tm = min(_TM_NORM, m)
    est = 4 * _nbytes((tm, d), _F32) + 2 * _nbytes((tm, d), _F32)
    return pl.pallas_call(
        _final_norm_kernel,
        grid=(m // tm,),
        in_specs=[pl.BlockSpec((tm, d), lambda i: (i, 0)),
                  pl.BlockSpec((1, d), lambda i: (0, 0))],
        out_specs=pl.BlockSpec((tm, d), lambda i: (i, 0)),
        out_shape=jax.ShapeDtypeStruct((m, d), _F32),
        compiler_params=_cparams(("arbitrary",), est),
        name="final_norm",
    )(x, g.reshape(1, d))


def _pair_slabs(slab, parity, fill):
    low = lax.broadcasted_iota(jnp.int32, slab.shape, 1) < SWA_HD
    if parity == 0:
        even = jnp.where(low, slab, 0.0)
        odd = pltpu.roll(even, SWA_HD, 1)
    else:
        odd = jnp.where(low, 0.0, slab)
        even = pltpu.roll(odd, SWA_HD, 1)
    if fill == 0.0:
        return even, odd
    return jnp.where(low, even, fill), jnp.where(low, fill, odd)


class _InCols:
    def __init__(self, tn):
        assert RET_QK % tn == 0 and SWA_Q % tn == 0 and (2 * SWA_KV) % tn == 0 and tn % SWA_KV == 0
        self.tn = tn
        self.n_ret = RET_QK // tn
        self.v = 2 * self.n_ret
        self.rg = 3 * self.n_ret
        self.sq = 4 * self.n_ret
        self.skv = self.sq + SWA_Q // tn
        self.n_skv = 2 * SWA_KV // tn
        self.gab = self.skv + self.n_skv


def _in_proj_kernel(*refs, cols, emit):
    if emit:
        h_ref, w_ref, cos_ref, sin_ref, qkv_ref, rg_ref, sq_ref, kv2_ref, skvf_ref, gab_ref, wbf_ref = refs
        wbf_ref[...] = w_ref[...].astype(_BF16)
        w = wbf_ref[...]
    else:
        h_ref, w_ref, cos_ref, sin_ref, qkv_ref, rg_ref, sq_ref, kv2_ref, skvf_ref, gab_ref = refs
        w = w_ref[...]
    j = pl.program_id(1)
    acc = jnp.dot(h_ref[...], w, preferred_element_type=_F32)
    half = RET_DK // 2
    tn = cols.tn

    @pl.when(j < cols.v)
    def _rope():
        c = cos_ref[...]
        s = sin_ref[...]
        scale = jnp.where(j >= cols.n_ret, RET_DK ** -0.5, 1.0).astype(_F32)
        for hh in range(tn // RET_DK):
            lo = hh * RET_DK
            x1 = acc[:, lo:lo + half]
            x2 = acc[:, lo + half:lo + RET_DK]
            qkv_ref[:, lo:lo + half] = ((x1 * c - x2 * s) * scale).astype(_BF16)
            qkv_ref[:, lo + half:lo + RET_DK] = ((x2 * c + x1 * s) * scale).astype(_BF16)

    @pl.when((j >= cols.v) & (j < cols.rg))
    def _ret_v():
        qkv_ref[...] = acc.astype(_BF16)

    @pl.when((j >= cols.rg) & (j < cols.sq))
    def _ret_gate():
        rg_ref[...] = acc

    @pl.when((j >= cols.sq) & (j < cols.skv))
    def _swa_q():
        sq_ref[...] = acc.astype(_BF16)

    def slabs(x, part):
        for kv in range(SWA_KV_HEADS):
            lo = _LANES * (kv // 2)
            even, odd = _pair_slabs(x[:, lo:lo + _LANES], kv % 2, fill=float(part))
            col = 2 * part * _LANES
            kv2_ref[kv, :, col:col + _LANES] = even.astype(_BF16)
            kv2_ref[kv, :, col + _LANES:col + 2 * _LANES] = odd.astype(_BF16)

    if cols.n_skv == 1:
        @pl.when(j == cols.skv)
        def _swa_kv():
            skvf_ref[...] = acc
            slabs(acc[:, :SWA_KV], 0)
            slabs(acc[:, SWA_KV:], 1)
    else:
        for part in range(2):
            @pl.when(j == cols.skv + part)
            def _swa_kv_part():
                skvf_ref[...] = acc
                slabs(acc, part)

    @pl.when(j >= cols.gab)
    def _gates():
        gab_ref[...] = acc


def _rope_tables(t, pos0, tm):
    half = RET_DK // 2
    inv = ROPE_BASE ** (-jnp.arange(half, dtype=_F32) * (2.0 / RET_DK))
    pos = (pos0 + jnp.arange(t, dtype=jnp.int32)).astype(_F32)
    ang = pos[:, None] * inv[None, :]
    reps = max(1, tm // t)
    return jnp.tile(jnp.cos(ang), (reps, 1)), jnp.tile(jnp.sin(ang), (reps, 1))


def _in_proj(h, w_in, t, pos0):
    m, d = h.shape
    n_in = w_in.shape[1]
    emit = w_in.dtype == _F32
    tm = _row_tile(m, t, _TM)
    tn = _TN_EMIT if emit else _TN
    cols = _InCols(tn)
    nj = n_in // tn
    assert n_in % tn == 0 and nj - cols.gab == 2 * d // tn, (n_in, d)
    assert not emit or m == tm, "the bf16 copy is written by a single row tile"
    cos, sin = _rope_tables(t, pos0, tm)
    tab_blocks = cos.shape[0] // tm
    half = RET_DK // 2

    def clamp(lo, n):
        return lambda i, j: (i, jnp.clip(j - lo, 0, n - 1))

    out_shape = [
        jax.ShapeDtypeStruct((m, 3 * RET_QK), _BF16),
        jax.ShapeDtypeStruct((m, RET_V), _F32),
        jax.ShapeDtypeStruct((m, SWA_Q), _BF16),
        jax.ShapeDtypeStruct((SWA_KV_HEADS, m, 4 * _LANES), _BF16),
        jax.ShapeDtypeStruct((m, 2 * SWA_KV), _F32),
        jax.ShapeDtypeStruct((m, 2 * d), _F32),
    ]
    out_specs = [
        pl.BlockSpec((tm, tn), clamp(0, cols.rg)),
        pl.BlockSpec((tm, tn), clamp(cols.rg, cols.n_ret)),
        pl.BlockSpec((tm, tn), clamp(cols.sq, cols.skv - cols.sq)),
        pl.BlockSpec((SWA_KV_HEADS, tm, 4 * _LANES), lambda i, j: (0, i, 0)),
        pl.BlockSpec((tm, tn), clamp(cols.skv, cols.n_skv)),
        pl.BlockSpec((tm, tn), clamp(cols.gab, nj - cols.gab)),
    ]
    w_bytes = _nbytes((d, tn), w_in.dtype)
    est = 2 * (_nbytes((tm, d), _BF16) + w_bytes + 2 * _nbytes((tm, half), _F32)
               + 2 * _nbytes((tm, tn), _BF16) + 3 * _nbytes((tm, tn), _F32)
               + _nbytes((SWA_KV_HEADS, tm, 4 * _LANES), _BF16)) + 2 * _nbytes((tm, tn), _F32)
    if emit:
        out_shape.append(jax.ShapeDtypeStruct((d, n_in), _BF16))
        out_specs.append(pl.BlockSpec((d, tn), lambda i, j: (0, j)))
        est += 2 * _nbytes((d, tn), _BF16)
    return pl.pallas_call(
        functools.partial(_in_proj_kernel, cols=cols, emit=emit),
        grid=(m // tm, nj),
        in_specs=[pl.BlockSpec((tm, d), lambda i, j: (i, 0)),
                  pl.BlockSpec((d, tn), lambda i, j: (0, j)),
                  pl.BlockSpec((tm, half), lambda i, j: (i % tab_blocks, 0)),
                  pl.BlockSpec((tm, half), lambda i, j: (i % tab_blocks, 0))],
        out_specs=out_specs,
        out_shape=out_shape,
        compiler_params=_cparams(("arbitrary", "arbitrary"), est),
        name="in_proj_emit" if emit else "in_proj",
    )(h, w_in, cos, sin)


def _retention_kernel(*refs, hb, cc, n_chunks, has_state):
    if has_state:
        lg_ref, q_ref, k_ref, v_ref, rg_ref, s0_ref, a_ref, sout_ref, s_scr = refs
    else:
        lg_ref, q_ref, k_ref, v_ref, rg_ref, a_ref, sout_ref, s_scr = refs
    row = lax.broadcasted_iota(jnp.int32, (cc, cc), 0)
    col = lax.broadcasted_iota(jnp.int32, (cc, cc), 1)
    diff = (row - col).astype(_F32)
    idx = lax.broadcasted_iota(jnp.int32, (cc, 1), 0).astype(_F32)
    for hh in range(hb):
        lg = lg_ref[hh][:, :1]
        dmask = jnp.where(diff >= 0, jnp.exp(lg * jnp.maximum(diff, 0.0)), 0.0)
        dq = jnp.exp(lg * (idx + 1.0))
        dk = jnp.exp(lg * (cc - 1.0 - idx))
        ds = jnp.exp(lg * float(cc))
        cols = slice(hh * RET_DK, (hh + 1) * RET_DK)
        if has_state:
            s_scr[...] = s0_ref[0, hh]
        else:
            s_scr[...] = jnp.zeros_like(s_scr)

        def body(c, carry):
            r = pl.ds(pl.multiple_of(c * cc, cc), cc)
            q = q_ref[r, cols]
            k = k_ref[r, cols]
            v = v_ref[r, cols]
            state = s_scr[...]
            scores = lax.dot_general(q, k, (((1,), (1,)), ((), ())), preferred_element_type=_F32) * dmask
            y = jnp.dot(scores.astype(_BF16), v, preferred_element_type=_F32)
            y = y + jnp.dot(q, state.astype(_BF16), preferred_element_type=_F32) * dq
            k_dec = (k.astype(_F32) * dk).astype(_BF16)
            s_scr[...] = state * ds + lax.dot_general(
                k_dec, v, (((0,), (0,)), ((), ())), preferred_element_type=_F32)
            mu = jnp.mean(y, axis=-1, keepdims=True)
            dev = y - mu
            var = jnp.mean(dev * dev, axis=-1, keepdims=True)
            yn = dev * lax.rsqrt(var + EPS)
            g = rg_ref[r, cols]
            a_ref[r, cols] = (g * jax.nn.sigmoid(g) * yn).astype(_BF16)
            return carry

        lax.fori_loop(0, n_chunks, body, 0)
        sout_ref[0, hh] = s_scr[...]


def _retention(qkv, rg, state, b, t):
    m = qkv.shape[0]
    cc = min(_RET_CHUNK, t)
    assert t % cc == 0 and cc % min(CHUNK, t) == 0
    hb = RET_HEADS if t * RET_QK * 2 <= (1 << 20) else 1
    nhb = RET_HEADS // hb
    wcols = hb * RET_DK
    h = jnp.arange(RET_HEADS, dtype=_F32)
    lg = jnp.log1p(-jnp.exp2(-5.0 - h))
    lg = jnp.broadcast_to(lg[:, None, None], (RET_HEADS, 1, _LANES))
    has_state = state is not None
    in_specs = [pl.BlockSpec((hb, 1, _LANES), lambda bi, hi: (hi, 0, 0)),
                pl.BlockSpec((t, wcols), lambda bi, hi: (bi, hi)),
                pl.BlockSpec((t, wcols), lambda bi, hi: (bi, nhb + hi)),
                pl.BlockSpec((t, wcols), lambda bi, hi: (bi, 2 * nhb + hi)),
                pl.BlockSpec((t, wcols), lambda bi, hi: (bi, hi))]
    args = [lg, qkv, qkv, qkv, rg]
    s_spec = pl.BlockSpec((1, hb, RET_DK, RET_DV), lambda bi, hi: (bi, hi, 0, 0))
    if has_state:
        in_specs.append(s_spec)
        args.append(state)
    est = 2 * (4 * _nbytes((t, wcols), _BF16) + _nbytes((t, wcols), _F32)
               + 2 * _nbytes((hb, RET_DK, RET_DV), _F32)) + 8 * _nbytes((cc, max(cc, RET_DV)), _F32)
    kern = functools.partial(_retention_kernel, hb=hb, cc=cc, n_chunks=t // cc, has_state=has_state)
    return pl.pallas_call(
        kern,
        grid=(b, nhb),
        in_specs=in_specs,
        out_specs=(pl.BlockSpec((t, wcols), lambda bi, hi: (bi, hi)), s_spec),
        out_shape=(jax.ShapeDtypeStruct((m, RET_V), _BF16),
                   jax.ShapeDtypeStruct((b, RET_HEADS, RET_DK, RET_DV), _F32)),
        scratch_shapes=[pltpu.VMEM((RET_DK, RET_DV), _F32)],
        compiler_params=_cparams(("arbitrary", "arbitrary"), est),
        name="retention",
    )(*args)


def _swa_core(qst, k_even, k_odd, v_even, v_odd, sink_even, sink_odd, valid):
    def weights(kx, sink):
        s = lax.dot_general(qst, kx, (((1,), (1,)), ((), ())), preferred_element_type=_F32) * (SWA_HD ** -0.5)
        if valid is not None:
            s = jnp.where(valid, s, NEG_INF)
        mx = jnp.maximum(jnp.max(s, axis=-1, keepdims=True), sink)
        return jnp.exp(s - mx).astype(_BF16), jnp.exp(sink - mx)

    e_even, t_even = weights(k_even, sink_even)
    e_odd, t_odd = weights(k_odd, sink_odd)
    o_even = jnp.dot(e_even, v_even, preferred_element_type=_F32)
    o_odd = jnp.dot(e_odd, v_odd, preferred_element_type=_F32)
    low = lax.broadcasted_iota(jnp.int32, o_even.shape, 1) < SWA_HD
    num = jnp.where(low, o_even, o_odd)
    den = jnp.where(low, pltpu.roll(o_even, SWA_HD, 1) + t_even, pltpu.roll(o_odd, SWA_HD, 1) + t_odd)
    return num / den


def _swa_prompt_kernel(q_ref, kv_ref, sink_ref, o_ref, *, n_blocks, unroll):
    sink_even = sink_ref[0, 0]
    sink_odd = sink_ref[0, 1]
    rows = _SWA_QB * CHUNK
    win = _SWA_QB + WINDOW_CHUNKS

    def valid_mask(n_key_chunks, first):
        shape = (SWA_PAIRS * rows, n_key_chunks * CHUNK)
        qc = (lax.broadcasted_iota(jnp.int32, shape, 0) // CHUNK) % _SWA_QB
        kc = lax.broadcasted_iota(jnp.int32, shape, 1) // CHUNK
        if first:
            return kc <= qc
        return (kc >= qc) & (kc <= qc + WINDOW_CHUNKS)

    def block(r, kr, valid):
        qst = jnp.concatenate([q_ref[r, p * _LANES:(p + 1) * _LANES] for p in range(SWA_PAIRS)], axis=0)
        o = _swa_core(qst,
                      kv_ref[0, kr, 0:_LANES], kv_ref[0, kr, _LANES:2 * _LANES],
                      kv_ref[0, kr, 2 * _LANES:3 * _LANES], kv_ref[0, kr, 3 * _LANES:4 * _LANES],
                      sink_even, sink_odd, valid)
        for p in range(SWA_PAIRS):
            o_ref[r, p * _LANES:(p + 1) * _LANES] = o[p * rows:(p + 1) * rows].astype(o_ref.dtype)

    block(pl.ds(0, rows), pl.ds(0, rows), valid_mask(_SWA_QB, True))
    if n_blocks > 1:
        valid = valid_mask(win, False)

        def body(b, carry):
            r = pl.ds(pl.multiple_of(b * rows, rows), rows)
            kr = pl.ds(pl.multiple_of(b * rows - WINDOW_CHUNKS * CHUNK, CHUNK), win * CHUNK)
            block(r, kr, valid)
            return carry

        lax.fori_loop(1, n_blocks, body, 0, unroll=unroll)


def _sink_columns(sinks, rows):
    s = sinks.astype(_F32).reshape(SWA_KV_HEADS, SWA_PAIRS, 2).transpose(0, 2, 1)
    return jnp.repeat(s, rows, axis=-1)[..., None]


def _swa_prompt(sq, kv2, sinks, b, t):
    m = sq.shape[0]
    gw = SWA_GROUP * SWA_HD
    rows = _SWA_QB * CHUNK
    assert t % rows == 0 and WINDOW_CHUNKS == _SWA_QB
    n_blocks = t // rows
    unroll = next(u for u in (3, 2, 1) if (n_blocks - 1) % u == 0)
    sink_cols = _sink_columns(sinks, rows)
    est = 2 * (2 * _nbytes((t, gw), _BF16) + _nbytes((t, 4 * _LANES), _BF16)) + (8 << 20)
    return pl.pallas_call(
        functools.partial(_swa_prompt_kernel, n_blocks=n_blocks, unroll=unroll),
        grid=(b, SWA_KV_HEADS),
        in_specs=[pl.BlockSpec((t, gw), lambda bi, kv: (bi, kv)),
                  pl.BlockSpec((1, t, 4 * _LANES), lambda bi, kv: (kv, bi, 0)),
                  pl.BlockSpec((1, 2, SWA_PAIRS * rows, 1), lambda bi, kv: (kv, 0, 0, 0))],
        out_specs=pl.BlockSpec((t, gw), lambda bi, kv: (bi, kv)),
        out_shape=jax.ShapeDtypeStruct((m, SWA_Q), _BF16),
        compiler_params=_cparams(("arbitrary", "arbitrary"), est),
        name="swa_prompt",
    )(sq, kv2, sink_cols)


def _swa_cache_kernel(q_ref, kv_ref, kc_ref, vc_ref, sink_ref, o_ref, *, L):
    gw = SWA_GROUP * SWA_HD
    kc = kc_ref[0]
    vc = vc_ref[0]
    for kv in range(SWA_KV_HEADS):
        lanes = slice(_LANES * (kv // 2), _LANES * (kv // 2 + 1))
        kc_even, kc_odd = _pair_slabs(kc[:, lanes], kv % 2, fill=0.0)
        vc_even, vc_odd = _pair_slabs(vc[:, lanes], kv % 2, fill=1.0)
        new = [kv_ref[kv, :, c * _LANES:(c + 1) * _LANES] for c in range(4)]
        qst = jnp.concatenate(
            [q_ref[:, kv * gw + p * _LANES:kv * gw + (p + 1) * _LANES] for p in range(SWA_PAIRS)], axis=0)
        o = _swa_core(qst,
                      jnp.concatenate([kc_even.astype(_BF16), new[0]], axis=0),
                      jnp.concatenate([kc_odd.astype(_BF16), new[1]], axis=0),
                      jnp.concatenate([vc_even.astype(_BF16), new[2]], axis=0),
                      jnp.concatenate([vc_odd.astype(_BF16), new[3]], axis=0),
                      sink_ref[kv, 0], sink_ref[kv, 1], None)
        for p in range(SWA_PAIRS):
            lo = kv * gw + p * _LANES
            o_ref[:, lo:lo + _LANES] = o[p * L:(p + 1) * L].astype(o_ref.dtype)


def _swa_cache(sq, kv2, k_cache, v_cache, sinks, b, t):
    m = sq.shape[0]
    rows = k_cache.shape[1]
    kc = k_cache.reshape(b, rows, SWA_KV)
    vc = v_cache.reshape(b, rows, SWA_KV)
    sink_cols = _sink_columns(sinks, t)
    est = 2 * (2 * _nbytes((t, SWA_Q), _BF16) + _nbytes((SWA_KV_HEADS, t, 4 * _LANES), _BF16)
               + 2 * _nbytes((rows, SWA_KV), _F32)) + (4 << 20)
    return pl.pallas_call(
        functools.partial(_swa_cache_kernel, L=t),
        grid=(b,),
        in_specs=[pl.BlockSpec((t, SWA_Q), lambda bi: (bi, 0)),
                  pl.BlockSpec((SWA_KV_HEADS, t, 4 * _LANES), lambda bi: (0, bi, 0)),
                  pl.BlockSpec((1, rows, SWA_KV), lambda bi: (bi, 0, 0)),
                  pl.BlockSpec((1, rows, SWA_KV), lambda bi: (bi, 0, 0)),
                  pl.BlockSpec((SWA_KV_HEADS, 2, SWA_PAIRS * t, 1), lambda bi: (0, 0, 0, 0))],
        out_specs=pl.BlockSpec((t, SWA_Q), lambda bi: (bi, 0)),
        out_shape=jax.ShapeDtypeStruct((m, SWA_Q), _BF16),
        compiler_params=_cparams(("arbitrary",), est),
        name="swa_cache",
    )(sq, kv2, kc, vc, sink_cols)


def _bf16_weight(w_ref, wbf_ref):
    if wbf_ref is None:
        return w_ref[...]
    wbf_ref[...] = w_ref[...].astype(_BF16)
    return wbf_ref[...]


def _merge_kernel(a_ref, o_ref, wr_ref, ws_ref, ga_ref, gb_ref, out_ref, wrb_ref=None, wsb_ref=None):
    yr = jnp.dot(a_ref[...], _bf16_weight(wr_ref, wrb_ref), preferred_element_type=_F32)
    ys = jnp.dot(o_ref[...], _bf16_weight(ws_ref, wsb_ref), preferred_element_type=_F32)
    out_ref[...] = (jax.nn.sigmoid(ga_ref[...]) * yr + jax.nn.sigmoid(gb_ref[...]) * ys).astype(out_ref.dtype)


def _merge(a_ret, o_swa, gab, w_ret_o, w_swa_o):
    m = a_ret.shape[0]
    d = w_ret_o.shape[1]
    emit = w_ret_o.dtype == _F32
    tm = min(_TM, m)
    tn = _TN
    nd = d // tn
    assert not emit or m == tm, "the bf16 copies are written by a single row tile"
    est = 2 * (_nbytes((tm, RET_V), _BF16) + _nbytes((tm, SWA_Q), _BF16) + _nbytes((RET_V, tn), w_ret_o.dtype)
               + _nbytes((SWA_Q, tn), w_swa_o.dtype) + 2 * _nbytes((tm, tn), _F32) + _nbytes((tm, tn), _BF16)
               ) + 4 * _nbytes((tm, tn), _F32)
    out_specs = [pl.BlockSpec((tm, tn), lambda i, j: (i, j))]
    out_shape = [jax.ShapeDtypeStruct((m, d), _BF16)]
    if emit:
        out_specs += [pl.BlockSpec((RET_V, tn), lambda i, j: (0, j)), pl.BlockSpec((SWA_Q, tn), lambda i, j: (0, j))]
        out_shape += [jax.ShapeDtypeStruct((RET_V, d), _BF16), jax.ShapeDtypeStruct((SWA_Q, d), _BF16)]
        est += 2 * (_nbytes((RET_V, tn), _BF16) + _nbytes((SWA_Q, tn), _BF16))
    return pl.pallas_call(
        _merge_kernel,
        grid=(m // tm, nd),
        in_specs=[pl.BlockSpec((tm, RET_V), lambda i, j: (i, 0)),
                  pl.BlockSpec((tm, SWA_Q), lambda i, j: (i, 0)),
                  pl.BlockSpec((RET_V, tn), lambda i, j: (0, j)),
                  pl.BlockSpec((SWA_Q, tn), lambda i, j: (0, j)),
                  pl.BlockSpec((tm, tn), lambda i, j: (i, j)),
                  pl.BlockSpec((tm, tn), lambda i, j: (i, nd + j))],
        out_specs=out_specs,
        out_shape=out_shape,
        compiler_params=_cparams(("arbitrary", "arbitrary"), est),
        name="merge_emit" if emit else "merge",
    )(a_ret, o_swa, w_ret_o, w_swa_o, gab, gab)


def _proj_residual_kernel(a_ref, w_ref, x_ref, gt_ref, o_ref, wbf_ref=None):
    f = jnp.dot(a_ref[...], _bf16_weight(w_ref, wbf_ref), preferred_element_type=_F32)
    o_ref[...] = x_ref[...] + _per_seq(f, gt_ref[...], lambda v, gt: gt * v)


def _proj_residual(a, w, x, gt, t, tm_cap, name):
    m, k = a.shape
    d = w.shape[1]
    emit = w.dtype == _F32
    tm = _row_tile(m, t, tm_cap)
    tn = _TN
    assert not emit or m == tm, "the bf16 copy is written by a single row tile"
    est = 2 * (_nbytes((tm, k), _BF16) + _nbytes((k, tn), w.dtype) + 2 * _nbytes((tm, tn), _F32)
               ) + 2 * _nbytes((tm, tn), _F32)
    out_specs = [pl.BlockSpec((tm, tn), lambda i, j: (i, j))]
    out_shape = [jax.ShapeDtypeStruct((m, d), _F32)]
    if emit:
        out_specs.append(pl.BlockSpec((k, tn), lambda i, j: (0, j)))
        out_shape.append(jax.ShapeDtypeStruct((k, d), _BF16))
        est += 2 * _nbytes((k, tn), _BF16)
    return pl.pallas_call(
        _proj_residual_kernel,
        grid=(m // tm, d // tn),
        in_specs=[pl.BlockSpec((tm, k), lambda i, j: (i, 0)),
                  pl.BlockSpec((k, tn), lambda i, j: (0, j)),
                  pl.BlockSpec((tm, tn), lambda i, j: (i, j)),
                  _mod_spec(tm, t, tn, lambda i, j: j)],
        out_specs=out_specs,
        out_shape=out_shape,
        compiler_params=_cparams(("arbitrary", "arbitrary"), est),
        name=name + "_emit" if emit else name,
    )(a, w, x, gt)


def _up_kernel(*refs, tps, nf, emit):
    if emit:
        h_ref, wu_ref, wv_ref, wc_ref, bc_ref, cs_ref, g_ref, tail_ref, wub_ref, wvb_ref, u_scr, v_scr, carry_ref = refs
    else:
        h_ref, wu_ref, wv_ref, wc_ref, bc_ref, cs_ref, g_ref, tail_ref, u_scr, v_scr, carry_ref = refs
        wub_ref = wvb_ref = None
    i = pl.program_id(0)
    j = pl.program_id(1)
    tm, tn = u_scr.shape

    def matmul():
        h = h_ref[...]
        u_scr[...] = jnp.dot(h, _bf16_weight(wu_ref, wub_ref), preferred_element_type=_F32)
        v_scr[...] = jnp.dot(h, _bf16_weight(wv_ref, wvb_ref), preferred_element_type=_F32)

    def epilogue():
        jp = j - 1
        u = u_scr[...]
        v = v_scr[...]
        cs = cs_ref[...]
        spt = cs.shape[0]
        rows = tm // spt
        c0 = cs[:, 0:1, :]
        c1 = cs[:, 1:2, :]
        if tps > 1:
            first = (i % tps) == 0
            prev = carry_ref[jp]
            c0 = jnp.where(first, c0, prev[_SUBLANES - 2:_SUBLANES - 1][None])
            c1 = jnp.where(first, c1, prev[_SUBLANES - 1:_SUBLANES][None])
            carry_ref[jp] = u[tm - _SUBLANES:tm]
        wc = wc_ref[...]
        bc = bc_ref[...]
        if spt == 1:
            r = lax.broadcasted_iota(jnp.int32, (tm, tn), 0)
            p1 = jnp.where(r == 0, c1[0], pltpu.roll(u, 1, 0))
            p2 = jnp.where(r == 0, c0[0], jnp.where(r == 1, c1[0], pltpu.roll(u, 2, 0)))
            uc = wc[0:1] * p2 + wc[1:2] * p1 + wc[2:3] * u + bc
            g_ref[...] = (jax.nn.gelu(uc) * v).astype(g_ref.dtype)
            tail_ref[0, 0] = u[tm - _SUBLANES:tm]
        else:
            u3 = u.reshape(spt, rows, tn)
            r = lax.broadcasted_iota(jnp.int32, (spt, rows, tn), 1)
            p1 = jnp.where(r == 0, c1, pltpu.roll(u3, 1, 1))
            p2 = jnp.where(r == 0, c0, jnp.where(r == 1, c1, pltpu.roll(u3, 2, 1)))
            uc = wc[0:1][None] * p2 + wc[1:2][None] * p1 + wc[2:3][None] * u3 + bc[None]
            g_ref[...] = (jax.nn.gelu(uc) * v.reshape(spt, rows, tn)).reshape(tm, tn).astype(g_ref.dtype)
            tail_ref[0] = u3[:, rows - _SUBLANES:rows, :]

    @pl.when(j == 0)
    def _first():
        if tps > 1:
            @pl.when(i == 0)
            def _init():
                carry_ref[...] = jnp.zeros_like(carry_ref)
        matmul()

    @pl.when((j > 0) & (j < nf))
    def _steady():
        epilogue()
        matmul()

    @pl.when(j == nf)
    def _drain():
        epilogue()


def _up_conv_geglu(h, w_u, w_v, w_conv, b_conv, conv_state, t):
    m, d = h.shape
    f = w_conv.shape[1]
    (wu_arr, wu_tile0), (wv_arr, wv_tile0) = w_u, w_v
    emit = wu_arr.dtype == _F32
    tm = _row_tile(m, t, _TM)
    tn = _TN_FF
    assert f % tn == 0, (f, tn)
    assert not emit or m == tm, "the bf16 copies are written by a single row tile"
    nf = f // tn
    spt, tps = _seq_block(tm, t)
    rows = tm // spt
    assert rows >= _SUBLANES and conv_state.shape[1] == CONV_W - 1
    cur = lambda j: jnp.minimum(j, nf - 1)
    prev = lambda j: jnp.maximum(j - 1, 0)
    est = 2 * (_nbytes((tm, d), _BF16) + 2 * _nbytes((d, tn), wu_arr.dtype) + _nbytes((tm, tn), _BF16)
               + _nbytes((spt, _SUBLANES, tn), _F32)) + _nbytes((nf, _SUBLANES, tn), _F32) + 10 * _nbytes((tm, tn), _F32)
    out_specs = [pl.BlockSpec((tm, tn), lambda i, j: (i, prev(j))),
                 pl.BlockSpec((1, spt, _SUBLANES, tn), lambda i, j: (i, 0, 0, prev(j)))]
    out_shape = [jax.ShapeDtypeStruct((m, f), _BF16),
                 jax.ShapeDtypeStruct((m // tm, spt, _SUBLANES, f), _F32)]
    if emit:
        out_specs += [pl.BlockSpec((d, tn), lambda i, j: (0, cur(j))) for _ in range(2)]
        out_shape += [jax.ShapeDtypeStruct((d, f), _BF16) for _ in range(2)]
        est += 4 * _nbytes((d, tn), _BF16)
    return pl.pallas_call(
        functools.partial(_up_kernel, tps=tps, nf=nf, emit=emit),
        grid=(m // tm, nf + 1),
        in_specs=[pl.BlockSpec((tm, d), lambda i, j: (i, 0)),
                  pl.BlockSpec((d, tn), lambda i, j: (0, wu_tile0 + cur(j))),
                  pl.BlockSpec((d, tn), lambda i, j: (0, wv_tile0 + cur(j))),
                  pl.BlockSpec((CONV_W, tn), lambda i, j: (0, prev(j))),
                  pl.BlockSpec((1, tn), lambda i, j: (0, prev(j))),
                  pl.BlockSpec((spt, CONV_W - 1, tn), lambda i, j: (i // tps, 0, prev(j)))],
        out_specs=out_specs,
        out_shape=out_shape,
        scratch_shapes=[pltpu.VMEM((tm, tn), _F32), pltpu.VMEM((tm, tn), _F32),
                        pltpu.VMEM((nf, _SUBLANES, tn), _F32)],
        compiler_params=_cparams(("arbitrary", "arbitrary"), est),
        name="up_conv_geglu_emit" if emit else "up_conv_geglu",
    )(h, wu_arr, wv_arr, w_conv, b_conv.reshape(1, f), conv_state)


def _layer(x, mod, pos0, ret_state, swa_kc, swa_vc, conv_state, w):
    b, t, d = x.shape
    m = b * t
    x2 = x.reshape(m, d)
    sh1, sc1, gt1, sh2, sc2, gt2 = [v.reshape(b, 1, d) for v in jnp.split(mod, 6, axis=-1)]
    bf = {}

    h = _norm_mod(x2, w["g_mix"], sc1, sh1, t)
    qkv, rg, sq, kv2, skv_f, gab, *w_bf = _in_proj(h, w["w_in"], t, pos0)
    bf.update(zip(("w_in",), w_bf))
    a_ret, s_new = _retention(qkv, rg, ret_state, b, t)
    if swa_kc is None:
        o_swa = _swa_prompt(sq, kv2, w["swa_sinks"], b, t)
        keep = min(WINDOW, t)
    else:
        o_swa = _swa_cache(sq, kv2, swa_kc, swa_vc, w["swa_sinks"], b, t)
        keep = t
    merged, *w_bf = _merge(a_ret, o_swa, gab, w["w_ret_o"], w["w_swa_o"])
    bf.update(zip(("w_ret_o", "w_swa_o"), w_bf))
    x1, *w_bf = _proj_residual(merged, w["w_out"], x2, gt1, t, _TM, "out_proj")
    bf.update(zip(("w_out",), w_bf))

    h2 = _norm_mod(x1, w["g_ffn"], sc2, sh2, t)
    f = w["b_conv"].shape[0]
    cs = jnp.zeros((b, CONV_W - 1, f), _F32) if conv_state is None else conv_state
    if "w_up_u" in w:
        w_u, w_v = (w["w_up_u"], 0), (w["w_up_v"], 0)
    else:
        w_u, w_v = (w["w_up"], 0), (w["w_up"], f // _TN_FF)
    g, tails, *w_bf = _up_conv_geglu(h2, w_u, w_v, w["w_conv"], w["b_conv"], cs, t)
    bf.update(zip(("w_up_u", "w_up_v"), w_bf))
    x3, = _proj_residual(g, w["w_down"], x1, gt2, t, _TM_DOWN, "down_proj")
    y = _final_norm(x3, w["g_final"])

    skv = skv_f.reshape(b, t, 2, SWA_KV_HEADS, SWA_HD)[:, t - keep:]
    k_new = skv[:, :, 0]
    v_new = skv[:, :, 1]
    segs = tails.shape[0] * tails.shape[1] // b
    tails = tails.reshape(b, segs, _SUBLANES, f)
    conv_new = tails[:, segs - 1, _SUBLANES - (CONV_W - 1):, :]
    return y.reshape(b, t, d), s_new, k_new, v_new, conv_new, bf


def kernel(x_prompt, x_sample, cache_swa_k, cache_swa_v, state_ret, state_conv, c_prompt, c_sample,
           g_mix, g_ffn, w_ada, b_ada, w_in, swa_sinks, w_ret_o, w_swa_o, w_out,
           w_up, w_conv, b_conv, w_down, g_final):
    depth = g_mix.shape[0]
    assert depth == 1, "single-layer stack"
    bp = x_prompt.shape[0]
    bs = x_sample.shape[0]
    l = 0

    c_all = jnp.concatenate([c_prompt, c_sample], axis=0)
    rows = -(-c_all.shape[0] // 16) * 16
    c_all = jnp.pad(c_all, ((0, rows - c_all.shape[0]), (0, 0)))
    mod = _ada(c_all, w_ada[l], b_ada[l])

    w = {
        "g_mix": g_mix[l], "g_ffn": g_ffn[l], "g_final": g_final, "swa_sinks": swa_sinks[l],
        "w_in": w_in[l], "w_ret_o": w_ret_o[l], "w_swa_o": w_swa_o[l], "w_out": w_out[l], "w_up": w_up[l],
        "w_down": w_down[l].astype(_BF16),
        "w_conv": w_conv[l], "b_conv": b_conv[l],
    }

    ys, ss, ks, vs, cs, w_bf = _layer(x_sample, mod[bp:bp + bs], PAST_LEN, state_ret[l], cache_swa_k[l],
                                      cache_swa_v[l], state_conv[l], w)
    yp, sp, kp, vp, cp, _ = _layer(x_prompt, mod[:bp], 0, None, None, None, None, {**w, **w_bf})
    stack = lambda a: a[None]
    return (yp, ys, stack(sp), stack(kp), stack(vp), stack(cp), stack(ss), stack(ks), stack(vs), stack(cs))
```

```python
import functools

import jax
import jax.numpy as jnp
import numpy as np
from jax import lax
from jax.experimental import pallas as pl
from jax.experimental.pallas import tpu as pltpu

_F32 = jnp.float32
_BF16 = jnp.bfloat16

CHUNK = 64
RET_HEADS = 8
RET_DK = 256
RET_DV = 256
RET_QK = RET_HEADS * RET_DK
RET_V = RET_HEADS * RET_DV
ROPE_BASE = 10000.0
SWA_HEADS = 32
SWA_KV_HEADS = 4
SWA_HD = 64
SWA_GROUP = SWA_HEADS // SWA_KV_HEADS
SWA_PAIRS = SWA_GROUP // 2
SWA_Q = SWA_HEADS * SWA_HD
SWA_KV = SWA_KV_HEADS * SWA_HD
WINDOW = 128
WINDOW_CHUNKS = WINDOW // CHUNK
CONV_W = 3
EPS = 1e-6
NEG_INF = -1e30
PAST_LEN = 2048

_V7X_VMEM_BYTES = 64 * 1024 * 1024
_LANES = 128
_SUBLANES = 8

_TM = 1024
_TM_DOWN = 512
_TM_NORM = 256
_TN = 2 * SWA_KV
_TN_EMIT = 256
_TN_ADA = 512
_RET_CHUNK = 256
_SWA_QB = 2
_TN_FF = 256


def _cparams(sem, vmem_estimate):
    limit = min(int(vmem_estimate * 1.15) + (4 << 20), _V7X_VMEM_BYTES - (6 << 20))
    return pltpu.CompilerParams(dimension_semantics=sem, vmem_limit_bytes=limit)


def _nbytes(shape, dtype):
    return int(np.prod(shape)) * jnp.dtype(dtype).itemsize


def _row_tile(m, t, cap):
    tm = min(cap, m)
    assert m % tm == 0 and (t % tm == 0 or tm % t == 0), (m, t, tm)
    return tm


def _seq_block(tm, t):
    return max(1, tm // t), max(1, t // tm)


def _mod_spec(tm, t, tn, col):
    spt, tps = _seq_block(tm, t)
    return pl.BlockSpec((spt, 1, tn), lambda i, j: (i // tps, 0, col(i, j)))


def _per_seq(val, mod, fn):
    spt = mod.shape[0]
    if spt == 1:
        return fn(val, mod[0])
    tm, n = val.shape
    return fn(val.reshape(spt, tm // spt, n), mod).reshape(tm, n)


def _ada_kernel(c_ref, w_ref, b_ref, o_ref):
    c = c_ref[...]
    a = (c * jax.nn.sigmoid(c)).astype(_BF16)
    o_ref[...] = jnp.dot(a, w_ref[...].astype(_BF16), preferred_element_type=_F32) + b_ref[...]


def _ada(c, w_ada, b_ada):
    rows, d = c.shape
    n = w_ada.shape[1]
    tn = _TN_ADA
    est = 2 * (_nbytes((rows, d), _F32) + _nbytes((d, tn), _F32) + _nbytes((rows, tn), _F32)) + _nbytes((d, tn), _BF16)
    return pl.pallas_call(
        _ada_kernel,
        grid=(n // tn,),
        in_specs=[pl.BlockSpec((rows, d), lambda j: (0, 0)),
                  pl.BlockSpec((d, tn), lambda j: (0, j)),
                  pl.BlockSpec((1, tn), lambda j: (0, j))],
        out_specs=pl.BlockSpec((rows, tn), lambda j: (0, j)),
        out_shape=jax.ShapeDtypeStruct((rows, n), _F32),
        compiler_params=_cparams(("arbitrary",), est),
        name="ada_mod",
    )(c, w_ada, b_ada.reshape(1, n))


def _rms(x, g):
    y = x * lax.rsqrt(jnp.mean(x * x, axis=-1, keepdims=True) + EPS)
    return y * g


def _norm_mod_kernel(x_ref, g_ref, sc_ref, sh_ref, o_ref):
    y = _rms(x_ref[...], g_ref[...])
    h = _per_seq(y, sc_ref[...], lambda v, sc: v * (1.0 + sc))
    h = _per_seq(h, sh_ref[...], lambda v, sh: v + sh)
    o_ref[...] = h.astype(o_ref.dtype)


def _norm_mod(x, g, sc, sh, t):
    m, d = x.shape
    tm = _row_tile(m, t, _TM_NORM)
    est = 2 * (_nbytes((tm, d), _F32) + _nbytes((tm, d), _BF16)) + 3 * _nbytes((tm, d), _F32)
    return pl.pallas_call(
        _norm_mod_kernel,
        grid=(m // tm, 1),
        in_specs=[pl.BlockSpec((tm, d), lambda i, j: (i, 0)),
                  pl.BlockSpec((1, d), lambda i, j: (0, 0)),
                  _mod_spec(tm, t, d, lambda i, j: 0),
                  _mod_spec(tm, t, d, lambda i, j: 0)],
        out_specs=pl.BlockSpec((tm, d), lambda i, j: (i, 0)),
        out_shape=jax.ShapeDtypeStruct((m, d), _BF16),
        compiler_params=_cparams(("arbitrary", "arbitrary"), est),
        name="norm_mod",
    )(x, g.reshape(1, d), sc, sh)


def _final_norm_kernel(x_ref, g_ref, o_ref):
    o_ref[...] = _rms(x_ref[...], g_ref[...])


def _final_norm(x, g):
    m, d = x.shape
    tm = min(_TM_NORM, m)
    est = 4 * _nbytes((tm, d), _F32) + 2 * _nbytes((tm, d), _F32)
    return pl.pallas_call(
        _final_norm_kernel,
        grid=(m // tm,),
        in_specs=[pl.BlockSpec((tm, d), lambda i: (i, 0)),
                  pl.BlockSpec((1, d), lambda i: (0, 0))],
        out_specs=pl.BlockSpec((tm, d), lambda i: (i, 0)),
        out_shape=jax.ShapeDtypeStruct((m, d), _F32),
        compiler_params=_cparams(("arbitrary",), est),
        name="final_norm",
    )(x, g.reshape(1, d))


def _pair_slabs(slab, parity, fill):
    low = lax.broadcasted_iota(jnp.int32, slab.shape, 1) < SWA_HD
    if parity == 0:
        even = jnp.where(low, slab, 0.0)
        odd = pltpu.roll(even, SWA_HD, 1)
    else:
        odd = jnp.where(low, 0.0, slab)
        even = pltpu.roll(odd, SWA_HD, 1)
    if fill == 0.0:
        return even, odd
    return jnp.where(low, even, fill), jnp.where(low, fill, odd)


class _InPlan:
    def __init__(self, tn, pair, d):
        w = tn * pair
        n_skv = 2 * SWA_KV // tn
        assert RET_QK % w == 0 and SWA_Q % w == 0 and (2 * d) % w == 0 and (RET_V + 2 * d) % w == 0
        assert tn % RET_DK == 0 and (2 * SWA_KV) % tn == 0 and (pair == 1 or n_skv == 1)
        self.tn, self.pair, self.w = tn, pair, w
        r = self.r = RET_QK // w
        self.t_skv = 5 * r
        self.t_gab = self.t_skv + n_skv
        self.steps = self.t_gab + 2 * d // w
        self.skv_block = (RET_V + 2 * d) // w
        self.fo_cols = RET_V + 2 * d + 2 * SWA_KV

    def w_tile(self, t):
        p, ts, tg = self.pair, self.t_skv, self.t_gab
        return jnp.where(t < ts, p * t, jnp.where(t < tg, p * ts + (t - ts), p * ts + (tg - ts) + p * (t - tg)))

    def bq_block(self, t):
        r = self.r
        return jnp.where(t < 3 * r, t, jnp.where(t < 4 * r, 3 * r - 1, jnp.where(t < 5 * r, t - r, 4 * r - 1)))

    def fo_block(self, t):
        r, ts, tg = self.r, self.t_skv, self.t_gab
        return jnp.where(t < 3 * r, 0, jnp.where(t < 4 * r, t - 3 * r, jnp.where(
            t < ts, r - 1, jnp.where(t < tg, self.skv_block + (t - ts), r + (t - tg)))))


def _in_proj_kernel(*refs, plan, emit):
    n_w = plan.pair
    h_ref, w_refs, (cos_ref, sin_ref, bq_ref, fo_ref) = refs[0], refs[1:1 + n_w], refs[1 + n_w:5 + n_w]
    if emit:
        wbf_ref, = refs[5 + n_w:]
        wbf_ref[...] = w_refs[0][...].astype(_BF16)
        w_refs = (wbf_ref,)
    t = pl.program_id(1)
    r, tn = plan.r, plan.tn
    half = RET_DK // 2

    def tiles():
        for n, w_ref in enumerate(w_refs):
            yield jnp.dot(h_ref[...], w_ref[...], preferred_element_type=_F32), n * tn

    @pl.when(t < 2 * r)
    def _rope():
        c = cos_ref[...]
        s = sin_ref[...]
        scale = jnp.where(t >= r, RET_DK ** -0.5, 1.0).astype(_F32)
        for acc, col in tiles():
            for hh in range(tn // RET_DK):
                lo = hh * RET_DK
                x1 = acc[:, lo:lo + half]
                x2 = acc[:, lo + half:lo + RET_DK]
                bq_ref[:, col + lo:col + lo + half] = ((x1 * c - x2 * s) * scale).astype(_BF16)
                bq_ref[:, col + lo + half:col + lo + RET_DK] = ((x2 * c + x1 * s) * scale).astype(_BF16)

    @pl.when(((t >= 2 * r) & (t < 3 * r)) | ((t >= 4 * r) & (t < 5 * r)))
    def _bf16():
        for acc, col in tiles():
            bq_ref[:, col:col + tn] = acc.astype(_BF16)

    @pl.when(((t >= 3 * r) & (t < 4 * r)) | (t >= plan.t_gab))
    def _f32():
        for acc, col in tiles():
            fo_ref[:, col:col + tn] = acc

    @pl.when((t >= plan.t_skv) & (t < plan.t_gab))
    def _kv():
        fo_ref[:, 0:tn] = jnp.dot(h_ref[...], w_refs[0][...], preferred_element_type=_F32)


def _rope_tables(t, pos0, tm):
    half = RET_DK // 2
    inv = ROPE_BASE ** (-jnp.arange(half, dtype=_F32) * (2.0 / RET_DK))
    pos = (pos0 + jnp.arange(t, dtype=jnp.int32)).astype(_F32)
    ang = pos[:, None] * inv[None, :]
    reps = max(1, tm // t)
    return jnp.tile(jnp.cos(ang), (reps, 1)), jnp.tile(jnp.sin(ang), (reps, 1))


def _in_proj(h, w_in, t, pos0):
    m, d = h.shape
    n_in = w_in.shape[1]
    emit = w_in.dtype == _F32
    tm = _row_tile(m, t, _TM)
    plan = _InPlan(_TN_EMIT, 1, d) if emit else _InPlan(_TN, 2, d)
    tn, w = plan.tn, plan.w
    assert n_in == plan.fo_cols + 4 * RET_QK, (n_in, d)
    assert not emit or m == tm, "the bf16 copy is written by a single row tile"
    cos, sin = _rope_tables(t, pos0, tm)
    tab_blocks = cos.shape[0] // tm
    half = RET_DK // 2
    last_tile = n_in // tn - 1

    out_shape = [jax.ShapeDtypeStruct((m, 4 * RET_QK), _BF16), jax.ShapeDtypeStruct((m, plan.fo_cols), _F32)]
    out_specs = [pl.BlockSpec((tm, w), lambda i, j: (i, plan.bq_block(j))),
                 pl.BlockSpec((tm, w), lambda i, j: (i, plan.fo_block(j)))]
    w_specs = [pl.BlockSpec((d, tn), lambda i, j, n=n: (0, jnp.minimum(plan.w_tile(j) + n, last_tile)))
               for n in range(plan.pair)]
    est = 2 * (_nbytes((tm, d), _BF16) + plan.pair * _nbytes((d, tn), w_in.dtype) + 2 * _nbytes((tm, half), _F32)
               + _nbytes((tm, w), _BF16) + _nbytes((tm, w), _F32)) + 2 * plan.pair * _nbytes((tm, tn), _F32)
    if emit:
        out_shape.append(jax.ShapeDtypeStruct((d, n_in), _BF16))
        out_specs.append(pl.BlockSpec((d, tn), lambda i, j: (0, plan.w_tile(j))))
        est += 2 * _nbytes((d, tn), _BF16)
    return pl.pallas_call(
        functools.partial(_in_proj_kernel, plan=plan, emit=emit),
        grid=(m // tm, plan.steps),
        in_specs=[pl.BlockSpec((tm, d), lambda i, j: (i, 0)), *w_specs,
                  pl.BlockSpec((tm, half), lambda i, j: (i % tab_blocks, 0)),
                  pl.BlockSpec((tm, half), lambda i, j: (i % tab_blocks, 0))],
        out_specs=out_specs,
        out_shape=out_shape,
        compiler_params=_cparams(("arbitrary", "arbitrary"), est),
        name="in_proj_emit" if emit else "in_proj",
    )(h, *([w_in] * plan.pair), cos, sin)


def _retention_kernel(*refs, hb, cc, n_chunks, has_state):
    if has_state:
        lg_ref, q_ref, k_ref, v_ref, rg_ref, s0_ref, a_ref, sout_ref, s_scr = refs
    else:
        lg_ref, q_ref, k_ref, v_ref, rg_ref, a_ref, sout_ref, s_scr = refs
    row = lax.broadcasted_iota(jnp.int32, (cc, cc), 0)
    col = lax.broadcasted_iota(jnp.int32, (cc, cc), 1)
    diff = (row - col).astype(_F32)
    idx = lax.broadcasted_iota(jnp.int32, (cc, 1), 0).astype(_F32)
    for hh in range(hb):
        lg = lg_ref[hh][:, :1]
        dmask = jnp.where(diff >= 0, jnp.exp(lg * jnp.maximum(diff, 0.0)), 0.0)
        dq = jnp.exp(lg * (idx + 1.0))
        dk = jnp.exp(lg * (cc - 1.0 - idx))
        ds = jnp.exp(lg * float(cc))
        cols = slice(hh * RET_DK, (hh + 1) * RET_DK)
        if has_state:
            s_scr[...] = s0_ref[0, hh]
        else:
            s_scr[...] = jnp.zeros_like(s_scr)

        def body(c, carry):
            r = pl.ds(pl.multiple_of(c * cc, cc), cc)
            q = q_ref[r, cols]
            k = k_ref[r, cols]
            v = v_ref[r, cols]
            state = s_scr[...]
            scores = lax.dot_general(q, k, (((1,), (1,)), ((), ())), preferred_element_type=_F32) * dmask
            y = jnp.dot(scores.astype(_BF16), v, preferred_element_type=_F32)
            y = y + jnp.dot(q, state.astype(_BF16), preferred_element_type=_F32) * dq
            k_dec = (k.astype(_F32) * dk).astype(_BF16)
            s_scr[...] = state * ds + lax.dot_general(
                k_dec, v, (((0,), (0,)), ((), ())), preferred_element_type=_F32)
            mu = jnp.mean(y, axis=-1, keepdims=True)
            dev = y - mu
            var = jnp.mean(dev * dev, axis=-1, keepdims=True)
            yn = dev * lax.rsqrt(var + EPS)
            g = rg_ref[r, cols]
            a_ref[r, cols] = (g * jax.nn.sigmoid(g) * yn).astype(_BF16)
            return carry

        lax.fori_loop(0, n_chunks, body, 0, unroll=2 if n_chunks % 2 == 0 else 1)
        sout_ref[0, hh] = s_scr[...]


def _retention(qkv, rg, state, b, t):
    m = qkv.shape[0]
    cc = min(_RET_CHUNK, t)
    assert t % cc == 0 and cc % min(CHUNK, t) == 0
    hb = RET_HEADS if t * RET_QK * 2 <= (1 << 20) else 1
    nhb = RET_HEADS // hb
    wcols = hb * RET_DK
    h = jnp.arange(RET_HEADS, dtype=_F32)
    lg = jnp.log1p(-jnp.exp2(-5.0 - h))
    lg = jnp.broadcast_to(lg[:, None, None], (RET_HEADS, 1, _LANES))
    has_state = state is not None
    in_specs = [pl.BlockSpec((hb, 1, _LANES), lambda bi, hi: (hi, 0, 0)),
                pl.BlockSpec((t, wcols), lambda bi, hi: (bi, hi)),
                pl.BlockSpec((t, wcols), lambda bi, hi: (bi, nhb + hi)),
                pl.BlockSpec((t, wcols), lambda bi, hi: (bi, 2 * nhb + hi)),
                pl.BlockSpec((t, wcols), lambda bi, hi: (bi, hi))]
    args = [lg, qkv, qkv, qkv, rg]
    s_spec = pl.BlockSpec((1, hb, RET_DK, RET_DV), lambda bi, hi: (bi, hi, 0, 0))
    if has_state:
        in_specs.append(s_spec)
        args.append(state)
    est = 2 * (4 * _nbytes((t, wcols), _BF16) + _nbytes((t, wcols), _F32)
               + 2 * _nbytes((hb, RET_DK, RET_DV), _F32)) + 8 * _nbytes((cc, max(cc, RET_DV)), _F32)
    kern = functools.partial(_retention_kernel, hb=hb, cc=cc, n_chunks=t // cc, has_state=has_state)
    return pl.pallas_call(
        kern,
        grid=(b, nhb),
        in_specs=in_specs,
        out_specs=(pl.BlockSpec((t, wcols), lambda bi, hi: (bi, hi)), s_spec),
        out_shape=(jax.ShapeDtypeStruct((m, RET_V), _BF16),
                   jax.ShapeDtypeStruct((b, RET_HEADS, RET_DK, RET_DV), _F32)),
        scratch_shapes=[pltpu.VMEM((RET_DK, RET_DV), _F32)],
        compiler_params=_cparams(("arbitrary", "arbitrary"), est),
        name="retention",
    )(*args)


def _swa_core(qst, k_even, k_odd, v_even, v_odd, sink_even, sink_odd, valid):
    def weights(kx, sink):
        s = lax.dot_general(qst, kx, (((1,), (1,)), ((), ())), preferred_element_type=_F32) * (SWA_HD ** -0.5)
        if valid is not None:
            s = jnp.where(valid, s, NEG_INF)
        mx = jnp.maximum(jnp.max(s, axis=-1, keepdims=True), sink)
        return jnp.exp(s - mx).astype(_BF16), jnp.exp(sink - mx)

    e_even, t_even = weights(k_even, sink_even)
    e_odd, t_odd = weights(k_odd, sink_odd)
    o_even = jnp.dot(e_even, v_even, preferred_element_type=_F32)
    o_odd = jnp.dot(e_odd, v_odd, preferred_element_type=_F32)
    low = lax.broadcasted_iota(jnp.int32, o_even.shape, 1) < SWA_HD
    num = jnp.where(low, o_even, o_odd)
    den = jnp.where(low, pltpu.roll(o_even, SWA_HD, 1) + t_even, pltpu.roll(o_odd, SWA_HD, 1) + t_odd)
    return num / den


def _swa_prompt_kernel(q_ref, k_ref, v_ref, sink_ref, o_ref, slab_ref, *, n_blocks, unroll):
    gw = SWA_GROUP * SWA_HD
    rows = _SWA_QB * CHUNK
    win = _SWA_QB + WINDOW_CHUNKS

    for par in range(2):
        slabs = _pair_slabs(k_ref[...], par, fill=0.0) + _pair_slabs(v_ref[...], par, fill=1.0)
        for c, x in enumerate(slabs):
            slab_ref[par, c] = x.astype(_BF16)

    def valid_mask(n_key_chunks, first):
        shape = (SWA_PAIRS * rows, n_key_chunks * CHUNK)
        qc = (lax.broadcasted_iota(jnp.int32, shape, 0) // CHUNK) % _SWA_QB
        kc = lax.broadcasted_iota(jnp.int32, shape, 1) // CHUNK
        if first:
            return kc <= qc
        return (kc >= qc) & (kc <= qc + WINDOW_CHUNKS)

    valid_first = valid_mask(_SWA_QB, True)
    valid = valid_mask(win, False)
    for par in range(2):
        sink_even = sink_ref[par, 0]
        sink_odd = sink_ref[par, 1]
        lane0 = par * gw

        def block(r, kr, mask):
            qst = jnp.concatenate(
                [q_ref[r, lane0 + p * _LANES:lane0 + (p + 1) * _LANES] for p in range(SWA_PAIRS)], axis=0)
            o = _swa_core(qst, slab_ref[par, 0, kr, :], slab_ref[par, 1, kr, :],
                          slab_ref[par, 2, kr, :], slab_ref[par, 3, kr, :], sink_even, sink_odd, mask)
            for p in range(SWA_PAIRS):
                o_ref[r, lane0 + p * _LANES:lane0 + (p + 1) * _LANES] = (
                    o[p * rows:(p + 1) * rows].astype(o_ref.dtype))

        block(pl.ds(0, rows), pl.ds(0, rows), valid_first)
        if n_blocks > 1:
            def body(b, carry):
                r = pl.ds(pl.multiple_of(b * rows, rows), rows)
                kr = pl.ds(pl.multiple_of(b * rows - WINDOW_CHUNKS * CHUNK, CHUNK), win * CHUNK)
                block(r, kr, valid)
                return carry

            lax.fori_loop(1, n_blocks, body, 0, unroll=unroll)


def _sink_columns(sinks, rows):
    s = sinks.astype(_F32).reshape(SWA_KV_HEADS, SWA_PAIRS, 2).transpose(0, 2, 1)
    return jnp.repeat(s, rows, axis=-1)[..., None]


def _swa_prompt(bq, fo, sinks, b, t):
    m = bq.shape[0]
    gw2 = 2 * SWA_GROUP * SWA_HD
    rows = _SWA_QB * CHUNK
    assert t % rows == 0 and WINDOW_CHUNKS == _SWA_QB
    n_blocks = t // rows
    unroll = next(u for u in (3, 2, 1) if (n_blocks - 1) % u == 0)
    sink_cols = _sink_columns(sinks, rows)
    q0 = 3 * RET_QK // gw2
    k0 = (fo.shape[1] - 2 * SWA_KV) // _LANES
    v0 = k0 + SWA_KV // _LANES
    est = 2 * (2 * _nbytes((t, gw2), _BF16) + 2 * _nbytes((t, _LANES), _F32)) + _nbytes((8, t, _LANES), _BF16) + (8 << 20)
    return pl.pallas_call(
        functools.partial(_swa_prompt_kernel, n_blocks=n_blocks, unroll=unroll),
        grid=(b, SWA_KV_HEADS // 2),
        in_specs=[pl.BlockSpec((t, gw2), lambda bi, kp: (bi, q0 + kp)),
                  pl.BlockSpec((t, _LANES), lambda bi, kp: (bi, k0 + kp)),
                  pl.BlockSpec((t, _LANES), lambda bi, kp: (bi, v0 + kp)),
                  pl.BlockSpec((2, 2, SWA_PAIRS * rows, 1), lambda bi, kp: (kp, 0, 0, 0))],
        out_specs=pl.BlockSpec((t, gw2), lambda bi, kp: (bi, kp)),
        out_shape=jax.ShapeDtypeStruct((m, SWA_Q), _BF16),
        scratch_shapes=[pltpu.VMEM((2, 4, t, _LANES), _BF16)],
        compiler_params=_cparams(("arbitrary", "arbitrary"), est),
        name="swa_prompt",
    )(bq, fo, fo, sink_cols)


def _swa_cache_kernel(q_ref, kvn_ref, kc_ref, vc_ref, sink_ref, o_ref, *, L):
    gw = SWA_GROUP * SWA_HD
    kvn = kvn_ref[...]
    k_all = jnp.concatenate([kc_ref[0], kvn[:, :SWA_KV]], axis=0)
    v_all = jnp.concatenate([vc_ref[0], kvn[:, SWA_KV:]], axis=0)
    for kv in range(SWA_KV_HEADS):
        lanes = slice(_LANES * (kv // 2), _LANES * (kv // 2 + 1))
        slabs = _pair_slabs(k_all[:, lanes], kv % 2, fill=0.0) + _pair_slabs(v_all[:, lanes], kv % 2, fill=1.0)
        qst = jnp.concatenate(
            [q_ref[:, kv * gw + p * _LANES:kv * gw + (p + 1) * _LANES] for p in range(SWA_PAIRS)], axis=0)
        o = _swa_core(qst, *[x.astype(_BF16) for x in slabs], sink_ref[kv, 0], sink_ref[kv, 1], None)
        for p in range(SWA_PAIRS):
            lo = kv * gw + p * _LANES
            o_ref[:, lo:lo + _LANES] = o[p * L:(p + 1) * L].astype(o_ref.dtype)


def _swa_cache(bq, fo, k_cache, v_cache, sinks, b, t):
    m = bq.shape[0]
    rows = k_cache.shape[1]
    kc = k_cache.reshape(b, rows, SWA_KV)
    vc = v_cache.reshape(b, rows, SWA_KV)
    sink_cols = _sink_columns(sinks, t)
    assert (fo.shape[1] - 2 * SWA_KV) % (2 * SWA_KV) == 0
    kv0 = fo.shape[1] // (2 * SWA_KV) - 1
    est = 2 * (2 * _nbytes((t, SWA_Q), _BF16) + _nbytes((t, 2 * SWA_KV), _F32)
               + 2 * _nbytes((rows, SWA_KV), _F32)) + (4 << 20)
    return pl.pallas_call(
        functools.partial(_swa_cache_kernel, L=t),
        grid=(b,),
        in_specs=[pl.BlockSpec((t, SWA_Q), lambda bi: (bi, 3 * RET_QK // SWA_Q)),
                  pl.BlockSpec((t, 2 * SWA_KV), lambda bi: (bi, kv0)),
                  pl.BlockSpec((1, rows, SWA_KV), lambda bi: (bi, 0, 0)),
                  pl.BlockSpec((1, rows, SWA_KV), lambda bi: (bi, 0, 0)),
                  pl.BlockSpec((SWA_KV_HEADS, 2, SWA_PAIRS * t, 1), lambda bi: (0, 0, 0, 0))],
        out_specs=pl.BlockSpec((t, SWA_Q), lambda bi: (bi, 0)),
        out_shape=jax.ShapeDtypeStruct((m, SWA_Q), _BF16),
        compiler_params=_cparams(("arbitrary",), est),
        name="swa_cache",
    )(bq, fo, kc, vc, sink_cols)


def _bf16_weight(w_ref, wbf_ref):
    if wbf_ref is None:
        return w_ref[...]
    wbf_ref[...] = w_ref[...].astype(_BF16)
    return wbf_ref[...]


def _merge_kernel(a_ref, o_ref, wr_ref, ws_ref, ga_ref, gb_ref, out_ref, wrb_ref=None, wsb_ref=None):
    yr = jnp.dot(a_ref[...], _bf16_weight(wr_ref, wrb_ref), preferred_element_type=_F32)
    ys = jnp.dot(o_ref[...], _bf16_weight(ws_ref, wsb_ref), preferred_element_type=_F32)
    out_ref[...] = (jax.nn.sigmoid(ga_ref[...]) * yr + jax.nn.sigmoid(gb_ref[...]) * ys).astype(out_ref.dtype)


def _merge(a_ret, o_swa, fo, w_ret_o, w_swa_o):
    m = a_ret.shape[0]
    d = w_ret_o.shape[1]
    emit = w_ret_o.dtype == _F32
    tm = min(_TM, m)
    tn = _TN
    nd = d // tn
    g0 = RET_V // tn
    assert not emit or m == tm, "the bf16 copies are written by a single row tile"
    est = 2 * (_nbytes((tm, RET_V), _BF16) + _nbytes((tm, SWA_Q), _BF16) + _nbytes((RET_V, tn), w_ret_o.dtype)
               + _nbytes((SWA_Q, tn), w_swa_o.dtype) + 2 * _nbytes((tm, tn), _F32) + _nbytes((tm, tn), _BF16)
               ) + 4 * _nbytes((tm, tn), _F32)
    out_specs = [pl.BlockSpec((tm, tn), lambda i, j: (i, j))]
    out_shape = [jax.ShapeDtypeStruct((m, d), _BF16)]
    if emit:
        out_specs += [pl.BlockSpec((RET_V, tn), lambda i, j: (0, j)), pl.BlockSpec((SWA_Q, tn), lambda i, j: (0, j))]
        out_shape += [jax.ShapeDtypeStruct((RET_V, d), _BF16), jax.ShapeDtypeStruct((SWA_Q, d), _BF16)]
        est += 2 * (_nbytes((RET_V, tn), _BF16) + _nbytes((SWA_Q, tn), _BF16))
    return pl.pallas_call(
        _merge_kernel,
        grid=(m // tm, nd),
        in_specs=[pl.BlockSpec((tm, RET_V), lambda i, j: (i, 0)),
                  pl.BlockSpec((tm, SWA_Q), lambda i, j: (i, 0)),
                  pl.BlockSpec((RET_V, tn), lambda i, j: (0, j)),
                  pl.BlockSpec((SWA_Q, tn), lambda i, j: (0, j)),
                  pl.BlockSpec((tm, tn), lambda i, j: (i, g0 + j)),
                  pl.BlockSpec((tm, tn), lambda i, j: (i, g0 + nd + j))],
        out_specs=out_specs,
        out_shape=out_shape,
        compiler_params=_cparams(("arbitrary", "arbitrary"), est),
        name="merge_emit" if emit else "merge",
    )(a_ret, o_swa, w_ret_o, w_swa_o, fo, fo)


def _proj_residual_kernel(a_ref, w_ref, x_ref, gt_ref, o_ref, wbf_ref=None):
    f = jnp.dot(a_ref[...], _bf16_weight(w_ref, wbf_ref), preferred_element_type=_F32)
    o_ref[...] = x_ref[...] + _per_seq(f, gt_ref[...], lambda v, gt: gt * v)


def _proj_residual(a, w, x, gt, t, tm_cap, name):
    m, k = a.shape
    d = w.shape[1]
    emit = w.dtype == _F32
    tm = _row_tile(m, t, tm_cap)
    tn = _TN
    assert not emit or m == tm, "the bf16 copy is written by a single row tile"
    est = 2 * (_nbytes((tm, k), _BF16) + _nbytes((k, tn), w.dtype) + 2 * _nbytes((tm, tn), _F32)
               ) + 2 * _nbytes((tm, tn), _F32)
    out_specs = [pl.BlockSpec((tm, tn), lambda i, j: (i, j))]
    out_shape = [jax.ShapeDtypeStruct((m, d), _F32)]
    if emit:
        out_specs.append(pl.BlockSpec((k, tn), lambda i, j: (0, j)))
        out_shape.append(jax.ShapeDtypeStruct((k, d), _BF16))
        est += 2 * _nbytes((k, tn), _BF16)
    return pl.pallas_call(
        _proj_residual_kernel,
        grid=(m // tm, d // tn),
        in_specs=[pl.BlockSpec((tm, k), lambda i, j: (i, 0)),
                  pl.BlockSpec((k, tn), lambda i, j: (0, j)),
                  pl.BlockSpec((tm, tn), lambda i, j: (i, j)),
                  _mod_spec(tm, t, tn, lambda i, j: j)],
        out_specs=out_specs,
        out_shape=out_shape,
        compiler_params=_cparams(("arbitrary", "arbitrary"), est),
        name=name + "_emit" if emit else name,
    )(a, w, x, gt)


def _up_kernel(*refs, tps, nf, emit):
    if emit:
        h_ref, wu_ref, wv_ref, wc_ref, bc_ref, cs_ref, g_ref, tail_ref, wub_ref, wvb_ref, u_scr, v_scr, carry_ref = refs
    else:
        h_ref, wu_ref, wv_ref, wc_ref, bc_ref, cs_ref, g_ref, tail_ref, u_scr, v_scr, carry_ref = refs
        wub_ref = wvb_ref = None
    i = pl.program_id(0)
    j = pl.program_id(1)
    tm, tn = u_scr.shape

    def matmul():
        h = h_ref[...]
        u_scr[...] = jnp.dot(h, _bf16_weight(wu_ref, wub_ref), preferred_element_type=_F32)
        v_scr[...] = jnp.dot(h, _bf16_weight(wv_ref, wvb_ref), preferred_element_type=_F32)

    def epilogue():
        jp = j - 1
        u = u_scr[...]
        v = v_scr[...]
        cs = cs_ref[...]
        spt = cs.shape[0]
        rows = tm // spt
        c0 = cs[:, 0:1, :]
        c1 = cs[:, 1:2, :]
        if tps > 1:
            first = (i % tps) == 0
            prev = carry_ref[jp]
            c0 = jnp.where(first, c0, prev[_SUBLANES - 2:_SUBLANES - 1][None])
            c1 = jnp.where(first, c1, prev[_SUBLANES - 1:_SUBLANES][None])
            carry_ref[jp] = u[tm - _SUBLANES:tm]
        wc = wc_ref[...]
        bc = bc_ref[...]
        if spt == 1:
            r = lax.broadcasted_iota(jnp.int32, (tm, tn), 0)
            p1 = jnp.where(r == 0, c1[0], pltpu.roll(u, 1, 0))
            p2 = jnp.where(r == 0, c0[0], jnp.where(r == 1, c1[0], pltpu.roll(u, 2, 0)))
            uc = wc[0:1] * p2 + wc[1:2] * p1 + wc[2:3] * u + bc
            g_ref[...] = (jax.nn.gelu(uc) * v).astype(g_ref.dtype)
            tail_ref[0, 0] = u[tm - _SUBLANES:tm]
        else:
            u3 = u.reshape(spt, rows, tn)
            r = lax.broadcasted_iota(jnp.int32, (spt, rows, tn), 1)
            p1 = jnp.where(r == 0, c1, pltpu.roll(u3, 1, 1))
            p2 = jnp.where(r == 0, c0, jnp.where(r == 1, c1, pltpu.roll(u3, 2, 1)))
            uc = wc[0:1][None] * p2 + wc[1:2][None] * p1 + wc[2:3][None] * u3 + bc[None]
            g_ref[...] = (jax.nn.gelu(uc) * v.reshape(spt, rows, tn)).reshape(tm, tn).astype(g_ref.dtype)
            tail_ref[0] = u3[:, rows - _SUBLANES:rows, :]

    @pl.when(j == 0)
    def _first():
        if tps > 1:
            @pl.when(i == 0)
            def _init():
                carry_ref[...] = jnp.zeros_like(carry_ref)
        matmul()

    @pl.when((j > 0) & (j < nf))
    def _steady():
        epilogue()
        matmul()

    @pl.when(j == nf)
    def _drain():
        epilogue()


def _up_conv_geglu(h, w_u, w_v, w_conv, b_conv, conv_state, t):
    m, d = h.shape
    f = w_conv.shape[1]
    (wu_arr, wu_tile0), (wv_arr, wv_tile0) = w_u, w_v
    emit = wu_arr.dtype == _F32
    tm = _row_tile(m, t, _TM)
    tn = _TN_FF
    assert f % tn == 0, (f, tn)
    assert not emit or m == tm, "the bf16 copies are written by a single row tile"
    nf = f // tn
    spt, tps = _seq_block(tm, t)
    rows = tm // spt
    assert rows >= _SUBLANES and conv_state.shape[1] == CONV_W - 1
    cur = lambda j: jnp.minimum(j, nf - 1)
    prev = lambda j: jnp.maximum(j - 1, 0)
    est = 2 * (_nbytes((tm, d), _BF16) + 2 * _nbytes((d, tn), wu_arr.dtype) + _nbytes((tm, tn), _BF16)
               + _nbytes((spt, _SUBLANES, tn), _F32)) + _nbytes((nf, _SUBLANES, tn), _F32) + 10 * _nbytes((tm, tn), _F32)
    out_specs = [pl.BlockSpec((tm, tn), lambda i, j: (i, prev(j))),
                 pl.BlockSpec((1, spt, _SUBLANES, tn), lambda i, j: (i, 0, 0, prev(j)))]
    out_shape = [jax.ShapeDtypeStruct((m, f), _BF16),
                 jax.ShapeDtypeStruct((m // tm, spt, _SUBLANES, f), _F32)]
    if emit:
        out_specs += [pl.BlockSpec((d, tn), lambda i, j: (0, cur(j))) for _ in range(2)]
        out_shape += [jax.ShapeDtypeStruct((d, f), _BF16) for _ in range(2)]
        est += 4 * _nbytes((d, tn), _BF16)
    return pl.pallas_call(
        functools.partial(_up_kernel, tps=tps, nf=nf, emit=emit),
        grid=(m // tm, nf + 1),
        in_specs=[pl.BlockSpec((tm, d), lambda i, j: (i, 0)),
                  pl.BlockSpec((d, tn), lambda i, j: (0, wu_tile0 + cur(j))),
                  pl.BlockSpec((d, tn), lambda i, j: (0, wv_tile0 + cur(j))),
                  pl.BlockSpec((CONV_W, tn), lambda i, j: (0, prev(j))),
                  pl.BlockSpec((1, tn), lambda i, j: (0, prev(j))),
                  pl.BlockSpec((spt, CONV_W - 1, tn), lambda i, j: (i // tps, 0, prev(j)))],
        out_specs=out_specs,
        out_shape=out_shape,
        scratch_shapes=[pltpu.VMEM((tm, tn), _F32), pltpu.VMEM((tm, tn), _F32),
                        pltpu.VMEM((nf, _SUBLANES, tn), _F32)],
        compiler_params=_cparams(("arbitrary", "arbitrary"), est),
        name="up_conv_geglu_emit" if emit else "up_conv_geglu",
    )(h, wu_arr, wv_arr, w_conv, b_conv.reshape(1, f), conv_state)


def _layer(x, mod, pos0, ret_state, swa_kc, swa_vc, conv_state, w):
    b, t, d = x.shape
    m = b * t
    x2 = x.reshape(m, d)
    sh1, sc1, gt1, sh2, sc2, gt2 = [v.reshape(b, 1, d) for v in jnp.split(mod, 6, axis=-1)]
    bf = {}

    h = _norm_mod(x2, w["g_mix"], sc1, sh1, t)
    bq, fo, *w_bf = _in_proj(h, w["w_in"], t, pos0)
    bf.update(zip(("w_in",), w_bf))
    a_ret, s_new = _retention(bq, fo, ret_state, b, t)
    if swa_kc is None:
        o_swa = _swa_prompt(bq, fo, w["swa_sinks"], b, t)
        keep = min(WINDOW, t)
    else:
        o_swa = _swa_cache(bq, fo, swa_kc, swa_vc, w["swa_sinks"], b, t)
        keep = t
    merged, *w_bf = _merge(a_ret, o_swa, fo, w["w_ret_o"], w["w_swa_o"])
    bf.update(zip(("w_ret_o", "w_swa_o"), w_bf))
    x1, *w_bf = _proj_residual(merged, w["w_out"], x2, gt1, t, _TM, "out_proj")
    bf.update(zip(("w_out",), w_bf))

    h2 = _norm_mod(x1, w["g_ffn"], sc2, sh2, t)
    f = w["b_conv"].shape[0]
    cs = jnp.zeros((b, CONV_W - 1, f), _F32) if conv_state is None else conv_state
    if "w_up_u" in w:
        w_u, w_v = (w["w_up_u"], 0), (w["w_up_v"], 0)
    else:
        w_u, w_v = (w["w_up"], 0), (w["w_up"], f // _TN_FF)
    g, tails, *w_bf = _up_conv_geglu(h2, w_u, w_v, w["w_conv"], w["b_conv"], cs, t)
    bf.update(zip(("w_up_u", "w_up_v"), w_bf))
    x3, = _proj_residual(g, w["w_down"], x1, gt2, t, _TM_DOWN, "down_proj")
    y = _final_norm(x3, w["g_final"])

    skv = fo[:, fo.shape[1] - 2 * SWA_KV:].reshape(b, t, 2, SWA_KV_HEADS, SWA_HD)[:, t - keep:]
    k_new = skv[:, :, 0]
    v_new = skv[:, :, 1]
    segs = tails.shape[0] * tails.shape[1] // b
    tails = tails.reshape(b, segs, _SUBLANES, f)
    conv_new = tails[:, segs - 1, _SUBLANES - (CONV_W - 1):, :]
    return y.reshape(b, t, d), s_new, k_new, v_new, conv_new, bf


def kernel(x_prompt, x_sample, cache_swa_k, cache_swa_v, state_ret, state_conv, c_prompt, c_sample,
           g_mix, g_ffn, w_ada, b_ada, w_in, swa_sinks, w_ret_o, w_swa_o, w_out,
           w_up, w_conv, b_conv, w_down, g_final):
    depth = g_mix.shape[0]
    assert depth == 1, "single-layer stack"
    bp = x_prompt.shape[0]
    bs = x_sample.shape[0]
    l = 0

    c_all = jnp.concatenate([c_prompt, c_sample], axis=0)
    rows = -(-c_all.shape[0] // 16) * 16
    c_all = jnp.pad(c_all, ((0, rows - c_all.shape[0]), (0, 0)))
    mod = _ada(c_all, w_ada[l], b_ada[l])

    w = {
        "g_mix": g_mix[l], "g_ffn": g_ffn[l], "g_final": g_final, "swa_sinks": swa_sinks[l],
        "w_in": w_in[l], "w_ret_o": w_ret_o[l], "w_swa_o": w_swa_o[l], "w_out": w_out[l], "w_up": w_up[l],
        "w_down": w_down[l].astype(_BF16),
        "w_conv": w_conv[l], "b_conv": b_conv[l],
    }

    ys, ss, ks, vs, cs, w_bf = _layer(x_sample, mod[bp:bp + bs], PAST_LEN, state_ret[l], cache_swa_k[l],
                                      cache_swa_v[l], state_conv[l], w)
    yp, sp, kp, vp, cp, _ = _layer(x_prompt, mod[:bp], 0, None, None, None, None, {**w, **w_bf})
    stack = lambda a: a[None]
    return (yp, ys, stack(sp), stack(kp), stack(vp), stack(cp), stack(ss), stack(ks), stack(vs), stack(cs))
```

```python
import functools

import jax
import jax.numpy as jnp
import numpy as np
from jax import lax
from jax.experimental import pallas as pl
from jax.experimental.pallas import tpu as pltpu

_F32 = jnp.float32
_BF16 = jnp.bfloat16

CHUNK = 64
RET_HEADS = 8
RET_DK = 256
RET_DV = 256
RET_QK = RET_HEADS * RET_DK
RET_V = RET_HEADS * RET_DV
ROPE_BASE = 10000.0
SWA_HEADS = 32
SWA_KV_HEADS = 4
SWA_HD = 64
SWA_GROUP = SWA_HEADS // SWA_KV_HEADS
SWA_PAIRS = SWA_GROUP // 2
SWA_Q = SWA_HEADS * SWA_HD
SWA_KV = SWA_KV_HEADS * SWA_HD
WINDOW = 128
WINDOW_CHUNKS = WINDOW // CHUNK
CONV_W = 3
EPS = 1e-6
NEG_INF = -1e30
PAST_LEN = 2048

_V7X_VMEM_BYTES = 64 * 1024 * 1024
_LANES = 128
_SUBLANES = 8

_TM = 1024
_TM_DOWN = 512
_TM_NORM = 256
_TN = 2 * SWA_KV
_TN_EMIT = 256
_TN_ADA = 512
_RET_CHUNK = 256
_SWA_QB = 2
_TN_FF = 256
_UP_TILES = 2


def _cparams(sem, vmem_estimate):
    limit = min(int(vmem_estimate * 1.15) + (4 << 20), _V7X_VMEM_BYTES - (6 << 20))
    return pltpu.CompilerParams(dimension_semantics=sem, vmem_limit_bytes=limit)


def _nbytes(shape, dtype):
    return int(np.prod(shape)) * jnp.dtype(dtype).itemsize


def _row_tile(m, t, cap):
    tm = min(cap, m)
    assert m % tm == 0 and (t % tm == 0 or tm % t == 0), (m, t, tm)
    return tm


def _seq_block(tm, t):
    return max(1, tm // t), max(1, t // tm)


def _mod_spec(tm, t, tn, col):
    spt, tps = _seq_block(tm, t)
    return pl.BlockSpec((spt, 1, tn), lambda i, j: (i // tps, 0, col(i, j)))


def _per_seq(val, mod, fn):
    spt = mod.shape[0]
    if spt == 1:
        return fn(val, mod[0])
    tm, n = val.shape
    return fn(val.reshape(spt, tm // spt, n), mod).reshape(tm, n)


def _ada_kernel(c_ref, w_ref, b_ref, o_ref):
    c = c_ref[...]
    a = (c * jax.nn.sigmoid(c)).astype(_BF16)
    o_ref[...] = jnp.dot(a, w_ref[...].astype(_BF16), preferred_element_type=_F32) + b_ref[...]


def _ada(c, w_ada, b_ada):
    rows, d = c.shape
    n = w_ada.shape[1]
    tn = _TN_ADA
    est = 2 * (_nbytes((rows, d), _F32) + _nbytes((d, tn), _F32) + _nbytes((rows, tn), _F32)) + _nbytes((d, tn), _BF16)
    return pl.pallas_call(
        _ada_kernel,
        grid=(n // tn,),
        in_specs=[pl.BlockSpec((rows, d), lambda j: (0, 0)),
                  pl.BlockSpec((d, tn), lambda j: (0, j)),
                  pl.BlockSpec((1, tn), lambda j: (0, j))],
        out_specs=pl.BlockSpec((rows, tn), lambda j: (0, j)),
        out_shape=jax.ShapeDtypeStruct((rows, n), _F32),
        compiler_params=_cparams(("arbitrary",), est),
        name="ada_mod",
    )(c, w_ada, b_ada.reshape(1, n))


def _rms(x, g):
    y = x * lax.rsqrt(jnp.mean(x * x, axis=-1, keepdims=True) + EPS)
    return y * g


def _norm_mod_kernel(x_ref, g_ref, sc_ref, sh_ref, o_ref):
    y = _rms(x_ref[...], g_ref[...])
    h = _per_seq(y, sc_ref[...], lambda v, sc: v * (1.0 + sc))
    h = _per_seq(h, sh_ref[...], lambda v, sh: v + sh)
    o_ref[...] = h.astype(o_ref.dtype)


def _norm_mod(x, g, sc, sh, t):
    m, d = x.shape
    tm = _row_tile(m, t, _TM_NORM)
    est = 2 * (_nbytes((tm, d), _F32) + _nbytes((tm, d), _BF16)) + 3 * _nbytes((tm, d), _F32)
    return pl.pallas_call(
        _norm_mod_kernel,
        grid=(m // tm, 1),
        in_specs=[pl.BlockSpec((tm, d), lambda i, j: (i, 0)),
                  pl.BlockSpec((1, d), lambda i, j: (0, 0)),
                  _mod_spec(tm, t, d, lambda i, j: 0),
                  _mod_spec(tm, t, d, lambda i, j: 0)],
        out_specs=pl.BlockSpec((tm, d), lambda i, j: (i, 0)),
        out_shape=jax.ShapeDtypeStruct((m, d), _BF16),
        compiler_params=_cparams(("arbitrary", "arbitrary"), est),
        name="norm_mod",
    )(x, g.reshape(1, d), sc, sh)


def _final_norm_kernel(x_ref, g_ref, o_ref):
    o_ref[...] = _rms(x_ref[...], g_ref[...])


def _final_norm(x, g):
    m, d = x.shape
    tm = min(_TM_NORM, m)
    est = 4 * _nbytes((tm, d), _F32) + 2 * _nbytes((tm, d), _F32)
    return pl.pallas_call(
        _final_norm_kernel,
        grid=(m // tm,),
        in_specs=[pl.BlockSpec((tm, d), lambda i: (i, 0)),
                  pl.BlockSpec((1, d), lambda i: (0, 0))],
        out_specs=pl.BlockSpec((tm, d), lambda i: (i, 0)),
        out_shape=jax.ShapeDtypeStruct((m, d), _F32),
        compiler_params=_cparams(("arbitrary",), est),
        name="final_norm",
    )(x, g.reshape(1, d))


def _pair_slabs(slab, parity, fill):
    low = lax.broadcasted_iota(jnp.int32, slab.shape, 1) < SWA_HD
    if parity == 0:
        even = jnp.where(low, slab, 0.0)
        odd = pltpu.roll(even, SWA_HD, 1)
    else:
        odd = jnp.where(low, 0.0, slab)
        even = pltpu.roll(odd, SWA_HD, 1)
    if fill == 0.0:
        return even, odd
    return jnp.where(low, even, fill), jnp.where(low, fill, odd)


class _InPlan:
    def __init__(self, tn, pair, d):
        w = tn * pair
        n_skv = 2 * SWA_KV // tn
        assert RET_QK % w == 0 and SWA_Q % w == 0 and (2 * d) % w == 0 and (RET_V + 2 * d) % w == 0
        assert tn % RET_DK == 0 and (2 * SWA_KV) % tn == 0 and (pair == 1 or n_skv == 1)
        self.tn, self.pair, self.w = tn, pair, w
        r = self.r = RET_QK // w
        self.t_skv = 5 * r
        self.t_gab = self.t_skv + n_skv
        self.steps = self.t_gab + 2 * d // w
        self.skv_block = (RET_V + 2 * d) // w
        self.fo_cols = RET_V + 2 * d + 2 * SWA_KV

    def w_tile(self, t):
        p, ts, tg = self.pair, self.t_skv, self.t_gab
        return jnp.where(t < ts, p * t, jnp.where(t < tg, p * ts + (t - ts), p * ts + (tg - ts) + p * (t - tg)))

    def bq_block(self, t):
        r = self.r
        return jnp.where(t < 3 * r, t, jnp.where(t < 4 * r, 3 * r - 1, jnp.where(t < 5 * r, t - r, 4 * r - 1)))

    def fo_block(self, t):
        r, ts, tg = self.r, self.t_skv, self.t_gab
        return jnp.where(t < 3 * r, 0, jnp.where(t < 4 * r, t - 3 * r, jnp.where(
            t < ts, r - 1, jnp.where(t < tg, self.skv_block + (t - ts), r + (t - tg)))))


def _in_proj_kernel(*refs, plan, emit):
    n_w = plan.pair
    h_ref, w_refs, (cos_ref, sin_ref, bq_ref, fo_ref) = refs[0], refs[1:1 + n_w], refs[1 + n_w:5 + n_w]
    if emit:
        wbf_ref, = refs[5 + n_w:]
        wbf_ref[...] = w_refs[0][...].astype(_BF16)
        w_refs = (wbf_ref,)
    t = pl.program_id(1)
    r, tn = plan.r, plan.tn
    half = RET_DK // 2

    def tiles():
        for n, w_ref in enumerate(w_refs):
            yield jnp.dot(h_ref[...], w_ref[...], preferred_element_type=_F32), n * tn

    @pl.when(t < 2 * r)
    def _rope():
        c = cos_ref[...]
        s = sin_ref[...]
        scale = jnp.where(t >= r, RET_DK ** -0.5, 1.0).astype(_F32)
        for acc, col in tiles():
            for hh in range(tn // RET_DK):
                lo = hh * RET_DK
                x1 = acc[:, lo:lo + half]
                x2 = acc[:, lo + half:lo + RET_DK]
                bq_ref[:, col + lo:col + lo + half] = ((x1 * c - x2 * s) * scale).astype(_BF16)
                bq_ref[:, col + lo + half:col + lo + RET_DK] = ((x2 * c + x1 * s) * scale).astype(_BF16)

    @pl.when(((t >= 2 * r) & (t < 3 * r)) | ((t >= 4 * r) & (t < 5 * r)))
    def _bf16():
        for acc, col in tiles():
            bq_ref[:, col:col + tn] = acc.astype(_BF16)

    @pl.when(((t >= 3 * r) & (t < 4 * r)) | (t >= plan.t_gab))
    def _f32():
        for acc, col in tiles():
            fo_ref[:, col:col + tn] = acc

    @pl.when((t >= plan.t_skv) & (t < plan.t_gab))
    def _kv():
        fo_ref[:, 0:tn] = jnp.dot(h_ref[...], w_refs[0][...], preferred_element_type=_F32)


def _rope_tables(t, pos0, tm):
    half = RET_DK // 2
    inv = ROPE_BASE ** (-jnp.arange(half, dtype=_F32) * (2.0 / RET_DK))
    pos = (pos0 + jnp.arange(t, dtype=jnp.int32)).astype(_F32)
    ang = pos[:, None] * inv[None, :]
    reps = max(1, tm // t)
    return jnp.tile(jnp.cos(ang), (reps, 1)), jnp.tile(jnp.sin(ang), (reps, 1))


def _in_proj(h, w_in, t, pos0):
    m, d = h.shape
    n_in = w_in.shape[1]
    emit = w_in.dtype == _F32
    tm = _row_tile(m, t, _TM)
    plan = _InPlan(_TN_EMIT, 1, d) if emit else _InPlan(_TN, 2, d)
    tn, w = plan.tn, plan.w
    assert n_in == plan.fo_cols + 4 * RET_QK, (n_in, d)
    assert not emit or m == tm, "the bf16 copy is written by a single row tile"
    cos, sin = _rope_tables(t, pos0, tm)
    tab_blocks = cos.shape[0] // tm
    half = RET_DK // 2
    last_tile = n_in // tn - 1

    out_shape = [jax.ShapeDtypeStruct((m, 4 * RET_QK), _BF16), jax.ShapeDtypeStruct((m, plan.fo_cols), _F32)]
    out_specs = [pl.BlockSpec((tm, w), lambda i, j: (i, plan.bq_block(j))),
                 pl.BlockSpec((tm, w), lambda i, j: (i, plan.fo_block(j)))]
    w_specs = [pl.BlockSpec((d, tn), lambda i, j, n=n: (0, jnp.minimum(plan.w_tile(j) + n, last_tile)))
               for n in range(plan.pair)]
    est = 2 * (_nbytes((tm, d), _BF16) + plan.pair * _nbytes((d, tn), w_in.dtype) + 2 * _nbytes((tm, half), _F32)
               + _nbytes((tm, w), _BF16) + _nbytes((tm, w), _F32)) + 2 * plan.pair * _nbytes((tm, tn), _F32)
    if emit:
        out_shape.append(jax.ShapeDtypeStruct((d, n_in), _BF16))
        out_specs.append(pl.BlockSpec((d, tn), lambda i, j: (0, plan.w_tile(j))))
        est += 2 * _nbytes((d, tn), _BF16)
    return pl.pallas_call(
        functools.partial(_in_proj_kernel, plan=plan, emit=emit),
        grid=(m // tm, plan.steps),
        in_specs=[pl.BlockSpec((tm, d), lambda i, j: (i, 0)), *w_specs,
                  pl.BlockSpec((tm, half), lambda i, j: (i % tab_blocks, 0)),
                  pl.BlockSpec((tm, half), lambda i, j: (i % tab_blocks, 0))],
        out_specs=out_specs,
        out_shape=out_shape,
        compiler_params=_cparams(("arbitrary", "arbitrary"), est),
        name="in_proj_emit" if emit else "in_proj",
    )(h, *([w_in] * plan.pair), cos, sin)


def _retention_kernel(*refs, hb, cc, n_chunks, has_state):
    if has_state:
        lg_ref, q_ref, k_ref, v_ref, rg_ref, s0_ref, a_ref, sout_ref, s_scr = refs
    else:
        lg_ref, q_ref, k_ref, v_ref, rg_ref, a_ref, sout_ref, s_scr = refs
    row = lax.broadcasted_iota(jnp.int32, (cc, cc), 0)
    col = lax.broadcasted_iota(jnp.int32, (cc, cc), 1)
    diff = (row - col).astype(_F32)
    idx = lax.broadcasted_iota(jnp.int32, (cc, 1), 0).astype(_F32)
    for hh in range(hb):
        lg = lg_ref[hh][:, :1]
        dmask = jnp.where(diff >= 0, jnp.exp(lg * jnp.maximum(diff, 0.0)), 0.0)
        dq = jnp.exp(lg * (idx + 1.0))
        dk = jnp.exp(lg * (cc - 1.0 - idx))
        ds = jnp.exp(lg * float(cc))
        cols = slice(hh * RET_DK, (hh + 1) * RET_DK)
        if has_state:
            s_scr[...] = s0_ref[0, hh]
        else:
            s_scr[...] = jnp.zeros_like(s_scr)

        def body(c, carry):
            r = pl.ds(pl.multiple_of(c * cc, cc), cc)
            q = q_ref[r, cols]
            k = k_ref[r, cols]
            v = v_ref[r, cols]
            state = s_scr[...]
            scores = lax.dot_general(q, k, (((1,), (1,)), ((), ())), preferred_element_type=_F32) * dmask
            y = jnp.dot(scores.astype(_BF16), v, preferred_element_type=_F32)
            y = y + jnp.dot(q, state.astype(_BF16), preferred_element_type=_F32) * dq
            k_dec = (k.astype(_F32) * dk).astype(_BF16)
            s_scr[...] = state * ds + lax.dot_general(
                k_dec, v, (((0,), (0,)), ((), ())), preferred_element_type=_F32)
            mu = jnp.mean(y, axis=-1, keepdims=True)
            dev = y - mu
            var = jnp.mean(dev * dev, axis=-1, keepdims=True)
            yn = dev * lax.rsqrt(var + EPS)
            g = rg_ref[r, cols]
            a_ref[r, cols] = (g * jax.nn.sigmoid(g) * yn).astype(_BF16)
            return carry

        lax.fori_loop(0, n_chunks, body, 0, unroll=2 if n_chunks % 2 == 0 else 1)
        sout_ref[0, hh] = s_scr[...]


def _retention(qkv, rg, state, b, t):
    m = qkv.shape[0]
    cc = min(_RET_CHUNK, t)
    assert t % cc == 0 and cc % min(CHUNK, t) == 0
    hb = RET_HEADS if t * RET_QK * 2 <= (1 << 20) else 1
    nhb = RET_HEADS // hb
    wcols = hb * RET_DK
    h = jnp.arange(RET_HEADS, dtype=_F32)
    lg = jnp.log1p(-jnp.exp2(-5.0 - h))
    lg = jnp.broadcast_to(lg[:, None, None], (RET_HEADS, 1, _LANES))
    has_state = state is not None
    in_specs = [pl.BlockSpec((hb, 1, _LANES), lambda bi, hi: (hi, 0, 0)),
                pl.BlockSpec((t, wcols), lambda bi, hi: (bi, hi)),
                pl.BlockSpec((t, wcols), lambda bi, hi: (bi, nhb + hi)),
                pl.BlockSpec((t, wcols), lambda bi, hi: (bi, 2 * nhb + hi)),
                pl.BlockSpec((t, wcols), lambda bi, hi: (bi, hi))]
    args = [lg, qkv, qkv, qkv, rg]
    s_spec = pl.BlockSpec((1, hb, RET_DK, RET_DV), lambda bi, hi: (bi, hi, 0, 0))
    if has_state:
        in_specs.append(s_spec)
        args.append(state)
    est = 2 * (4 * _nbytes((t, wcols), _BF16) + _nbytes((t, wcols), _F32)
               + 2 * _nbytes((hb, RET_DK, RET_DV), _F32)) + 8 * _nbytes((cc, max(cc, RET_DV)), _F32)
    kern = functools.partial(_retention_kernel, hb=hb, cc=cc, n_chunks=t // cc, has_state=has_state)
    return pl.pallas_call(
        kern,
        grid=(b, nhb),
        in_specs=in_specs,
        out_specs=(pl.BlockSpec((t, wcols), lambda bi, hi: (bi, hi)), s_spec),
        out_shape=(jax.ShapeDtypeStruct((m, RET_V), _BF16),
                   jax.ShapeDtypeStruct((b, RET_HEADS, RET_DK, RET_DV), _F32)),
        scratch_shapes=[pltpu.VMEM((RET_DK, RET_DV), _F32)],
        compiler_params=_cparams(("arbitrary", "arbitrary"), est),
        name="retention",
    )(*args)


def _swa_core(qst, k_even, k_odd, v_even, v_odd, sink_even, sink_odd, valid):
    def weights(kx, sink):
        s = lax.dot_general(qst, kx, (((1,), (1,)), ((), ())), preferred_element_type=_F32) * (SWA_HD ** -0.5)
        if valid is not None:
            s = jnp.where(valid, s, NEG_INF)
        mx = jnp.maximum(jnp.max(s, axis=-1, keepdims=True), sink)
        return jnp.exp(s - mx).astype(_BF16), jnp.exp(sink - mx)

    e_even, t_even = weights(k_even, sink_even)
    e_odd, t_odd = weights(k_odd, sink_odd)
    o_even = jnp.dot(e_even, v_even, preferred_element_type=_F32)
    o_odd = jnp.dot(e_odd, v_odd, preferred_element_type=_F32)
    low = lax.broadcasted_iota(jnp.int32, o_even.shape, 1) < SWA_HD
    num = jnp.where(low, o_even, o_odd)
    den = jnp.where(low, pltpu.roll(o_even, SWA_HD, 1) + t_even, pltpu.roll(o_odd, SWA_HD, 1) + t_odd)
    return num / den


def _swa_prompt_kernel(q_ref, k_ref, v_ref, sink_ref, o_ref, slab_ref, *, n_blocks, unroll):
    gw = SWA_GROUP * SWA_HD
    rows = _SWA_QB * CHUNK
    win = _SWA_QB + WINDOW_CHUNKS

    for par in range(2):
        slabs = _pair_slabs(k_ref[...], par, fill=0.0) + _pair_slabs(v_ref[...], par, fill=1.0)
        for c, x in enumerate(slabs):
            slab_ref[par, c] = x.astype(_BF16)

    def valid_mask(n_key_chunks, first):
        shape = (SWA_PAIRS * rows, n_key_chunks * CHUNK)
        qc = (lax.broadcasted_iota(jnp.int32, shape, 0) // CHUNK) % _SWA_QB
        kc = lax.broadcasted_iota(jnp.int32, shape, 1) // CHUNK
        if first:
            return kc <= qc
        return (kc >= qc) & (kc <= qc + WINDOW_CHUNKS)

    valid_first = valid_mask(_SWA_QB, True)
    valid = valid_mask(win, False)
    for par in range(2):
        sink_even = sink_ref[par, 0]
        sink_odd = sink_ref[par, 1]
        lane0 = par * gw

        def block(r, kr, mask):
            qst = jnp.concatenate(
                [q_ref[r, lane0 + p * _LANES:lane0 + (p + 1) * _LANES] for p in range(SWA_PAIRS)], axis=0)
            o = _swa_core(qst, slab_ref[par, 0, kr, :], slab_ref[par, 1, kr, :],
                          slab_ref[par, 2, kr, :], slab_ref[par, 3, kr, :], sink_even, sink_odd, mask)
            for p in range(SWA_PAIRS):
                o_ref[r, lane0 + p * _LANES:lane0 + (p + 1) * _LANES] = (
                    o[p * rows:(p + 1) * rows].astype(o_ref.dtype))

        block(pl.ds(0, rows), pl.ds(0, rows), valid_first)
        if n_blocks > 1:
            def body(b, carry):
                r = pl.ds(pl.multiple_of(b * rows, rows), rows)
                kr = pl.ds(pl.multiple_of(b * rows - WINDOW_CHUNKS * CHUNK, CHUNK), win * CHUNK)
                block(r, kr, valid)
                return carry

            lax.fori_loop(1, n_blocks, body, 0, unroll=unroll)


def _sink_columns(sinks, rows):
    s = sinks.astype(_F32).reshape(SWA_KV_HEADS, SWA_PAIRS, 2).transpose(0, 2, 1)
    return jnp.repeat(s, rows, axis=-1)[..., None]


def _swa_prompt(bq, fo, sinks, b, t):
    m = bq.shape[0]
    gw2 = 2 * SWA_GROUP * SWA_HD
    rows = _SWA_QB * CHUNK
    assert t % rows == 0 and WINDOW_CHUNKS == _SWA_QB
    n_blocks = t // rows
    unroll = next(u for u in (3, 2, 1) if (n_blocks - 1) % u == 0)
    sink_cols = _sink_columns(sinks, rows)
    q0 = 3 * RET_QK // gw2
    k0 = (fo.shape[1] - 2 * SWA_KV) // _LANES
    v0 = k0 + SWA_KV // _LANES
    est = 2 * (2 * _nbytes((t, gw2), _BF16) + 2 * _nbytes((t, _LANES), _F32)) + _nbytes((8, t, _LANES), _BF16) + (8 << 20)
    return pl.pallas_call(
        functools.partial(_swa_prompt_kernel, n_blocks=n_blocks, unroll=unroll),
        grid=(b, SWA_KV_HEADS // 2),
        in_specs=[pl.BlockSpec((t, gw2), lambda bi, kp: (bi, q0 + kp)),
                  pl.BlockSpec((t, _LANES), lambda bi, kp: (bi, k0 + kp)),
                  pl.BlockSpec((t, _LANES), lambda bi, kp: (bi, v0 + kp)),
                  pl.BlockSpec((2, 2, SWA_PAIRS * rows, 1), lambda bi, kp: (kp, 0, 0, 0))],
        out_specs=pl.BlockSpec((t, gw2), lambda bi, kp: (bi, kp)),
        out_shape=jax.ShapeDtypeStruct((m, SWA_Q), _BF16),
        scratch_shapes=[pltpu.VMEM((2, 4, t, _LANES), _BF16)],
        compiler_params=_cparams(("arbitrary", "arbitrary"), est),
        name="swa_prompt",
    )(bq, fo, fo, sink_cols)


def _swa_cache_kernel(q_ref, kvn_ref, kc_ref, vc_ref, sink_ref, o_ref, *, L):
    gw = SWA_GROUP * SWA_HD
    kvn = kvn_ref[...]
    k_all = jnp.concatenate([kc_ref[0], kvn[:, :SWA_KV]], axis=0)
    v_all = jnp.concatenate([vc_ref[0], kvn[:, SWA_KV:]], axis=0)
    for kv in range(SWA_KV_HEADS):
        lanes = slice(_LANES * (kv // 2), _LANES * (kv // 2 + 1))
        slabs = _pair_slabs(k_all[:, lanes], kv % 2, fill=0.0) + _pair_slabs(v_all[:, lanes], kv % 2, fill=1.0)
        qst = jnp.concatenate(
            [q_ref[:, kv * gw + p * _LANES:kv * gw + (p + 1) * _LANES] for p in range(SWA_PAIRS)], axis=0)
        o = _swa_core(qst, *[x.astype(_BF16) for x in slabs], sink_ref[kv, 0], sink_ref[kv, 1], None)
        for p in range(SWA_PAIRS):
            lo = kv * gw + p * _LANES
            o_ref[:, lo:lo + _LANES] = o[p * L:(p + 1) * L].astype(o_ref.dtype)


def _swa_cache(bq, fo, k_cache, v_cache, sinks, b, t):
    m = bq.shape[0]
    rows = k_cache.shape[1]
    kc = k_cache.reshape(b, rows, SWA_KV)
    vc = v_cache.reshape(b, rows, SWA_KV)
    sink_cols = _sink_columns(sinks, t)
    assert (fo.shape[1] - 2 * SWA_KV) % (2 * SWA_KV) == 0
    kv0 = fo.shape[1] // (2 * SWA_KV) - 1
    est = 2 * (2 * _nbytes((t, SWA_Q), _BF16) + _nbytes((t, 2 * SWA_KV), _F32)
               + 2 * _nbytes((rows, SWA_KV), _F32)) + (4 << 20)
    return pl.pallas_call(
        functools.partial(_swa_cache_kernel, L=t),
        grid=(b,),
        in_specs=[pl.BlockSpec((t, SWA_Q), lambda bi: (bi, 3 * RET_QK // SWA_Q)),
                  pl.BlockSpec((t, 2 * SWA_KV), lambda bi: (bi, kv0)),
                  pl.BlockSpec((1, rows, SWA_KV), lambda bi: (bi, 0, 0)),
                  pl.BlockSpec((1, rows, SWA_KV), lambda bi: (bi, 0, 0)),
                  pl.BlockSpec((SWA_KV_HEADS, 2, SWA_PAIRS * t, 1), lambda bi: (0, 0, 0, 0))],
        out_specs=pl.BlockSpec((t, SWA_Q), lambda bi: (bi, 0)),
        out_shape=jax.ShapeDtypeStruct((m, SWA_Q), _BF16),
        compiler_params=_cparams(("arbitrary",), est),
        name="swa_cache",
    )(bq, fo, kc, vc, sink_cols)


def _bf16_weight(w_ref, wbf_ref):
    if wbf_ref is None:
        return w_ref[...]
    wbf_ref[...] = w_ref[...].astype(_BF16)
    return wbf_ref[...]


def _merge_kernel(a_ref, o_ref, wr_ref, ws_ref, ga_ref, gb_ref, out_ref, wrb_ref=None, wsb_ref=None):
    yr = jnp.dot(a_ref[...], _bf16_weight(wr_ref, wrb_ref), preferred_element_type=_F32)
    ys = jnp.dot(o_ref[...], _bf16_weight(ws_ref, wsb_ref), preferred_element_type=_F32)
    out_ref[...] = (jax.nn.sigmoid(ga_ref[...]) * yr + jax.nn.sigmoid(gb_ref[...]) * ys).astype(out_ref.dtype)


def _merge(a_ret, o_swa, fo, w_ret_o, w_swa_o):
    m = a_ret.shape[0]
    d = w_ret_o.shape[1]
    emit = w_ret_o.dtype == _F32
    tm = min(_TM, m)
    tn = _TN
    nd = d // tn
    g0 = RET_V // tn
    assert not emit or m == tm, "the bf16 copies are written by a single row tile"
    est = 2 * (_nbytes((tm, RET_V), _BF16) + _nbytes((tm, SWA_Q), _BF16) + _nbytes((RET_V, tn), w_ret_o.dtype)
               + _nbytes((SWA_Q, tn), w_swa_o.dtype) + 2 * _nbytes((tm, tn), _F32) + _nbytes((tm, tn), _BF16)
               ) + 4 * _nbytes((tm, tn), _F32)
    out_specs = [pl.BlockSpec((tm, tn), lambda i, j: (i, j))]
    out_shape = [jax.ShapeDtypeStruct((m, d), _BF16)]
    if emit:
        out_specs += [pl.BlockSpec((RET_V, tn), lambda i, j: (0, j)), pl.BlockSpec((SWA_Q, tn), lambda i, j: (0, j))]
        out_shape += [jax.ShapeDtypeStruct((RET_V, d), _BF16), jax.ShapeDtypeStruct((SWA_Q, d), _BF16)]
        est += 2 * (_nbytes((RET_V, tn), _BF16) + _nbytes((SWA_Q, tn), _BF16))
    return pl.pallas_call(
        _merge_kernel,
        grid=(m // tm, nd),
        in_specs=[pl.BlockSpec((tm, RET_V), lambda i, j: (i, 0)),
                  pl.BlockSpec((tm, SWA_Q), lambda i, j: (i, 0)),
                  pl.BlockSpec((RET_V, tn), lambda i, j: (0, j)),
                  pl.BlockSpec((SWA_Q, tn), lambda i, j: (0, j)),
                  pl.BlockSpec((tm, tn), lambda i, j: (i, g0 + j)),
                  pl.BlockSpec((tm, tn), lambda i, j: (i, g0 + nd + j))],
        out_specs=out_specs,
        out_shape=out_shape,
        compiler_params=_cparams(("arbitrary", "arbitrary"), est),
        name="merge_emit" if emit else "merge",
    )(a_ret, o_swa, w_ret_o, w_swa_o, fo, fo)


def _proj_residual_kernel(*refs, tiles):
    a_ref, w_refs, (x_ref, gt_ref, o_ref) = refs[0], refs[1:1 + tiles], refs[1 + tiles:4 + tiles]
    wbf_ref = refs[4 + tiles] if len(refs) > 4 + tiles else None
    tn = w_refs[0].shape[1]
    for c, w_ref in enumerate(w_refs):
        lanes = slice(c * tn, (c + 1) * tn)
        f = jnp.dot(a_ref[...], _bf16_weight(w_ref, wbf_ref), preferred_element_type=_F32)
        o_ref[:, lanes] = x_ref[:, lanes] + _per_seq(f, gt_ref[:, :, lanes], lambda v, gt: gt * v)


def _proj_residual(a, w, x, gt, t, tm_cap, tiles, name):
    m, k = a.shape
    d = w.shape[1]
    emit = w.dtype == _F32
    tm = _row_tile(m, t, tm_cap)
    tn = _TN
    tiles = 1 if emit else min(tiles, d // tn)
    wd = tiles * tn
    assert d % wd == 0
    assert not emit or m == tm, "the bf16 copy is written by a single row tile"
    est = 2 * (_nbytes((tm, k), _BF16) + tiles * _nbytes((k, tn), w.dtype) + 2 * _nbytes((tm, wd), _F32)
               ) + 2 * _nbytes((tm, tn), _F32)
    out_specs = [pl.BlockSpec((tm, wd), lambda i, j: (i, j))]
    out_shape = [jax.ShapeDtypeStruct((m, d), _F32)]
    if emit:
        out_specs.append(pl.BlockSpec((k, tn), lambda i, j: (0, j)))
        out_shape.append(jax.ShapeDtypeStruct((k, d), _BF16))
        est += 2 * _nbytes((k, tn), _BF16)
    return pl.pallas_call(
        functools.partial(_proj_residual_kernel, tiles=tiles),
        grid=(m // tm, d // wd),
        in_specs=[pl.BlockSpec((tm, k), lambda i, j: (i, 0)),
                  *[pl.BlockSpec((k, tn), lambda i, j, c=c: (0, tiles * j + c)) for c in range(tiles)],
                  pl.BlockSpec((tm, wd), lambda i, j: (i, j)),
                  _mod_spec(tm, t, wd, lambda i, j: j)],
        out_specs=out_specs,
        out_shape=out_shape,
        compiler_params=_cparams(("arbitrary", "arbitrary"), est),
        name=name + "_emit" if emit else name,
    )(a, *([w] * tiles), x, gt)


def _up_kernel(*refs, tps, nf, tiles, emit):
    h_ref, wu_refs, wv_refs = refs[0], refs[1:1 + tiles], refs[1 + tiles:1 + 2 * tiles]
    wc_ref, bc_ref, cs_ref, g_ref, tail_ref = refs[1 + 2 * tiles:6 + 2 * tiles]
    wub_ref, wvb_ref = refs[6 + 2 * tiles:8 + 2 * tiles] if emit else (None, None)
    u_scr, v_scr, carry_ref = refs[-3:]
    i = pl.program_id(0)
    j = pl.program_id(1)
    tm = u_scr.shape[0]
    tn = u_scr.shape[1] // tiles
    n_full, rem = divmod(nf, tiles)

    def matmul(k):
        h = h_ref[...]
        for c in range(k):
            lanes = slice(c * tn, (c + 1) * tn)
            u_scr[:, lanes] = jnp.dot(h, _bf16_weight(wu_refs[c], wub_ref), preferred_element_type=_F32)
            v_scr[:, lanes] = jnp.dot(h, _bf16_weight(wv_refs[c], wvb_ref), preferred_element_type=_F32)

    def epilogue_tile(lanes):
        jp = j - 1
        u = u_scr[:, lanes]
        v = v_scr[:, lanes]
        cs = cs_ref[:, :, lanes]
        spt = cs.shape[0]
        rows = tm // spt
        c0 = cs[:, 0:1, :]
        c1 = cs[:, 1:2, :]
        if tps > 1:
            first = (i % tps) == 0
            prev = carry_ref[jp, :, lanes]
            c0 = jnp.where(first, c0, prev[_SUBLANES - 2:_SUBLANES - 1][None])
            c1 = jnp.where(first, c1, prev[_SUBLANES - 1:_SUBLANES][None])
            carry_ref[jp, :, lanes] = u[tm - _SUBLANES:tm]
        wc = wc_ref[:, lanes]
        bc = bc_ref[:, lanes]
        if spt == 1:
            r = lax.broadcasted_iota(jnp.int32, (tm, tn), 0)
            p1 = jnp.where(r == 0, c1[0], pltpu.roll(u, 1, 0))
            p2 = jnp.where(r == 0, c0[0], jnp.where(r == 1, c1[0], pltpu.roll(u, 2, 0)))
            uc = wc[0:1] * p2 + wc[1:2] * p1 + wc[2:3] * u + bc
            g_ref[:, lanes] = (jax.nn.gelu(uc) * v).astype(g_ref.dtype)
            tail_ref[0, 0, :, lanes] = u[tm - _SUBLANES:tm]
        else:
            u3 = u.reshape(spt, rows, tn)
            r = lax.broadcasted_iota(jnp.int32, (spt, rows, tn), 1)
            p1 = jnp.where(r == 0, c1, pltpu.roll(u3, 1, 1))
            p2 = jnp.where(r == 0, c0, jnp.where(r == 1, c1, pltpu.roll(u3, 2, 1)))
            uc = wc[0:1][None] * p2 + wc[1:2][None] * p1 + wc[2:3][None] * u3 + bc[None]
            g_ref[:, lanes] = (jax.nn.gelu(uc) * v.reshape(spt, rows, tn)).reshape(tm, tn).astype(g_ref.dtype)
            tail_ref[0, :, :, lanes] = u3[:, rows - _SUBLANES:rows, :]

    def epilogue(k):
        for c in range(k):
            epilogue_tile(slice(c * tn, (c + 1) * tn))

    @pl.when(j == 0)
    def _first():
        if tps > 1:
            @pl.when(i == 0)
            def _init():
                carry_ref[...] = jnp.zeros_like(carry_ref)
        matmul(tiles)

    @pl.when((j > 0) & (j < n_full))
    def _steady():
        epilogue(tiles)
        matmul(tiles)

    if rem:
        @pl.when(j == n_full)
        def _short():
            epilogue(tiles)
            matmul(rem)

    @pl.when(j == n_full + (1 if rem else 0))
    def _drain():
        epilogue(rem if rem else tiles)


def _up_conv_geglu(h, w_u, w_v, w_conv, b_conv, conv_state, t):
    m, d = h.shape
    f = w_conv.shape[1]
    (wu_arr, wu_tile0), (wv_arr, wv_tile0) = w_u, w_v
    emit = wu_arr.dtype == _F32
    tm = _row_tile(m, t, _TM)
    tn = _TN_FF
    assert f % tn == 0, (f, tn)
    assert not emit or m == tm, "the bf16 copies are written by a single row tile"
    nf = f // tn
    tiles = 1 if emit else min(_UP_TILES, nf)
    n_groups = -(-nf // tiles)
    w = tiles * tn
    spt, tps = _seq_block(tm, t)
    rows = tm // spt
    assert rows >= _SUBLANES and conv_state.shape[1] == CONV_W - 1
    cur = lambda j: jnp.minimum(j, n_groups - 1)
    prev = lambda j: jnp.maximum(j - 1, 0)
    tile = lambda j, c: jnp.minimum(tiles * cur(j) + c, nf - 1)
    est = 2 * (_nbytes((tm, d), _BF16) + 2 * tiles * _nbytes((d, tn), wu_arr.dtype) + _nbytes((tm, w), _BF16)
               + _nbytes((spt, _SUBLANES, w), _F32)) + _nbytes((n_groups, _SUBLANES, w), _F32) + (
        2 * _nbytes((tm, w), _F32) + 10 * _nbytes((tm, tn), _F32))
    out_specs = [pl.BlockSpec((tm, w), lambda i, j: (i, prev(j))),
                 pl.BlockSpec((1, spt, _SUBLANES, w), lambda i, j: (i, 0, 0, prev(j)))]
    out_shape = [jax.ShapeDtypeStruct((m, f), _BF16),
                 jax.ShapeDtypeStruct((m // tm, spt, _SUBLANES, f), _F32)]
    if emit:
        out_specs += [pl.BlockSpec((d, tn), lambda i, j: (0, cur(j))) for _ in range(2)]
        out_shape += [jax.ShapeDtypeStruct((d, f), _BF16) for _ in range(2)]
        est += 4 * _nbytes((d, tn), _BF16)
    w_specs = [pl.BlockSpec((d, tn), lambda i, j, c=c, t0=t0: (0, t0 + tile(j, c)))
               for t0 in (wu_tile0, wv_tile0) for c in range(tiles)]
    return pl.pallas_call(
        functools.partial(_up_kernel, tps=tps, nf=nf, tiles=tiles, emit=emit),
        grid=(m // tm, n_groups + 1),
        in_specs=[pl.BlockSpec((tm, d), lambda i, j: (i, 0)), *w_specs,
                  pl.BlockSpec((CONV_W, w), lambda i, j: (0, prev(j))),
                  pl.BlockSpec((1, w), lambda i, j: (0, prev(j))),
                  pl.BlockSpec((spt, CONV_W - 1, w), lambda i, j: (i // tps, 0, prev(j)))],
        out_specs=out_specs,
        out_shape=out_shape,
        scratch_shapes=[pltpu.VMEM((tm, w), _F32), pltpu.VMEM((tm, w), _F32),
                        pltpu.VMEM((n_groups, _SUBLANES, w), _F32)],
        compiler_params=_cparams(("arbitrary", "arbitrary"), est),
        name="up_conv_geglu_emit" if emit else "up_conv_geglu",
    )(h, *([wu_arr] * tiles), *([wv_arr] * tiles), w_conv, b_conv.reshape(1, f), conv_state)


def _layer(x, mod, pos0, ret_state, swa_kc, swa_vc, conv_state, w):
    b, t, d = x.shape
    m = b * t
    x2 = x.reshape(m, d)
    sh1, sc1, gt1, sh2, sc2, gt2 = [v.reshape(b, 1, d) for v in jnp.split(mod, 6, axis=-1)]
    bf = {}

    h = _norm_mod(x2, w["g_mix"], sc1, sh1, t)
    bq, fo, *w_bf = _in_proj(h, w["w_in"], t, pos0)
    bf.update(zip(("w_in",), w_bf))
    a_ret, s_new = _retention(bq, fo, ret_state, b, t)
    if swa_kc is None:
        o_swa = _swa_prompt(bq, fo, w["swa_sinks"], b, t)
        keep = min(WINDOW, t)
    else:
        o_swa = _swa_cache(bq, fo, swa_kc, swa_vc, w["swa_sinks"], b, t)
        keep = t
    merged, *w_bf = _merge(a_ret, o_swa, fo, w["w_ret_o"], w["w_swa_o"])
    bf.update(zip(("w_ret_o", "w_swa_o"), w_bf))
    x1, *w_bf = _proj_residual(merged, w["w_out"], x2, gt1, t, _TM, 2, "out_proj")
    bf.update(zip(("w_out",), w_bf))

    h2 = _norm_mod(x1, w["g_ffn"], sc2, sh2, t)
    f = w["b_conv"].shape[0]
    cs = jnp.zeros((b, CONV_W - 1, f), _F32) if conv_state is None else conv_state
    if "w_up_u" in w:
        w_u, w_v = (w["w_up_u"], 0), (w["w_up_v"], 0)
    else:
        w_u, w_v = (w["w_up"], 0), (w["w_up"], f // _TN_FF)
    g, tails, *w_bf = _up_conv_geglu(h2, w_u, w_v, w["w_conv"], w["b_conv"], cs, t)
    bf.update(zip(("w_up_u", "w_up_v"), w_bf))
    x3, = _proj_residual(g, w["w_down"], x1, gt2, t, _TM_DOWN, 1, "down_proj")
    y = _final_norm(x3, w["g_final"])

    skv = fo.reshape(b, t, fo.shape[1])[:, t - keep:, fo.shape[1] - 2 * SWA_KV:]
    k_new = skv[:, :, :SWA_KV].reshape(b, keep, SWA_KV_HEADS, SWA_HD)
    v_new = skv[:, :, SWA_KV:].reshape(b, keep, SWA_KV_HEADS, SWA_HD)
    segs = tails.shape[0] * tails.shape[1] // b
    tails = tails.reshape(b, segs, _SUBLANES, f)
    conv_new = tails[:, segs - 1, _SUBLANES - (CONV_W - 1):, :]
    return y.reshape(b, t, d), s_new, k_new, v_new, conv_new, bf


def kernel(x_prompt, x_sample, cache_swa_k, cache_swa_v, state_ret, state_conv, c_prompt, c_sample,
           g_mix, g_ffn, w_ada, b_ada, w_in, swa_sinks, w_ret_o, w_swa_o, w_out,
           w_up, w_conv, b_conv, w_down, g_final):
    depth = g_mix.shape[0]
    assert depth == 1, "single-layer stack"
    bp = x_prompt.shape[0]
    bs = x_sample.shape[0]
    l = 0

    c_all = jnp.concatenate([c_prompt, c_sample], axis=0)
    rows = -(-c_all.shape[0] // 16) * 16
    c_all = jnp.pad(c_all, ((0, rows - c_all.shape[0]), (0, 0)))
    mod = _ada(c_all, w_ada[l], b_ada[l])

    w = {
        "g_mix": g_mix[l], "g_ffn": g_ffn[l], "g_final": g_final, "swa_sinks": swa_sinks[l],
        "w_in": w_in[l], "w_ret_o": w_ret_o[l], "w_swa_o": w_swa_o[l], "w_out": w_out[l], "w_up": w_up[l],
        "w_down": w_down[l].astype(_BF16),
        "w_conv": w_conv[l], "b_conv": b_conv[l],
    }

    ys, ss, ks, vs, cs, w_bf = _layer(x_sample, mod[bp:bp + bs], PAST_LEN, state_ret[l], cache_swa_k[l],
                                      cache_swa_v[l], state_conv[l], w)
    yp, sp, kp, vp, cp, _ = _layer(x_prompt, mod[:bp], 0, None, None, None, None, {**w, **w_bf})
    stack = lambda a: a[None]
    return (yp, ys, stack(sp), stack(kp), stack(vp), stack(cp), stack(ss), stack(ks), stack(vs), stack(cs))
```

```python
import functools

import jax
import jax.numpy as jnp
import numpy as np
from jax import lax
from jax.experimental import pallas as pl
from jax.experimental.pallas import tpu as pltpu

_F32 = jnp.float32
_BF16 = jnp.bfloat16

CHUNK = 64
RET_HEADS = 8
RET_DK = 256
RET_DV = 256
RET_QK = RET_HEADS * RET_DK
RET_V = RET_HEADS * RET_DV
ROPE_BASE = 10000.0
SWA_HEADS = 32
SWA_KV_HEADS = 4
SWA_HD = 64
SWA_GROUP = SWA_HEADS // SWA_KV_HEADS
SWA_PAIRS = SWA_GROUP // 2
SWA_Q = SWA_HEADS * SWA_HD
SWA_KV = SWA_KV_HEADS * SWA_HD
WINDOW = 128
WINDOW_CHUNKS = WINDOW // CHUNK
CONV_W = 3
EPS = 1e-6
NEG_INF = -1e30
PAST_LEN = 2048

_V7X_VMEM_BYTES = 64 * 1024 * 1024
_LANES = 128
_SUBLANES = 8

_TM = 1024
_TM_DOWN = 512
_TM_NORM = 512
_TN = 2 * SWA_KV
_TN_EMIT = 512
_TN_ADA = 512
_RET_CHUNK = 256
_SWA_QB = 2
_TN_FF = 256
_UP_TILES = 2


def _cparams(sem, vmem_estimate, flags=None):
    limit = min(int(vmem_estimate * 1.15) + (4 << 20), _V7X_VMEM_BYTES - (6 << 20))
    return pltpu.CompilerParams(dimension_semantics=sem, vmem_limit_bytes=limit, flags=flags)


def _nbytes(shape, dtype):
    return int(np.prod(shape)) * jnp.dtype(dtype).itemsize


def _row_tile(m, t, cap):
    tm = min(cap, m)
    assert m % tm == 0 and (t % tm == 0 or tm % t == 0), (m, t, tm)
    return tm


def _seq_block(tm, t):
    return max(1, tm // t), max(1, t // tm)


def _mod_spec(tm, t, tn, col):
    spt, tps = _seq_block(tm, t)
    return pl.BlockSpec((spt, 1, tn), lambda i, j: (i // tps, 0, col(i, j)))


def _per_seq(val, mod, fn):
    spt = mod.shape[0]
    if spt == 1:
        return fn(val, mod[0])
    tm, n = val.shape
    return fn(val.reshape(spt, tm // spt, n), mod).reshape(tm, n)


def _ada_kernel(c_ref, w_ref, b_ref, o_ref):
    c = c_ref[...]
    a = (c * jax.nn.sigmoid(c)).astype(_BF16)
    o_ref[...] = jnp.dot(a, w_ref[...].astype(_BF16), preferred_element_type=_F32) + b_ref[...]


def _ada(c, w_ada, b_ada):
    rows, d = c.shape
    n = w_ada.shape[1]
    tn = _TN_ADA
    est = 2 * (_nbytes((rows, d), _F32) + _nbytes((d, tn), _F32) + _nbytes((rows, tn), _F32)) + _nbytes((d, tn), _BF16)
    return pl.pallas_call(
        _ada_kernel,
        grid=(n // tn,),
        in_specs=[pl.BlockSpec((rows, d), lambda j: (0, 0)),
                  pl.BlockSpec((d, tn), lambda j: (0, j)),
                  pl.BlockSpec((1, tn), lambda j: (0, j))],
        out_specs=pl.BlockSpec((rows, tn), lambda j: (0, j)),
        out_shape=jax.ShapeDtypeStruct((rows, n), _F32),
        compiler_params=_cparams(("arbitrary",), est),
        name="ada_mod",
    )(c, w_ada, b_ada.reshape(1, n))


def _rms(x, g):
    y = x * lax.rsqrt(jnp.mean(x * x, axis=-1, keepdims=True) + EPS)
    return y * g


def _norm_mod_kernel(x_ref, g_ref, sc_ref, sh_ref, o_ref):
    y = _rms(x_ref[...], g_ref[...])
    h = _per_seq(y, sc_ref[...], lambda v, sc: v * (1.0 + sc))
    h = _per_seq(h, sh_ref[...], lambda v, sh: v + sh)
    o_ref[...] = h.astype(o_ref.dtype)


def _norm_mod(x, g, sc, sh, t):
    m, d = x.shape
    tm = _row_tile(m, t, _TM_NORM)
    est = 2 * (_nbytes((tm, d), _F32) + _nbytes((tm, d), _BF16)) + 3 * _nbytes((tm, d), _F32)
    return pl.pallas_call(
        _norm_mod_kernel,
        grid=(m // tm, 1),
        in_specs=[pl.BlockSpec((tm, d), lambda i, j: (i, 0)),
                  pl.BlockSpec((1, d), lambda i, j: (0, 0)),
                  _mod_spec(tm, t, d, lambda i, j: 0),
                  _mod_spec(tm, t, d, lambda i, j: 0)],
        out_specs=pl.BlockSpec((tm, d), lambda i, j: (i, 0)),
        out_shape=jax.ShapeDtypeStruct((m, d), _BF16),
        compiler_params=_cparams(("arbitrary", "arbitrary"), est),
        name="norm_mod",
    )(x, g.reshape(1, d), sc, sh)


def _final_norm_kernel(x_ref, g_ref, o_ref):
    o_ref[...] = _rms(x_ref[...], g_ref[...])


def _final_norm(x, g):
    m, d = x.shape
    tm = min(_TM_NORM, m)
    est = 4 * _nbytes((tm, d), _F32) + 2 * _nbytes((tm, d), _F32)
    return pl.pallas_call(
        _final_norm_kernel,
        grid=(m // tm,),
        in_specs=[pl.BlockSpec((tm, d), lambda i: (i, 0)),
                  pl.BlockSpec((1, d), lambda i: (0, 0))],
        out_specs=pl.BlockSpec((tm, d), lambda i: (i, 0)),
        out_shape=jax.ShapeDtypeStruct((m, d), _F32),
        compiler_params=_cparams(("arbitrary",), est),
        name="final_norm",
    )(x, g.reshape(1, d))


def _pair_slabs(slab, parity, fill):
    low = lax.broadcasted_iota(jnp.int32, slab.shape, 1) < SWA_HD
    if parity == 0:
        even = jnp.where(low, slab, 0.0)
        odd = pltpu.roll(even, SWA_HD, 1)
    else:
        odd = jnp.where(low, 0.0, slab)
        even = pltpu.roll(odd, SWA_HD, 1)
    if fill == 0.0:
        return even, odd
    return jnp.where(low, even, fill), jnp.where(low, fill, odd)


class _InPlan:
    def __init__(self, tn, pair, d):
        w = tn * pair
        n_skv = 2 * SWA_KV // tn
        assert RET_QK % w == 0 and SWA_Q % w == 0 and (2 * d) % w == 0 and (RET_V + 2 * d) % w == 0
        assert tn % RET_DK == 0 and (2 * SWA_KV) % tn == 0 and (pair == 1 or n_skv == 1)
        self.tn, self.pair, self.w = tn, pair, w
        r = self.r = RET_QK // w
        self.t_skv = 5 * r
        self.t_gab = self.t_skv + n_skv
        self.steps = self.t_gab + 2 * d // w
        self.skv_block = (RET_V + 2 * d) // w
        self.fo_cols = RET_V + 2 * d + 2 * SWA_KV

    def w_tile(self, t):
        p, ts, tg = self.pair, self.t_skv, self.t_gab
        return jnp.where(t < ts, p * t, jnp.where(t < tg, p * ts + (t - ts), p * ts + (tg - ts) + p * (t - tg)))

    def bq_block(self, t):
        r = self.r
        return jnp.where(t < 3 * r, t, jnp.where(t < 4 * r, 3 * r - 1, jnp.where(t < 5 * r, t - r, 4 * r - 1)))

    def fo_block(self, t):
        r, ts, tg = self.r, self.t_skv, self.t_gab
        return jnp.where(t < 3 * r, 0, jnp.where(t < 4 * r, t - 3 * r, jnp.where(
            t < ts, r - 1, jnp.where(t < tg, self.skv_block + (t - ts), r + (t - tg)))))


def _in_proj_kernel(*refs, plan, emit):
    n_w = plan.pair
    h_ref, w_refs, (cos_ref, sin_ref, bq_ref, fo_ref) = refs[0], refs[1:1 + n_w], refs[1 + n_w:5 + n_w]
    if emit:
        wbf_ref, = refs[5 + n_w:]
        wbf_ref[...] = w_refs[0][...].astype(_BF16)
        w_refs = (wbf_ref,)
    t = pl.program_id(1)
    r, tn = plan.r, plan.tn
    half = RET_DK // 2

    def tiles():
        for n, w_ref in enumerate(w_refs):
            yield jnp.dot(h_ref[...], w_ref[...], preferred_element_type=_F32), n * tn

    @pl.when(t < 2 * r)
    def _rope():
        c = cos_ref[...]
        s = sin_ref[...]
        scale = jnp.where(t >= r, RET_DK ** -0.5, 1.0).astype(_F32)
        for acc, col in tiles():
            for hh in range(tn // RET_DK):
                lo = hh * RET_DK
                x1 = acc[:, lo:lo + half]
                x2 = acc[:, lo + half:lo + RET_DK]
                bq_ref[:, col + lo:col + lo + half] = ((x1 * c - x2 * s) * scale).astype(_BF16)
                bq_ref[:, col + lo + half:col + lo + RET_DK] = ((x2 * c + x1 * s) * scale).astype(_BF16)

    @pl.when(((t >= 2 * r) & (t < 3 * r)) | ((t >= 4 * r) & (t < 5 * r)))
    def _bf16():
        for acc, col in tiles():
            bq_ref[:, col:col + tn] = acc.astype(_BF16)

    @pl.when(((t >= 3 * r) & (t < 4 * r)) | (t >= plan.t_gab))
    def _f32():
        for acc, col in tiles():
            fo_ref[:, col:col + tn] = acc

    @pl.when((t >= plan.t_skv) & (t < plan.t_gab))
    def _kv():
        fo_ref[:, 0:tn] = jnp.dot(h_ref[...], w_refs[0][...], preferred_element_type=_F32)


def _rope_tables(t, pos0, tm):
    half = RET_DK // 2
    inv = ROPE_BASE ** (-jnp.arange(half, dtype=_F32) * (2.0 / RET_DK))
    pos = (pos0 + jnp.arange(t, dtype=jnp.int32)).astype(_F32)
    ang = pos[:, None] * inv[None, :]
    reps = max(1, tm // t)
    return jnp.tile(jnp.cos(ang), (reps, 1)), jnp.tile(jnp.sin(ang), (reps, 1))


def _in_proj(h, w_in, t, pos0):
    m, d = h.shape
    n_in = w_in.shape[1]
    emit = w_in.dtype == _F32
    tm = _row_tile(m, t, _TM)
    plan = _InPlan(_TN_EMIT, 1, d) if emit else _InPlan(_TN, 2, d)
    tn, w = plan.tn, plan.w
    assert n_in == plan.fo_cols + 4 * RET_QK, (n_in, d)
    assert not emit or m == tm, "the bf16 copy is written by a single row tile"
    cos, sin = _rope_tables(t, pos0, tm)
    tab_blocks = cos.shape[0] // tm
    half = RET_DK // 2
    last_tile = n_in // tn - 1

    out_shape = [jax.ShapeDtypeStruct((m, 4 * RET_QK), _BF16), jax.ShapeDtypeStruct((m, plan.fo_cols), _F32)]
    out_specs = [pl.BlockSpec((tm, w), lambda i, j: (i, plan.bq_block(j))),
                 pl.BlockSpec((tm, w), lambda i, j: (i, plan.fo_block(j)))]
    w_specs = [pl.BlockSpec((d, tn), lambda i, j, n=n: (0, jnp.minimum(plan.w_tile(j) + n, last_tile)))
               for n in range(plan.pair)]
    est = 2 * (_nbytes((tm, d), _BF16) + plan.pair * _nbytes((d, tn), w_in.dtype) + 2 * _nbytes((tm, half), _F32)
               + _nbytes((tm, w), _BF16) + _nbytes((tm, w), _F32)) + 2 * plan.pair * _nbytes((tm, tn), _F32)
    if emit:
        out_shape.append(jax.ShapeDtypeStruct((d, n_in), _BF16))
        out_specs.append(pl.BlockSpec((d, tn), lambda i, j: (0, plan.w_tile(j))))
        est += 2 * _nbytes((d, tn), _BF16)
    return pl.pallas_call(
        functools.partial(_in_proj_kernel, plan=plan, emit=emit),
        grid=(m // tm, plan.steps),
        in_specs=[pl.BlockSpec((tm, d), lambda i, j: (i, 0)), *w_specs,
                  pl.BlockSpec((tm, half), lambda i, j: (i % tab_blocks, 0)),
                  pl.BlockSpec((tm, half), lambda i, j: (i % tab_blocks, 0))],
        out_specs=out_specs,
        out_shape=out_shape,
        compiler_params=_cparams(("arbitrary", "arbitrary"), est),
        name="in_proj_emit" if emit else "in_proj",
    )(h, *([w_in] * plan.pair), cos, sin)


def _retention_kernel(*refs, hb, cc, n_chunks, has_state):
    if has_state:
        lg_ref, q_ref, k_ref, v_ref, rg_ref, s0_ref, a_ref, sout_ref, s_scr = refs
    else:
        lg_ref, q_ref, k_ref, v_ref, rg_ref, a_ref, sout_ref, s_scr = refs
    row = lax.broadcasted_iota(jnp.int32, (cc, cc), 0)
    col = lax.broadcasted_iota(jnp.int32, (cc, cc), 1)
    diff = (row - col).astype(_F32)
    idx = lax.broadcasted_iota(jnp.int32, (cc, 1), 0).astype(_F32)
    for hh in range(hb):
        lg = lg_ref[hh][:, :1]
        dmask = jnp.where(diff >= 0, jnp.exp(lg * jnp.maximum(diff, 0.0)), 0.0)
        dq = jnp.exp(lg * (idx + 1.0))
        dk = jnp.exp(lg * (cc - 1.0 - idx))
        ds = jnp.exp(lg * float(cc))
        cols = slice(hh * RET_DK, (hh + 1) * RET_DK)
        if has_state:
            s_scr[...] = s0_ref[0, hh]
        else:
            s_scr[...] = jnp.zeros_like(s_scr)

        def body(c, carry):
            r = pl.ds(pl.multiple_of(c * cc, cc), cc)
            q = q_ref[r, cols]
            k = k_ref[r, cols]
            v = v_ref[r, cols]
            state = s_scr[...]
            scores = lax.dot_general(q, k, (((1,), (1,)), ((), ())), preferred_element_type=_F32) * dmask
            y = jnp.dot(scores.astype(_BF16), v, preferred_element_type=_F32)
            y = y + jnp.dot(q, state.astype(_BF16), preferred_element_type=_F32) * dq
            k_dec = (k.astype(_F32) * dk).astype(_BF16)
            s_scr[...] = state * ds + lax.dot_general(
                k_dec, v, (((0,), (0,)), ((), ())), preferred_element_type=_F32)
            mu = jnp.mean(y, axis=-1, keepdims=True)
            dev = y - mu
            var = jnp.mean(dev * dev, axis=-1, keepdims=True)
            yn = dev * lax.rsqrt(var + EPS)
            g = rg_ref[r, cols]
            a_ref[r, cols] = (g * jax.nn.sigmoid(g) * yn).astype(_BF16)
            return carry

        lax.fori_loop(0, n_chunks, body, 0, unroll=2 if n_chunks % 2 == 0 else 1)
        sout_ref[0, hh] = s_scr[...]


def _retention(qkv, rg, state, b, t):
    m = qkv.shape[0]
    cc = min(_RET_CHUNK, t)
    assert t % cc == 0 and cc % min(CHUNK, t) == 0
    hb = RET_HEADS if t * RET_QK * 2 <= (1 << 20) else 1
    nhb = RET_HEADS // hb
    wcols = hb * RET_DK
    h = jnp.arange(RET_HEADS, dtype=_F32)
    lg = jnp.log1p(-jnp.exp2(-5.0 - h))
    lg = jnp.broadcast_to(lg[:, None, None], (RET_HEADS, 1, _LANES))
    has_state = state is not None
    in_specs = [pl.BlockSpec((hb, 1, _LANES), lambda bi, hi: (hi, 0, 0)),
                pl.BlockSpec((t, wcols), lambda bi, hi: (bi, hi)),
                pl.BlockSpec((t, wcols), lambda bi, hi: (bi, nhb + hi)),
                pl.BlockSpec((t, wcols), lambda bi, hi: (bi, 2 * nhb + hi)),
                pl.BlockSpec((t, wcols), lambda bi, hi: (bi, hi))]
    args = [lg, qkv, qkv, qkv, rg]
    s_spec = pl.BlockSpec((1, hb, RET_DK, RET_DV), lambda bi, hi: (bi, hi, 0, 0))
    if has_state:
        in_specs.append(s_spec)
        args.append(state)
    est = 2 * (4 * _nbytes((t, wcols), _BF16) + _nbytes((t, wcols), _F32)
               + 2 * _nbytes((hb, RET_DK, RET_DV), _F32)) + 8 * _nbytes((cc, max(cc, RET_DV)), _F32)
    kern = functools.partial(_retention_kernel, hb=hb, cc=cc, n_chunks=t // cc, has_state=has_state)
    return pl.pallas_call(
        kern,
        grid=(b, nhb),
        in_specs=in_specs,
        out_specs=(pl.BlockSpec((t, wcols), lambda bi, hi: (bi, hi)), s_spec),
        out_shape=(jax.ShapeDtypeStruct((m, RET_V), _BF16),
                   jax.ShapeDtypeStruct((b, RET_HEADS, RET_DK, RET_DV), _F32)),
        scratch_shapes=[pltpu.VMEM((RET_DK, RET_DV), _F32)],
        compiler_params=_cparams(("arbitrary", "arbitrary"), est),
        name="retention",
    )(*args)


def _swa_core(qst, k_even, k_odd, v_even, v_odd, sink_even, sink_odd, valid):
    def weights(kx, sink):
        s = lax.dot_general(qst, kx, (((1,), (1,)), ((), ())), preferred_element_type=_F32) * (SWA_HD ** -0.5)
        if valid is not None:
            s = jnp.where(valid, s, NEG_INF)
        mx = jnp.maximum(jnp.max(s, axis=-1, keepdims=True), sink)
        return jnp.exp(s - mx).astype(_BF16), jnp.exp(sink - mx)

    e_even, t_even = weights(k_even, sink_even)
    e_odd, t_odd = weights(k_odd, sink_odd)
    o_even = jnp.dot(e_even, v_even, preferred_element_type=_F32)
    o_odd = jnp.dot(e_odd, v_odd, preferred_element_type=_F32)
    low = lax.broadcasted_iota(jnp.int32, o_even.shape, 1) < SWA_HD
    num = jnp.where(low, o_even, o_odd)
    den = jnp.where(low, pltpu.roll(o_even, SWA_HD, 1) + t_even, pltpu.roll(o_odd, SWA_HD, 1) + t_odd)
    return num / den


def _swa_prompt_kernel(q_ref, k_ref, v_ref, sink_ref, o_ref, slab_ref, *, n_blocks, unroll):
    gw = SWA_GROUP * SWA_HD
    rows = _SWA_QB * CHUNK
    win = _SWA_QB + WINDOW_CHUNKS

    for par in range(2):
        slabs = _pair_slabs(k_ref[...], par, fill=0.0) + _pair_slabs(v_ref[...], par, fill=1.0)
        for c, x in enumerate(slabs):
            slab_ref[par, c] = x.astype(_BF16)

    def valid_mask(n_key_chunks, first):
        shape = (SWA_PAIRS * rows, n_key_chunks * CHUNK)
        qc = (lax.broadcasted_iota(jnp.int32, shape, 0) // CHUNK) % _SWA_QB
        kc = lax.broadcasted_iota(jnp.int32, shape, 1) // CHUNK
        if first:
            return kc <= qc
        return (kc >= qc) & (kc <= qc + WINDOW_CHUNKS)

    valid_first = valid_mask(_SWA_QB, True)
    valid = valid_mask(win, False)
    for par in range(2):
        sink_even = sink_ref[par, 0]
        sink_odd = sink_ref[par, 1]
        lane0 = par * gw

        def block(r, kr, mask):
            qst = jnp.concatenate(
                [q_ref[r, lane0 + p * _LANES:lane0 + (p + 1) * _LANES] for p in range(SWA_PAIRS)], axis=0)
            o = _swa_core(qst, slab_ref[par, 0, kr, :], slab_ref[par, 1, kr, :],
                          slab_ref[par, 2, kr, :], slab_ref[par, 3, kr, :], sink_even, sink_odd, mask)
            for p in range(SWA_PAIRS):
                o_ref[r, lane0 + p * _LANES:lane0 + (p + 1) * _LANES] = (
                    o[p * rows:(p + 1) * rows].astype(o_ref.dtype))

        block(pl.ds(0, rows), pl.ds(0, rows), valid_first)
        if n_blocks > 1:
            def body(b, carry):
                r = pl.ds(pl.multiple_of(b * rows, rows), rows)
                kr = pl.ds(pl.multiple_of(b * rows - WINDOW_CHUNKS * CHUNK, CHUNK), win * CHUNK)
                block(r, kr, valid)
                return carry

            lax.fori_loop(1, n_blocks, body, 0, unroll=unroll)


def _sink_columns(sinks, rows):
    s = sinks.astype(_F32).reshape(SWA_KV_HEADS, SWA_PAIRS, 2).transpose(0, 2, 1)
    return jnp.repeat(s, rows, axis=-1)[..., None]


def _swa_prompt(bq, fo, sinks, b, t):
    m = bq.shape[0]
    gw2 = 2 * SWA_GROUP * SWA_HD
    rows = _SWA_QB * CHUNK
    assert t % rows == 0 and WINDOW_CHUNKS == _SWA_QB
    n_blocks = t // rows
    unroll = next(u for u in (3, 2, 1) if (n_blocks - 1) % u == 0)
    sink_cols = _sink_columns(sinks, rows)
    q0 = 3 * RET_QK // gw2
    k0 = (fo.shape[1] - 2 * SWA_KV) // _LANES
    v0 = k0 + SWA_KV // _LANES
    est = (2 * (2 * _nbytes((t, gw2), _BF16) + 2 * _nbytes((t, _LANES), _F32))
           + _nbytes((2, 4, t, _LANES), _BF16) + (8 << 20))
    return pl.pallas_call(
        functools.partial(_swa_prompt_kernel, n_blocks=n_blocks, unroll=unroll),
        grid=(b, SWA_KV_HEADS // 2),
        in_specs=[pl.BlockSpec((t, gw2), lambda bi, kp: (bi, q0 + kp)),
                  pl.BlockSpec((t, _LANES), lambda bi, kp: (bi, k0 + kp)),
                  pl.BlockSpec((t, _LANES), lambda bi, kp: (bi, v0 + kp)),
                  pl.BlockSpec((2, 2, SWA_PAIRS * rows, 1), lambda bi, kp: (kp, 0, 0, 0))],
        out_specs=pl.BlockSpec((t, gw2), lambda bi, kp: (bi, kp)),
        out_shape=jax.ShapeDtypeStruct((m, SWA_Q), _BF16),
        scratch_shapes=[pltpu.VMEM((2, 4, t, _LANES), _BF16)],
        compiler_params=_cparams(("arbitrary", "arbitrary"), est),
        name="swa_prompt",
    )(bq, fo, fo, sink_cols)


def _swa_cache_kernel(q_ref, kvn_ref, kc_ref, vc_ref, sink_ref, o_ref, *, L):
    gw = SWA_GROUP * SWA_HD
    kvn = kvn_ref[...]
    k_all = jnp.concatenate([kc_ref[0], kvn[:, :SWA_KV]], axis=0)
    v_all = jnp.concatenate([vc_ref[0], kvn[:, SWA_KV:]], axis=0)
    for kv in range(SWA_KV_HEADS):
        lanes = slice(_LANES * (kv // 2), _LANES * (kv // 2 + 1))
        slabs = _pair_slabs(k_all[:, lanes], kv % 2, fill=0.0) + _pair_slabs(v_all[:, lanes], kv % 2, fill=1.0)
        qst = jnp.concatenate(
            [q_ref[:, kv * gw + p * _LANES:kv * gw + (p + 1) * _LANES] for p in range(SWA_PAIRS)], axis=0)
        o = _swa_core(qst, *[x.astype(_BF16) for x in slabs], sink_ref[kv, 0], sink_ref[kv, 1], None)
        for p in range(SWA_PAIRS):
            lo = kv * gw + p * _LANES
            o_ref[:, lo:lo + _LANES] = o[p * L:(p + 1) * L].astype(o_ref.dtype)


def _swa_cache(bq, fo, k_cache, v_cache, sinks, b, t):
    m = bq.shape[0]
    rows = k_cache.shape[1]
    kc = k_cache.reshape(b, rows, SWA_KV)
    vc = v_cache.reshape(b, rows, SWA_KV)
    sink_cols = _sink_columns(sinks, t)
    assert (fo.shape[1] - 2 * SWA_KV) % (2 * SWA_KV) == 0
    kv0 = fo.shape[1] // (2 * SWA_KV) - 1
    est = 2 * (2 * _nbytes((t, SWA_Q), _BF16) + _nbytes((t, 2 * SWA_KV), _F32)
               + 2 * _nbytes((rows, SWA_KV), _F32)) + (4 << 20)
    return pl.pallas_call(
        functools.partial(_swa_cache_kernel, L=t),
        grid=(b,),
        in_specs=[pl.BlockSpec((t, SWA_Q), lambda bi: (bi, 3 * RET_QK // SWA_Q)),
                  pl.BlockSpec((t, 2 * SWA_KV), lambda bi: (bi, kv0)),
                  pl.BlockSpec((1, rows, SWA_KV), lambda bi: (bi, 0, 0)),
                  pl.BlockSpec((1, rows, SWA_KV), lambda bi: (bi, 0, 0)),
                  pl.BlockSpec((SWA_KV_HEADS, 2, SWA_PAIRS * t, 1), lambda bi: (0, 0, 0, 0))],
        out_specs=pl.BlockSpec((t, SWA_Q), lambda bi: (bi, 0)),
        out_shape=jax.ShapeDtypeStruct((m, SWA_Q), _BF16),
        compiler_params=_cparams(("arbitrary",), est),
        name="swa_cache",
    )(bq, fo, kc, vc, sink_cols)


def _bf16_weight(w_ref, wbf_ref):
    if wbf_ref is None:
        return w_ref[...]
    wbf_ref[...] = w_ref[...].astype(_BF16)
    return wbf_ref[...]


def _merge_kernel(a_ref, o_ref, wr_ref, ws_ref, ga_ref, gb_ref, out_ref, wrb_ref=None, wsb_ref=None):
    yr = jnp.dot(a_ref[...], _bf16_weight(wr_ref, wrb_ref), preferred_element_type=_F32)
    ys = jnp.dot(o_ref[...], _bf16_weight(ws_ref, wsb_ref), preferred_element_type=_F32)
    out_ref[...] = (jax.nn.sigmoid(ga_ref[...]) * yr + jax.nn.sigmoid(gb_ref[...]) * ys).astype(out_ref.dtype)


def _merge(a_ret, o_swa, fo, w_ret_o, w_swa_o):
    m = a_ret.shape[0]
    d = w_ret_o.shape[1]
    emit = w_ret_o.dtype == _F32
    tm = min(_TM, m)
    tn = _TN
    nd = d // tn
    g0 = RET_V // tn
    assert not emit or m == tm, "the bf16 copies are written by a single row tile"
    est = 2 * (_nbytes((tm, RET_V), _BF16) + _nbytes((tm, SWA_Q), _BF16) + _nbytes((RET_V, tn), w_ret_o.dtype)
               + _nbytes((SWA_Q, tn), w_swa_o.dtype) + 2 * _nbytes((tm, tn), _F32) + _nbytes((tm, tn), _BF16)
               ) + 4 * _nbytes((tm, tn), _F32)
    out_specs = [pl.BlockSpec((tm, tn), lambda i, j: (i, j))]
    out_shape = [jax.ShapeDtypeStruct((m, d), _BF16)]
    if emit:
        out_specs += [pl.BlockSpec((RET_V, tn), lambda i, j: (0, j)), pl.BlockSpec((SWA_Q, tn), lambda i, j: (0, j))]
        out_shape += [jax.ShapeDtypeStruct((RET_V, d), _BF16), jax.ShapeDtypeStruct((SWA_Q, d), _BF16)]
        est += 2 * (_nbytes((RET_V, tn), _BF16) + _nbytes((SWA_Q, tn), _BF16))
    return pl.pallas_call(
        _merge_kernel,
        grid=(m // tm, nd),
        in_specs=[pl.BlockSpec((tm, RET_V), lambda i, j: (i, 0)),
                  pl.BlockSpec((tm, SWA_Q), lambda i, j: (i, 0)),
                  pl.BlockSpec((RET_V, tn), lambda i, j: (0, j)),
                  pl.BlockSpec((SWA_Q, tn), lambda i, j: (0, j)),
                  pl.BlockSpec((tm, tn), lambda i, j: (i, g0 + j)),
                  pl.BlockSpec((tm, tn), lambda i, j: (i, g0 + nd + j))],
        out_specs=out_specs,
        out_shape=out_shape,
        compiler_params=_cparams(("arbitrary", "arbitrary"), est),
        name="merge_emit" if emit else "merge",
    )(a_ret, o_swa, w_ret_o, w_swa_o, fo, fo)


def _proj_residual_kernel(*refs, tiles):
    a_ref, w_refs, (x_ref, gt_ref, o_ref) = refs[0], refs[1:1 + tiles], refs[1 + tiles:4 + tiles]
    wbf_ref = refs[4 + tiles] if len(refs) > 4 + tiles else None
    tn = w_refs[0].shape[1]
    for c, w_ref in enumerate(w_refs):
        lanes = slice(c * tn, (c + 1) * tn)
        f = jnp.dot(a_ref[...], _bf16_weight(w_ref, wbf_ref), preferred_element_type=_F32)
        o_ref[:, lanes] = x_ref[:, lanes] + _per_seq(f, gt_ref[:, :, lanes], lambda v, gt: gt * v)


def _proj_residual(a, w, x, gt, t, tm_cap, tiles, name):
    m, k = a.shape
    d = w.shape[1]
    emit = w.dtype == _F32
    tm = _row_tile(m, t, tm_cap)
    tn = _TN
    tiles = 1 if emit else min(tiles, d // tn)
    wd = tiles * tn
    assert d % wd == 0
    assert not emit or m == tm, "the bf16 copy is written by a single row tile"
    est = 2 * (_nbytes((tm, k), _BF16) + tiles * _nbytes((k, tn), w.dtype) + 2 * _nbytes((tm, wd), _F32)
               ) + 2 * _nbytes((tm, tn), _F32)
    out_specs = [pl.BlockSpec((tm, wd), lambda i, j: (i, j))]
    out_shape = [jax.ShapeDtypeStruct((m, d), _F32)]
    if emit:
        out_specs.append(pl.BlockSpec((k, tn), lambda i, j: (0, j)))
        out_shape.append(jax.ShapeDtypeStruct((k, d), _BF16))
        est += 2 * _nbytes((k, tn), _BF16)
    return pl.pallas_call(
        functools.partial(_proj_residual_kernel, tiles=tiles),
        grid=(m // tm, d // wd),
        in_specs=[pl.BlockSpec((tm, k), lambda i, j: (i, 0)),
                  *[pl.BlockSpec((k, tn), lambda i, j, c=c: (0, tiles * j + c)) for c in range(tiles)],
                  pl.BlockSpec((tm, wd), lambda i, j: (i, j)),
                  _mod_spec(tm, t, wd, lambda i, j: j)],
        out_specs=out_specs,
        out_shape=out_shape,
        compiler_params=_cparams(("arbitrary", "arbitrary"), est),
        name=name + "_emit" if emit else name,
    )(a, *([w] * tiles), x, gt)


def _up_kernel(*refs, tps, nf, tiles, emit):
    h_ref, wu_refs, wv_refs = refs[0], refs[1:1 + tiles], refs[1 + tiles:1 + 2 * tiles]
    wc_ref, bc_ref, cs_ref, g_ref, tail_ref = refs[1 + 2 * tiles:6 + 2 * tiles]
    wub_ref, wvb_ref = refs[6 + 2 * tiles:8 + 2 * tiles] if emit else (None, None)
    u_scr, v_scr, carry_ref = refs[-3:]
    i = pl.program_id(0)
    j = pl.program_id(1)
    tm = u_scr.shape[0]
    tn = u_scr.shape[1] // tiles
    n_full, rem = divmod(nf, tiles)

    def ordered_after(x, token):
        bits = pltpu.bitcast(token, jnp.uint32)
        zero = pltpu.bitcast((bits >> 16) >> 16, _F32)
        return (x.reshape(tm // _SUBLANES, _SUBLANES, tn) + zero[None]).reshape(tm, tn)

    def matmul(k, token=None):
        h = h_ref[...]
        for c in range(k):
            lanes = slice(c * tn, (c + 1) * tn)
            u = jnp.dot(h, _bf16_weight(wu_refs[c], wub_ref), preferred_element_type=_F32)
            u_scr[:, lanes] = u if (token is None or c > 0) else ordered_after(u, token)
            v_scr[:, lanes] = jnp.dot(h, _bf16_weight(wv_refs[c], wvb_ref), preferred_element_type=_F32)

    def epilogue_tile(lanes):
        jp = j - 1
        u = u_scr[:, lanes]
        v = v_scr[:, lanes]
        cs = cs_ref[:, :, lanes]
        spt = cs.shape[0]
        rows = tm // spt
        c0 = cs[:, 0:1, :]
        c1 = cs[:, 1:2, :]
        if tps > 1:
            first = (i % tps) == 0
            prev = carry_ref[jp, :, lanes]
            c0 = jnp.where(first, c0, prev[_SUBLANES - 2:_SUBLANES - 1][None])
            c1 = jnp.where(first, c1, prev[_SUBLANES - 1:_SUBLANES][None])
            carry_ref[jp, :, lanes] = u[tm - _SUBLANES:tm]
        wc = wc_ref[:, lanes]
        bc = bc_ref[:, lanes]
        if spt == 1:
            r = lax.broadcasted_iota(jnp.int32, (tm, tn), 0)
            p1 = jnp.where(r == 0, c1[0], pltpu.roll(u, 1, 0))
            p2 = jnp.where(r == 0, c0[0], jnp.where(r == 1, c1[0], pltpu.roll(u, 2, 0)))
            uc = wc[0:1] * p2 + wc[1:2] * p1 + wc[2:3] * u + bc
            g = jax.nn.gelu(uc) * v
            tail_ref[0, 0, :, lanes] = u[tm - _SUBLANES:tm]
        else:
            u3 = u.reshape(spt, rows, tn)
            r = lax.broadcasted_iota(jnp.int32, (spt, rows, tn), 1)
            p1 = jnp.where(r == 0, c1, pltpu.roll(u3, 1, 1))
            p2 = jnp.where(r == 0, c0, jnp.where(r == 1, c1, pltpu.roll(u3, 2, 1)))
            uc = wc[0:1][None] * p2 + wc[1:2][None] * p1 + wc[2:3][None] * u3 + bc[None]
            g = (jax.nn.gelu(uc) * v.reshape(spt, rows, tn)).reshape(tm, tn)
            tail_ref[0, :, :, lanes] = u3[:, rows - _SUBLANES:rows, :]
        g_ref[:, lanes] = g.astype(g_ref.dtype)
        return jnp.max(g.reshape(tm // _SUBLANES, _SUBLANES, tn), axis=0)

    def epilogue(k):
        token = None
        for c in range(k):
            part = epilogue_tile(slice(c * tn, (c + 1) * tn))
            token = part if token is None else jnp.maximum(token, part)
        return token

    @pl.when(j == 0)
    def _first():
        if tps > 1:
            @pl.when(i == 0)
            def _init():
                carry_ref[...] = jnp.zeros_like(carry_ref)
        matmul(tiles)

    @pl.when((j > 0) & (j < n_full))
    def _steady():
        matmul(tiles, epilogue(tiles))

    if rem:
        @pl.when(j == n_full)
        def _short():
            matmul(rem, epilogue(tiles))

    @pl.when(j == n_full + (1 if rem else 0))
    def _drain():
        epilogue(rem if rem else tiles)


def _up_conv_geglu(h, w_u, w_v, w_conv, b_conv, conv_state, t):
    m, d = h.shape
    f = w_conv.shape[1]
    (wu_arr, wu_tile0), (wv_arr, wv_tile0) = w_u, w_v
    emit = wu_arr.dtype == _F32
    tm = _row_tile(m, t, _TM)
    tn = _TN_FF
    assert f % tn == 0, (f, tn)
    assert not emit or m == tm, "the bf16 copies are written by a single row tile"
    nf = f // tn
    tiles = 1 if emit else min(_UP_TILES, nf)
    n_groups = -(-nf // tiles)
    w = tiles * tn
    spt, tps = _seq_block(tm, t)
    rows = tm // spt
    assert rows >= _SUBLANES and conv_state.shape[1] == CONV_W - 1
    cur = lambda j: jnp.minimum(j, n_groups - 1)
    prev = lambda j: jnp.maximum(j - 1, 0)
    tile = lambda j, c: jnp.minimum(tiles * cur(j) + c, nf - 1)
    est = 2 * (_nbytes((tm, d), _BF16) + 2 * tiles * _nbytes((d, tn), wu_arr.dtype) + _nbytes((tm, w), _BF16)
               + _nbytes((spt, _SUBLANES, w), _F32)) + _nbytes((n_groups, _SUBLANES, w), _F32) + (
        2 * _nbytes((tm, w), _F32) + 10 * _nbytes((tm, tn), _F32))
    out_specs = [pl.BlockSpec((tm, w), lambda i, j: (i, prev(j))),
                 pl.BlockSpec((1, spt, _SUBLANES, w), lambda i, j: (i, 0, 0, prev(j)))]
    out_shape = [jax.ShapeDtypeStruct((m, f), _BF16),
                 jax.ShapeDtypeStruct((m // tm, spt, _SUBLANES, f), _F32)]
    if emit:
        out_specs += [pl.BlockSpec((d, tn), lambda i, j: (0, cur(j))) for _ in range(2)]
        out_shape += [jax.ShapeDtypeStruct((d, f), _BF16) for _ in range(2)]
        est += 4 * _nbytes((d, tn), _BF16)
    w_specs = [pl.BlockSpec((d, tn), lambda i, j, c=c, t0=t0: (0, t0 + tile(j, c)))
               for t0 in (wu_tile0, wv_tile0) for c in range(tiles)]
    return pl.pallas_call(
        functools.partial(_up_kernel, tps=tps, nf=nf, tiles=tiles, emit=emit),
        grid=(m // tm, n_groups + 1),
        in_specs=[pl.BlockSpec((tm, d), lambda i, j: (i, 0)), *w_specs,
                  pl.BlockSpec((CONV_W, w), lambda i, j: (0, prev(j))),
                  pl.BlockSpec((1, w), lambda i, j: (0, prev(j))),
                  pl.BlockSpec((spt, CONV_W - 1, w), lambda i, j: (i // tps, 0, prev(j)))],
        out_specs=out_specs,
        out_shape=out_shape,
        scratch_shapes=[pltpu.VMEM((tm, w), _F32), pltpu.VMEM((tm, w), _F32),
                        pltpu.VMEM((n_groups, _SUBLANES, w), _F32)],
        compiler_params=_cparams(("arbitrary", "arbitrary"), est),
        name="up_conv_geglu_emit" if emit else "up_conv_geglu",
    )(h, *([wu_arr] * tiles), *([wv_arr] * tiles), w_conv, b_conv.reshape(1, f), conv_state)


def _layer(x, mod, pos0, ret_state, swa_kc, swa_vc, conv_state, w):
    b, t, d = x.shape
    m = b * t
    x2 = x.reshape(m, d)
    sh1, sc1, gt1, sh2, sc2, gt2 = [v.reshape(b, 1, d) for v in jnp.split(mod, 6, axis=-1)]
    bf = {}

    h = _norm_mod(x2, w["g_mix"], sc1, sh1, t)
    bq, fo, *w_bf = _in_proj(h, w["w_in"], t, pos0)
    bf.update(zip(("w_in",), w_bf))
    a_ret, s_new = _retention(bq, fo, ret_state, b, t)
    if swa_kc is None:
        o_swa = _swa_prompt(bq, fo, w["swa_sinks"], b, t)
        keep = min(WINDOW, t)
    else:
        o_swa = _swa_cache(bq, fo, swa_kc, swa_vc, w["swa_sinks"], b, t)
        keep = t
    merged, *w_bf = _merge(a_ret, o_swa, fo, w["w_ret_o"], w["w_swa_o"])
    bf.update(zip(("w_ret_o", "w_swa_o"), w_bf))
    x1, *w_bf = _proj_residual(merged, w["w_out"], x2, gt1, t, _TM, 2, "out_proj")
    bf.update(zip(("w_out",), w_bf))

    h2 = _norm_mod(x1, w["g_ffn"], sc2, sh2, t)
    f = w["b_conv"].shape[0]
    cs = jnp.zeros((b, CONV_W - 1, f), _F32) if conv_state is None else conv_state
    if "w_up_u" in w:
        w_u, w_v = (w["w_up_u"], 0), (w["w_up_v"], 0)
    else:
        w_u, w_v = (w["w_up"], 0), (w["w_up"], f // _TN_FF)
    g, tails, *w_bf = _up_conv_geglu(h2, w_u, w_v, w["w_conv"], w["b_conv"], cs, t)
    bf.update(zip(("w_up_u", "w_up_v"), w_bf))
    x3, = _proj_residual(g, w["w_down"], x1, gt2, t, _TM_DOWN, 1, "down_proj")
    y = _final_norm(x3, w["g_final"])

    skv = fo.reshape(b, t, fo.shape[1])[:, t - keep:, fo.shape[1] - 2 * SWA_KV:]
    k_new = skv[:, :, :SWA_KV].reshape(b, keep, SWA_KV_HEADS, SWA_HD)
    v_new = skv[:, :, SWA_KV:].reshape(b, keep, SWA_KV_HEADS, SWA_HD)
    segs = tails.shape[0] * tails.shape[1] // b
    tails = tails.reshape(b, segs, _SUBLANES, f)
    conv_new = tails[:, segs - 1, _SUBLANES - (CONV_W - 1):, :]
    return y.reshape(b, t, d), s_new, k_new, v_new, conv_new, bf


def kernel(x_prompt, x_sample, cache_swa_k, cache_swa_v, state_ret, state_conv, c_prompt, c_sample,
           g_mix, g_ffn, w_ada, b_ada, w_in, swa_sinks, w_ret_o, w_swa_o, w_out,
           w_up, w_conv, b_conv, w_down, g_final):
    depth = g_mix.shape[0]
    assert depth == 1, "single-layer stack"
    bp = x_prompt.shape[0]
    bs = x_sample.shape[0]
    l = 0

    c_all = jnp.concatenate([c_prompt, c_sample], axis=0)
    rows = -(-c_all.shape[0] // 16) * 16
    c_all = jnp.pad(c_all, ((0, rows - c_all.shape[0]), (0, 0)))
    mod = _ada(c_all, w_ada[l], b_ada[l])

    w = {
        "g_mix": g_mix[l], "g_ffn": g_ffn[l], "g_final": g_final, "swa_sinks": swa_sinks[l],
        "w_in": w_in[l], "w_ret_o": w_ret_o[l], "w_swa_o": w_swa_o[l], "w_out": w_out[l], "w_up": w_up[l],
        "w_down": w_down[l].astype(_BF16),
        "w_conv": w_conv[l], "b_conv": b_conv[l],
    }

    ys, ss, ks, vs, cs, w_bf = _layer(x_sample, mod[bp:bp + bs], PAST_LEN, state_ret[l], cache_swa_k[l],
                                      cache_swa_v[l], state_conv[l], w)
    yp, sp, kp, vp, cp, _ = _layer(x_prompt, mod[:bp], 0, None, None, None, None, {**w, **w_bf})
    stack = lambda a: a[None]
    return (yp, ys, stack(sp), stack(kp), stack(vp), stack(cp), stack(ss), stack(ks), stack(vs), stack(cs))
```

```python
import functools

import jax
import jax.numpy as jnp
import numpy as np
from jax import lax
from jax.experimental import pallas as pl
from jax.experimental.pallas import tpu as pltpu

_F32 = jnp.float32
_BF16 = jnp.bfloat16

CHUNK = 64
RET_HEADS = 8
RET_DK = 256
RET_DV = 256
RET_QK = RET_HEADS * RET_DK
RET_V = RET_HEADS * RET_DV
ROPE_BASE = 10000.0
SWA_HEADS = 32
SWA_KV_HEADS = 4
SWA_HD = 64
SWA_GROUP = SWA_HEADS // SWA_KV_HEADS
SWA_PAIRS = SWA_GROUP // 2
SWA_Q = SWA_HEADS * SWA_HD
SWA_KV = SWA_KV_HEADS * SWA_HD
WINDOW = 128
WINDOW_CHUNKS = WINDOW // CHUNK
CONV_W = 3
EPS = 1e-6
NEG_INF = -1e30
PAST_LEN = 2048

_V7X_VMEM_BYTES = 64 * 1024 * 1024
_LANES = 128
_SUBLANES = 8

_TM = 1024
_TM_DOWN = 512
_TM_NORM = 512
_TN = 2 * SWA_KV
_TN_EMIT = 512
_TN_ADA = 512
_RET_CHUNK = 256
_SWA_QB = 2
_TN_FF = 256
_UP_TILES = 2


def _cparams(sem, vmem_estimate, flags=None):
    limit = min(int(vmem_estimate * 1.15) + (4 << 20), _V7X_VMEM_BYTES - (6 << 20))
    return pltpu.CompilerParams(dimension_semantics=sem, vmem_limit_bytes=limit, flags=flags)


def _nbytes(shape, dtype):
    return int(np.prod(shape)) * jnp.dtype(dtype).itemsize


def _row_tile(m, t, cap):
    tm = min(cap, m)
    assert m % tm == 0 and (t % tm == 0 or tm % t == 0), (m, t, tm)
    return tm


def _seq_block(tm, t):
    return max(1, tm // t), max(1, t // tm)


def _mod_spec(tm, t, tn, col):
    spt, tps = _seq_block(tm, t)
    return pl.BlockSpec((spt, 1, tn), lambda i, j: (i // tps, 0, col(i, j)))


def _per_seq(val, mod, fn):
    spt = mod.shape[0]
    if spt == 1:
        return fn(val, mod[0])
    tm, n = val.shape
    return fn(val.reshape(spt, tm // spt, n), mod).reshape(tm, n)


def _ada_kernel(c_ref, w_ref, b_ref, o_ref):
    c = c_ref[...]
    a = (c * jax.nn.sigmoid(c)).astype(_BF16)
    o_ref[...] = jnp.dot(a, w_ref[...].astype(_BF16), preferred_element_type=_F32) + b_ref[...]


def _ada(c, w_ada, b_ada):
    rows, d = c.shape
    n = w_ada.shape[1]
    tn = _TN_ADA
    est = 2 * (_nbytes((rows, d), _F32) + _nbytes((d, tn), _F32) + _nbytes((rows, tn), _F32)) + _nbytes((d, tn), _BF16)
    return pl.pallas_call(
        _ada_kernel,
        grid=(n // tn,),
        in_specs=[pl.BlockSpec((rows, d), lambda j: (0, 0)),
                  pl.BlockSpec((d, tn), lambda j: (0, j)),
                  pl.BlockSpec((1, tn), lambda j: (0, j))],
        out_specs=pl.BlockSpec((rows, tn), lambda j: (0, j)),
        out_shape=jax.ShapeDtypeStruct((rows, n), _F32),
        compiler_params=_cparams(("arbitrary",), est),
        name="ada_mod",
    )(c, w_ada, b_ada.reshape(1, n))


def _norm_rows(m):
    return _TM_NORM if m >= 8 * _TM_NORM else _TM_NORM // 2


def _rms(x, g):
    y = x * lax.rsqrt(jnp.mean(x * x, axis=-1, keepdims=True) + EPS)
    return y * g


def _norm_mod_kernel(x_ref, g_ref, sc_ref, sh_ref, o_ref):
    y = _rms(x_ref[...], g_ref[...])
    h = _per_seq(y, sc_ref[...], lambda v, sc: v * (1.0 + sc))
    h = _per_seq(h, sh_ref[...], lambda v, sh: v + sh)
    o_ref[...] = h.astype(o_ref.dtype)


def _norm_mod(x, g, sc, sh, t):
    m, d = x.shape
    tm = _row_tile(m, t, _norm_rows(m))
    est = 2 * (_nbytes((tm, d), _F32) + _nbytes((tm, d), _BF16)) + 3 * _nbytes((tm, d), _F32)
    return pl.pallas_call(
        _norm_mod_kernel,
        grid=(m // tm, 1),
        in_specs=[pl.BlockSpec((tm, d), lambda i, j: (i, 0)),
                  pl.BlockSpec((1, d), lambda i, j: (0, 0)),
                  _mod_spec(tm, t, d, lambda i, j: 0),
                  _mod_spec(tm, t, d, lambda i, j: 0)],
        out_specs=pl.BlockSpec((tm, d), lambda i, j: (i, 0)),
        out_shape=jax.ShapeDtypeStruct((m, d), _BF16),
        compiler_params=_cparams(("arbitrary", "arbitrary"), est),
        name="norm_mod",
    )(x, g.reshape(1, d), sc, sh)


def _final_norm_kernel(x_ref, g_ref, o_ref):
    o_ref[...] = _rms(x_ref[...], g_ref[...])


def _final_norm(x, g):
    m, d = x.shape
    tm = min(_norm_rows(m), m)
    est = 4 * _nbytes((tm, d), _F32) + 2 * _nbytes((tm, d), _F32)
    return pl.pallas_call(
        _final_norm_kernel,
        grid=(m // tm,),
        in_specs=[pl.BlockSpec((tm, d), lambda i: (i, 0)),
                  pl.BlockSpec((1, d), lambda i: (0, 0))],
        out_specs=pl.BlockSpec((tm, d), lambda i: (i, 0)),
        out_shape=jax.ShapeDtypeStruct((m, d), _F32),
        compiler_params=_cparams(("arbitrary",), est),
        name="final_norm",
    )(x, g.reshape(1, d))


def _pair_slabs(slab, parity, fill):
    low = lax.broadcasted_iota(jnp.int32, slab.shape, 1) < SWA_HD
    if parity == 0:
        even = jnp.where(low, slab, 0.0)
        odd = pltpu.roll(even, SWA_HD, 1)
    else:
        odd = jnp.where(low, 0.0, slab)
        even = pltpu.roll(odd, SWA_HD, 1)
    if fill == 0.0:
        return even, odd
    return jnp.where(low, even, fill), jnp.where(low, fill, odd)


class _InPlan:
    def __init__(self, tn, pair, d):
        w = tn * pair
        n_skv = 2 * SWA_KV // tn
        assert RET_QK % w == 0 and SWA_Q % w == 0 and (2 * d) % w == 0 and (RET_V + 2 * d) % w == 0
        assert tn % RET_DK == 0 and (2 * SWA_KV) % tn == 0 and (pair == 1 or n_skv == 1)
        self.tn, self.pair, self.w = tn, pair, w
        r = self.r = RET_QK // w
        self.t_skv = 5 * r
        self.t_gab = self.t_skv + n_skv
        self.steps = self.t_gab + 2 * d // w
        self.skv_block = (RET_V + 2 * d) // w
        self.fo_cols = RET_V + 2 * d + 2 * SWA_KV

    def w_tile(self, t):
        p, ts, tg = self.pair, self.t_skv, self.t_gab
        return jnp.where(t < ts, p * t, jnp.where(t < tg, p * ts + (t - ts), p * ts + (tg - ts) + p * (t - tg)))

    def bq_block(self, t):
        r = self.r
        return jnp.where(t < 3 * r, t, jnp.where(t < 4 * r, 3 * r - 1, jnp.where(t < 5 * r, t - r, 4 * r - 1)))

    def fo_block(self, t):
        r, ts, tg = self.r, self.t_skv, self.t_gab
        return jnp.where(t < 3 * r, 0, jnp.where(t < 4 * r, t - 3 * r, jnp.where(
            t < ts, r - 1, jnp.where(t < tg, self.skv_block + (t - ts), r + (t - tg)))))


def _in_proj_kernel(*refs, plan, emit):
    n_w = plan.pair
    h_ref, w_refs, (cos_ref, sin_ref, bq_ref, fo_ref) = refs[0], refs[1:1 + n_w], refs[1 + n_w:5 + n_w]
    if emit:
        wbf_ref, = refs[5 + n_w:]
        _bf16_weight(w_refs[0], wbf_ref)
        w_refs = (wbf_ref,)
    t = pl.program_id(1)
    r, tn = plan.r, plan.tn
    half = RET_DK // 2

    def tiles():
        for n, w_ref in enumerate(w_refs):
            yield jnp.dot(h_ref[...], _bf16_weight(w_ref, None), preferred_element_type=_F32), n * tn

    @pl.when(t < 2 * r)
    def _rope():
        c = cos_ref[...]
        s = sin_ref[...]
        scale = jnp.where(t >= r, RET_DK ** -0.5, 1.0).astype(_F32)
        for acc, col in tiles():
            for hh in range(tn // RET_DK):
                lo = hh * RET_DK
                x1 = acc[:, lo:lo + half]
                x2 = acc[:, lo + half:lo + RET_DK]
                bq_ref[:, col + lo:col + lo + half] = ((x1 * c - x2 * s) * scale).astype(_BF16)
                bq_ref[:, col + lo + half:col + lo + RET_DK] = ((x2 * c + x1 * s) * scale).astype(_BF16)

    @pl.when(((t >= 2 * r) & (t < 3 * r)) | ((t >= 4 * r) & (t < 5 * r)))
    def _bf16():
        for acc, col in tiles():
            bq_ref[:, col:col + tn] = acc.astype(_BF16)

    @pl.when(((t >= 3 * r) & (t < 4 * r)) | (t >= plan.t_gab))
    def _f32():
        for acc, col in tiles():
            fo_ref[:, col:col + tn] = acc

    @pl.when((t >= plan.t_skv) & (t < plan.t_gab))
    def _kv():
        fo_ref[:, 0:tn] = jnp.dot(h_ref[...], _bf16_weight(w_refs[0], None), preferred_element_type=_F32)


def _rope_tables(t, pos0, tm):
    half = RET_DK // 2
    inv = ROPE_BASE ** (-jnp.arange(half, dtype=_F32) * (2.0 / RET_DK))
    pos = (pos0 + jnp.arange(t, dtype=jnp.int32)).astype(_F32)
    ang = pos[:, None] * inv[None, :]
    reps = max(1, tm // t)
    return jnp.tile(jnp.cos(ang), (reps, 1)), jnp.tile(jnp.sin(ang), (reps, 1))


def _in_proj(h, w_in, t, pos0):
    m, d = h.shape
    n_in = _w_cols(w_in)
    emit = w_in.dtype == _F32
    tm = _row_tile(m, t, _TM)
    plan = _InPlan(_TN_EMIT, 1, d) if emit else _InPlan(_TN, 2, d)
    tn, w = plan.tn, plan.w
    assert n_in == plan.fo_cols + 4 * RET_QK, (n_in, d)
    assert not emit or m == tm, "the bf16 copy is written by a single row tile"
    cos, sin = _rope_tables(t, pos0, tm)
    tab_blocks = cos.shape[0] // tm
    half = RET_DK // 2
    last_tile = n_in // tn - 1

    out_shape = [jax.ShapeDtypeStruct((m, 4 * RET_QK), _BF16), jax.ShapeDtypeStruct((m, plan.fo_cols), _F32)]
    out_specs = [pl.BlockSpec((tm, w), lambda i, j: (i, plan.bq_block(j))),
                 pl.BlockSpec((tm, w), lambda i, j: (i, plan.fo_block(j)))]
    w_specs = [_w_spec(w_in, d, tn, lambda i, j, n=n: jnp.minimum(plan.w_tile(j) + n, last_tile))
               for n in range(plan.pair)]
    est = 2 * (_nbytes((tm, d), _BF16) + plan.pair * _nbytes((d, tn), w_in.dtype) + 2 * _nbytes((tm, half), _F32)
               + _nbytes((tm, w), _BF16) + _nbytes((tm, w), _F32)) + 2 * plan.pair * _nbytes((tm, tn), _F32)
    if emit:
        assert tn == _TN, "the copy's tiles are the column tiles of the bf16 call"
        copy_shape, copy_spec = _w_copy(d, n_in, tn, lambda i, j: plan.w_tile(j))
        out_shape.append(copy_shape)
        out_specs.append(copy_spec)
        est += 2 * _nbytes((d, tn), _BF16)
    return pl.pallas_call(
        functools.partial(_in_proj_kernel, plan=plan, emit=emit),
        grid=(m // tm, plan.steps),
        in_specs=[pl.BlockSpec((tm, d), lambda i, j: (i, 0)), *w_specs,
                  pl.BlockSpec((tm, half), lambda i, j: (i % tab_blocks, 0)),
                  pl.BlockSpec((tm, half), lambda i, j: (i % tab_blocks, 0))],
        out_specs=out_specs,
        out_shape=out_shape,
        compiler_params=_cparams(("arbitrary", "arbitrary"), est),
        name="in_proj_emit" if emit else "in_proj",
    )(h, *([w_in] * plan.pair), cos, sin)


def _retention_kernel(*refs, hb, cc, n_chunks, has_state):
    if has_state:
        lg_ref, q_ref, k_ref, v_ref, rg_ref, s0_ref, a_ref, sout_ref, s_scr = refs
    else:
        lg_ref, q_ref, k_ref, v_ref, rg_ref, a_ref, sout_ref, s_scr = refs
    row = lax.broadcasted_iota(jnp.int32, (cc, cc), 0)
    col = lax.broadcasted_iota(jnp.int32, (cc, cc), 1)
    diff = (row - col).astype(_F32)
    idx = lax.broadcasted_iota(jnp.int32, (cc, 1), 0).astype(_F32)
    for hh in range(hb):
        lg = lg_ref[hh][:, :1]
        dmask = jnp.where(diff >= 0, jnp.exp(lg * jnp.maximum(diff, 0.0)), 0.0)
        dq = jnp.exp(lg * (idx + 1.0))
        dk = jnp.exp(lg * (cc - 1.0 - idx))
        ds = jnp.exp(lg * float(cc))
        cols = slice(hh * RET_DK, (hh + 1) * RET_DK)
        if has_state:
            s_scr[...] = s0_ref[0, hh]
        else:
            s_scr[...] = jnp.zeros_like(s_scr)

        def body(c, carry):
            r = pl.ds(pl.multiple_of(c * cc, cc), cc)
            q = q_ref[r, cols]
            k = k_ref[r, cols]
            v = v_ref[r, cols]
            state = s_scr[...]
            scores = lax.dot_general(q, k, (((1,), (1,)), ((), ())), preferred_element_type=_F32) * dmask
            y = jnp.dot(scores.astype(_BF16), v, preferred_element_type=_F32)
            y = y + jnp.dot(q, state.astype(_BF16), preferred_element_type=_F32) * dq
            k_dec = (k.astype(_F32) * dk).astype(_BF16)
            s_scr[...] = state * ds + lax.dot_general(
                k_dec, v, (((0,), (0,)), ((), ())), preferred_element_type=_F32)
            mu = jnp.mean(y, axis=-1, keepdims=True)
            dev = y - mu
            var = jnp.mean(dev * dev, axis=-1, keepdims=True)
            yn = dev * lax.rsqrt(var + EPS)
            g = rg_ref[r, cols]
            a_ref[r, cols] = (g * jax.nn.sigmoid(g) * yn).astype(_BF16)
            return carry

        lax.fori_loop(0, n_chunks, body, 0, unroll=2 if n_chunks % 2 == 0 else 1)
        sout_ref[0, hh] = s_scr[...]


def _retention(qkv, rg, state, b, t):
    m = qkv.shape[0]
    cc = min(_RET_CHUNK, t)
    assert t % cc == 0 and cc % min(CHUNK, t) == 0
    hb = RET_HEADS if t * RET_QK * 2 <= (1 << 20) else 1
    nhb = RET_HEADS // hb
    wcols = hb * RET_DK
    h = jnp.arange(RET_HEADS, dtype=_F32)
    lg = jnp.log1p(-jnp.exp2(-5.0 - h))
    lg = jnp.broadcast_to(lg[:, None, None], (RET_HEADS, 1, _LANES))
    has_state = state is not None
    in_specs = [pl.BlockSpec((hb, 1, _LANES), lambda bi, hi: (hi, 0, 0)),
                pl.BlockSpec((t, wcols), lambda bi, hi: (bi, hi)),
                pl.BlockSpec((t, wcols), lambda bi, hi: (bi, nhb + hi)),
                pl.BlockSpec((t, wcols), lambda bi, hi: (bi, 2 * nhb + hi)),
                pl.BlockSpec((t, wcols), lambda bi, hi: (bi, hi))]
    args = [lg, qkv, qkv, qkv, rg]
    s_spec = pl.BlockSpec((1, hb, RET_DK, RET_DV), lambda bi, hi: (bi, hi, 0, 0))
    if has_state:
        in_specs.append(s_spec)
        args.append(state)
    est = 2 * (4 * _nbytes((t, wcols), _BF16) + _nbytes((t, wcols), _F32)
               + 2 * _nbytes((hb, RET_DK, RET_DV), _F32)) + 8 * _nbytes((cc, max(cc, RET_DV)), _F32)
    kern = functools.partial(_retention_kernel, hb=hb, cc=cc, n_chunks=t // cc, has_state=has_state)
    return pl.pallas_call(
        kern,
        grid=(b, nhb),
        in_specs=in_specs,
        out_specs=(pl.BlockSpec((t, wcols), lambda bi, hi: (bi, hi)), s_spec),
        out_shape=(jax.ShapeDtypeStruct((m, RET_V), _BF16),
                   jax.ShapeDtypeStruct((b, RET_HEADS, RET_DK, RET_DV), _F32)),
        scratch_shapes=[pltpu.VMEM((RET_DK, RET_DV), _F32)],
        compiler_params=_cparams(("arbitrary", "arbitrary"), est),
        name="retention",
    )(*args)


def _swa_core(qst, k_even, k_odd, v_even, v_odd, sink_even, sink_odd, valid):
    def weights(kx, sink):
        s = lax.dot_general(qst, kx, (((1,), (1,)), ((), ())), preferred_element_type=_F32) * (SWA_HD ** -0.5)
        if valid is not None:
            s = jnp.where(valid, s, NEG_INF)
        mx = jnp.maximum(jnp.max(s, axis=-1, keepdims=True), sink)
        return jnp.exp(s - mx).astype(_BF16), jnp.exp(sink - mx)

    e_even, t_even = weights(k_even, sink_even)
    e_odd, t_odd = weights(k_odd, sink_odd)
    o_even = jnp.dot(e_even, v_even, preferred_element_type=_F32)
    o_odd = jnp.dot(e_odd, v_odd, preferred_element_type=_F32)
    low = lax.broadcasted_iota(jnp.int32, o_even.shape, 1) < SWA_HD
    num = jnp.where(low, o_even, o_odd)
    den = jnp.where(low, pltpu.roll(o_even, SWA_HD, 1) + t_even, pltpu.roll(o_odd, SWA_HD, 1) + t_odd)
    return num / den


def _swa_prompt_kernel(q_ref, k_ref, v_ref, sink_ref, o_ref, slab_ref, *, n_blocks, unroll):
    gw = SWA_GROUP * SWA_HD
    rows = _SWA_QB * CHUNK
    win = _SWA_QB + WINDOW_CHUNKS

    for par in range(2):
        slabs = _pair_slabs(k_ref[...], par, fill=0.0) + _pair_slabs(v_ref[...], par, fill=1.0)
        for c, x in enumerate(slabs):
            slab_ref[par, c] = x.astype(_BF16)

    def valid_mask(n_key_chunks, first):
        shape = (SWA_PAIRS * rows, n_key_chunks * CHUNK)
        qc = (lax.broadcasted_iota(jnp.int32, shape, 0) // CHUNK) % _SWA_QB
        kc = lax.broadcasted_iota(jnp.int32, shape, 1) // CHUNK
        if first:
            return kc <= qc
        return (kc >= qc) & (kc <= qc + WINDOW_CHUNKS)

    valid_first = valid_mask(_SWA_QB, True)
    valid = valid_mask(win, False)
    for par in range(2):
        sink_even = sink_ref[par, 0]
        sink_odd = sink_ref[par, 1]
        lane0 = par * gw

        def block(r, kr, mask):
            qst = jnp.concatenate(
                [q_ref[r, lane0 + p * _LANES:lane0 + (p + 1) * _LANES] for p in range(SWA_PAIRS)], axis=0)
            o = _swa_core(qst, slab_ref[par, 0, kr, :], slab_ref[par, 1, kr, :],
                          slab_ref[par, 2, kr, :], slab_ref[par, 3, kr, :], sink_even, sink_odd, mask)
            for p in range(SWA_PAIRS):
                o_ref[r, lane0 + p * _LANES:lane0 + (p + 1) * _LANES] = (
                    o[p * rows:(p + 1) * rows].astype(o_ref.dtype))

        block(pl.ds(0, rows), pl.ds(0, rows), valid_first)
        if n_blocks > 1:
            def body(b, carry):
                r = pl.ds(pl.multiple_of(b * rows, rows), rows)
                kr = pl.ds(pl.multiple_of(b * rows - WINDOW_CHUNKS * CHUNK, CHUNK), win * CHUNK)
                block(r, kr, valid)
                return carry

            lax.fori_loop(1, n_blocks, body, 0, unroll=unroll)


def _sink_columns(sinks, rows):
    s = sinks.astype(_F32).reshape(SWA_KV_HEADS, SWA_PAIRS, 2).transpose(0, 2, 1)
    return jnp.repeat(s, rows, axis=-1)[..., None]


def _swa_prompt(bq, fo, sinks, b, t):
    m = bq.shape[0]
    gw2 = 2 * SWA_GROUP * SWA_HD
    rows = _SWA_QB * CHUNK
    assert t % rows == 0 and WINDOW_CHUNKS == _SWA_QB
    n_blocks = t // rows
    unroll = next(u for u in (3, 2, 1) if (n_blocks - 1) % u == 0)
    sink_cols = _sink_columns(sinks, rows)
    q0 = 3 * RET_QK // gw2
    k0 = (fo.shape[1] - 2 * SWA_KV) // _LANES
    v0 = k0 + SWA_KV // _LANES
    est = (2 * (2 * _nbytes((t, gw2), _BF16) + 2 * _nbytes((t, _LANES), _F32))
           + _nbytes((2, 4, t, _LANES), _BF16) + (8 << 20))
    return pl.pallas_call(
        functools.partial(_swa_prompt_kernel, n_blocks=n_blocks, unroll=unroll),
        grid=(b, SWA_KV_HEADS // 2),
        in_specs=[pl.BlockSpec((t, gw2), lambda bi, kp: (bi, q0 + kp)),
                  pl.BlockSpec((t, _LANES), lambda bi, kp: (bi, k0 + kp)),
                  pl.BlockSpec((t, _LANES), lambda bi, kp: (bi, v0 + kp)),
                  pl.BlockSpec((2, 2, SWA_PAIRS * rows, 1), lambda bi, kp: (kp, 0, 0, 0))],
        out_specs=pl.BlockSpec((t, gw2), lambda bi, kp: (bi, kp)),
        out_shape=jax.ShapeDtypeStruct((m, SWA_Q), _BF16),
        scratch_shapes=[pltpu.VMEM((2, 4, t, _LANES), _BF16)],
        compiler_params=_cparams(("arbitrary", "arbitrary"), est),
        name="swa_prompt",
    )(bq, fo, fo, sink_cols)


def _swa_cache_kernel(q_ref, kvn_ref, kc_ref, vc_ref, sink_ref, o_ref, *, L):
    gw = SWA_GROUP * SWA_HD
    kvn = kvn_ref[...]
    k_all = jnp.concatenate([kc_ref[0], kvn[:, :SWA_KV]], axis=0)
    v_all = jnp.concatenate([vc_ref[0], kvn[:, SWA_KV:]], axis=0)
    for kv in range(SWA_KV_HEADS):
        lanes = slice(_LANES * (kv // 2), _LANES * (kv // 2 + 1))
        slabs = _pair_slabs(k_all[:, lanes], kv % 2, fill=0.0) + _pair_slabs(v_all[:, lanes], kv % 2, fill=1.0)
        qst = jnp.concatenate(
            [q_ref[:, kv * gw + p * _LANES:kv * gw + (p + 1) * _LANES] for p in range(SWA_PAIRS)], axis=0)
        o = _swa_core(qst, *[x.astype(_BF16) for x in slabs], sink_ref[kv, 0], sink_ref[kv, 1], None)
        for p in range(SWA_PAIRS):
            lo = kv * gw + p * _LANES
            o_ref[:, lo:lo + _LANES] = o[p * L:(p + 1) * L].astype(o_ref.dtype)


def _swa_cache(bq, fo, k_cache, v_cache, sinks, b, t):
    m = bq.shape[0]
    rows = k_cache.shape[1]
    kc = k_cache.reshape(b, rows, SWA_KV)
    vc = v_cache.reshape(b, rows, SWA_KV)
    sink_cols = _sink_columns(sinks, t)
    assert (fo.shape[1] - 2 * SWA_KV) % (2 * SWA_KV) == 0
    kv0 = fo.shape[1] // (2 * SWA_KV) - 1
    est = 2 * (2 * _nbytes((t, SWA_Q), _BF16) + _nbytes((t, 2 * SWA_KV), _F32)
               + 2 * _nbytes((rows, SWA_KV), _F32)) + (4 << 20)
    return pl.pallas_call(
        functools.partial(_swa_cache_kernel, L=t),
        grid=(b,),
        in_specs=[pl.BlockSpec((t, SWA_Q), lambda bi: (bi, 3 * RET_QK // SWA_Q)),
                  pl.BlockSpec((t, 2 * SWA_KV), lambda bi: (bi, kv0)),
                  pl.BlockSpec((1, rows, SWA_KV), lambda bi: (bi, 0, 0)),
                  pl.BlockSpec((1, rows, SWA_KV), lambda bi: (bi, 0, 0)),
                  pl.BlockSpec((SWA_KV_HEADS, 2, SWA_PAIRS * t, 1), lambda bi: (0, 0, 0, 0))],
        out_specs=pl.BlockSpec((t, SWA_Q), lambda bi: (bi, 0)),
        out_shape=jax.ShapeDtypeStruct((m, SWA_Q), _BF16),
        compiler_params=_cparams(("arbitrary",), est),
        name="swa_cache",
    )(bq, fo, kc, vc, sink_cols)


def _bf16_weight(w_ref, wbf_ref):
    if wbf_ref is None:
        return w_ref[0] if len(w_ref.shape) == 3 else w_ref[...]
    wbf_ref[0] = w_ref[...].astype(_BF16)
    return wbf_ref[0]


def _tile_major(w, tn):
    k, n = w.shape
    return w.reshape(k, n // tn, tn).transpose(1, 0, 2)


def _w_cols(w):
    return w.shape[0] * w.shape[2] if w.ndim == 3 else w.shape[1]


def _w_spec(w, k, tn, tile):
    if w.ndim == 3:
        assert w.shape[1:] == (k, tn), (w.shape, k, tn)
        return pl.BlockSpec((1, k, tn), lambda i, j: (tile(i, j), 0, 0))
    return pl.BlockSpec((k, tn), lambda i, j: (0, tile(i, j)))


def _w_copy(k, n, tn, tile):
    return jax.ShapeDtypeStruct((n // tn, k, tn), _BF16), pl.BlockSpec((1, k, tn), lambda i, j: (tile(i, j), 0, 0))


def _merge_kernel(a_ref, o_ref, wr_ref, ws_ref, ga_ref, gb_ref, out_ref, wrb_ref=None, wsb_ref=None):
    yr = jnp.dot(a_ref[...], _bf16_weight(wr_ref, wrb_ref), preferred_element_type=_F32)
    ys = jnp.dot(o_ref[...], _bf16_weight(ws_ref, wsb_ref), preferred_element_type=_F32)
    out_ref[...] = (jax.nn.sigmoid(ga_ref[...]) * yr + jax.nn.sigmoid(gb_ref[...]) * ys).astype(out_ref.dtype)


def _merge(a_ret, o_swa, fo, w_ret_o, w_swa_o):
    m = a_ret.shape[0]
    d = _w_cols(w_ret_o)
    emit = w_ret_o.dtype == _F32
    tm = min(_TM, m)
    tn = _TN
    nd = d // tn
    g0 = RET_V // tn
    col = lambda i, j: j
    assert not emit or m == tm, "the bf16 copies are written by a single row tile"
    est = 2 * (_nbytes((tm, RET_V), _BF16) + _nbytes((tm, SWA_Q), _BF16) + _nbytes((RET_V, tn), w_ret_o.dtype)
               + _nbytes((SWA_Q, tn), w_swa_o.dtype) + 2 * _nbytes((tm, tn), _F32) + _nbytes((tm, tn), _BF16)
               ) + 4 * _nbytes((tm, tn), _F32)
    out_specs = [pl.BlockSpec((tm, tn), lambda i, j: (i, j))]
    out_shape = [jax.ShapeDtypeStruct((m, d), _BF16)]
    if emit:
        for k in (RET_V, SWA_Q):
            copy_shape, copy_spec = _w_copy(k, d, tn, col)
            out_shape.append(copy_shape)
            out_specs.append(copy_spec)
        est += 2 * (_nbytes((RET_V, tn), _BF16) + _nbytes((SWA_Q, tn), _BF16))
    return pl.pallas_call(
        _merge_kernel,
        grid=(m // tm, nd),
        in_specs=[pl.BlockSpec((tm, RET_V), lambda i, j: (i, 0)),
                  pl.BlockSpec((tm, SWA_Q), lambda i, j: (i, 0)),
                  _w_spec(w_ret_o, RET_V, tn, col),
                  _w_spec(w_swa_o, SWA_Q, tn, col),
                  pl.BlockSpec((tm, tn), lambda i, j: (i, g0 + j)),
                  pl.BlockSpec((tm, tn), lambda i, j: (i, g0 + nd + j))],
        out_specs=out_specs,
        out_shape=out_shape,
        compiler_params=_cparams(("arbitrary", "arbitrary"), est),
        name="merge_emit" if emit else "merge",
    )(a_ret, o_swa, w_ret_o, w_swa_o, fo, fo)


def _proj_residual_kernel(*refs, tiles):
    a_ref, w_refs, (x_ref, gt_ref, o_ref) = refs[0], refs[1:1 + tiles], refs[1 + tiles:4 + tiles]
    wbf_ref = refs[4 + tiles] if len(refs) > 4 + tiles else None
    tn = w_refs[0].shape[-1]
    for c, w_ref in enumerate(w_refs):
        lanes = slice(c * tn, (c + 1) * tn)
        f = jnp.dot(a_ref[...], _bf16_weight(w_ref, wbf_ref), preferred_element_type=_F32)
        o_ref[:, lanes] = x_ref[:, lanes] + _per_seq(f, gt_ref[:, :, lanes], lambda v, gt: gt * v)


def _proj_residual(a, w, x, gt, t, tm_cap, tiles, name):
    m, k = a.shape
    d = _w_cols(w)
    emit = w.dtype == _F32
    tm = _row_tile(m, t, tm_cap)
    tn = _TN
    tiles = 1 if emit else min(tiles, d // tn)
    wd = tiles * tn
    assert d % wd == 0
    assert not emit or m == tm, "the bf16 copy is written by a single row tile"
    est = 2 * (_nbytes((tm, k), _BF16) + tiles * _nbytes((k, tn), w.dtype) + 2 * _nbytes((tm, wd), _F32)
               ) + 2 * _nbytes((tm, tn), _F32)
    out_specs = [pl.BlockSpec((tm, wd), lambda i, j: (i, j))]
    out_shape = [jax.ShapeDtypeStruct((m, d), _F32)]
    if emit:
        copy_shape, copy_spec = _w_copy(k, d, tn, lambda i, j: j)
        out_shape.append(copy_shape)
        out_specs.append(copy_spec)
        est += 2 * _nbytes((k, tn), _BF16)
    return pl.pallas_call(
        functools.partial(_proj_residual_kernel, tiles=tiles),
        grid=(m // tm, d // wd),
        in_specs=[pl.BlockSpec((tm, k), lambda i, j: (i, 0)),
                  *[_w_spec(w, k, tn, lambda i, j, c=c: tiles * j + c) for c in range(tiles)],
                  pl.BlockSpec((tm, wd), lambda i, j: (i, j)),
                  _mod_spec(tm, t, wd, lambda i, j: j)],
        out_specs=out_specs,
        out_shape=out_shape,
        compiler_params=_cparams(("arbitrary", "arbitrary"), est),
        name=name + "_emit" if emit else name,
    )(a, *([w] * tiles), x, gt)


def _up_kernel(*refs, tps, nf, tiles, emit):
    h_ref, wu_refs, wv_refs = refs[0], refs[1:1 + tiles], refs[1 + tiles:1 + 2 * tiles]
    wc_ref, bc_ref, cs_ref, g_ref, tail_ref = refs[1 + 2 * tiles:6 + 2 * tiles]
    wub_ref, wvb_ref = refs[6 + 2 * tiles:8 + 2 * tiles] if emit else (None, None)
    u_scr, v_scr, carry_ref = refs[-3:]
    i = pl.program_id(0)
    j = pl.program_id(1)
    tm = u_scr.shape[0]
    tn = u_scr.shape[1] // tiles
    n_full, rem = divmod(nf, tiles)

    def ordered_after(x, token):
        bits = pltpu.bitcast(token, jnp.uint32)
        zero = pltpu.bitcast((bits >> 16) >> 16, _F32)
        return (x.reshape(tm // _SUBLANES, _SUBLANES, tn) + zero[None]).reshape(tm, tn)

    def matmul(k, token=None):
        h = h_ref[...]
        for c in range(k):
            lanes = slice(c * tn, (c + 1) * tn)
            u = jnp.dot(h, _bf16_weight(wu_refs[c], wub_ref), preferred_element_type=_F32)
            u_scr[:, lanes] = u if (token is None or c > 0) else ordered_after(u, token)
            v_scr[:, lanes] = jnp.dot(h, _bf16_weight(wv_refs[c], wvb_ref), preferred_element_type=_F32)

    def epilogue_tile(lanes):
        jp = j - 1
        u = u_scr[:, lanes]
        v = v_scr[:, lanes]
        cs = cs_ref[:, :, lanes]
        spt = cs.shape[0]
        rows = tm // spt
        c0 = cs[:, 0:1, :]
        c1 = cs[:, 1:2, :]
        if tps > 1:
            first = (i % tps) == 0
            prev = carry_ref[jp, :, lanes]
            c0 = jnp.where(first, c0, prev[_SUBLANES - 2:_SUBLANES - 1][None])
            c1 = jnp.where(first, c1, prev[_SUBLANES - 1:_SUBLANES][None])
            carry_ref[jp, :, lanes] = u[tm - _SUBLANES:tm]
        wc = wc_ref[:, lanes]
        bc = bc_ref[:, lanes]
        if spt == 1:
            r = lax.broadcasted_iota(jnp.int32, (tm, tn), 0)
            p1 = jnp.where(r == 0, c1[0], pltpu.roll(u, 1, 0))
            p2 = jnp.where(r == 0, c0[0], jnp.where(r == 1, c1[0], pltpu.roll(u, 2, 0)))
            uc = wc[0:1] * p2 + wc[1:2] * p1 + wc[2:3] * u + bc
            g = jax.nn.gelu(uc) * v
            tail_ref[0, 0, :, lanes] = u[tm - _SUBLANES:tm]
        else:
            u3 = u.reshape(spt, rows, tn)
            r = lax.broadcasted_iota(jnp.int32, (spt, rows, tn), 1)
            p1 = jnp.where(r == 0, c1, pltpu.roll(u3, 1, 1))
            p2 = jnp.where(r == 0, c0, jnp.where(r == 1, c1, pltpu.roll(u3, 2, 1)))
            uc = wc[0:1][None] * p2 + wc[1:2][None] * p1 + wc[2:3][None] * u3 + bc[None]
            g = (jax.nn.gelu(uc) * v.reshape(spt, rows, tn)).reshape(tm, tn)
            tail_ref[0, :, :, lanes] = u3[:, rows - _SUBLANES:rows, :]
        g_ref[:, lanes] = g.astype(g_ref.dtype)
        return jnp.max(g.reshape(tm // _SUBLANES, _SUBLANES, tn), axis=0)

    def epilogue(k):
        token = None
        for c in range(k):
            part = epilogue_tile(slice(c * tn, (c + 1) * tn))
            token = part if token is None else jnp.maximum(token, part)
        return token

    @pl.when(j == 0)
    def _first():
        if tps > 1:
            @pl.when(i == 0)
            def _init():
                carry_ref[...] = jnp.zeros_like(carry_ref)
        matmul(tiles)

    @pl.when((j > 0) & (j < n_full))
    def _steady():
        token = epilogue(tiles)
        matmul(tiles, None if emit else token)

    if rem:
        @pl.when(j == n_full)
        def _short():
            token = epilogue(tiles)
            matmul(rem, None if emit else token)

    @pl.when(j == n_full + (1 if rem else 0))
    def _drain():
        epilogue(rem if rem else tiles)


def _up_conv_geglu(h, w_u, w_v, w_conv, b_conv, conv_state, t):
    m, d = h.shape
    f = w_conv.shape[1]
    (wu_arr, wu_tile0), (wv_arr, wv_tile0) = w_u, w_v
    emit = wu_arr.dtype == _F32
    tm = _row_tile(m, t, _TM)
    tn = _TN_FF
    assert f % tn == 0, (f, tn)
    assert not emit or m == tm, "the bf16 copies are written by a single row tile"
    nf = f // tn
    tiles = 1 if emit else min(_UP_TILES, nf)
    n_groups = -(-nf // tiles)
    w = tiles * tn
    spt, tps = _seq_block(tm, t)
    rows = tm // spt
    assert rows >= _SUBLANES and conv_state.shape[1] == CONV_W - 1
    cur = lambda j: jnp.minimum(j, n_groups - 1)
    prev = lambda j: jnp.maximum(j - 1, 0)
    tile = lambda j, c: jnp.minimum(tiles * cur(j) + c, nf - 1)
    est = 2 * (_nbytes((tm, d), _BF16) + 2 * tiles * _nbytes((d, tn), wu_arr.dtype) + _nbytes((tm, w), _BF16)
               + _nbytes((spt, _SUBLANES, w), _F32)) + _nbytes((n_groups, _SUBLANES, w), _F32) + (
        2 * _nbytes((tm, w), _F32) + 10 * _nbytes((tm, tn), _F32))
    out_specs = [pl.BlockSpec((tm, w), lambda i, j: (i, prev(j))),
                 pl.BlockSpec((1, spt, _SUBLANES, w), lambda i, j: (i, 0, 0, prev(j)))]
    out_shape = [jax.ShapeDtypeStruct((m, f), _BF16),
                 jax.ShapeDtypeStruct((m // tm, spt, _SUBLANES, f), _F32)]
    if emit:
        for _ in range(2):
            copy_shape, copy_spec = _w_copy(d, f, tn, lambda i, j: cur(j))
            out_shape.append(copy_shape)
            out_specs.append(copy_spec)
        est += 4 * _nbytes((d, tn), _BF16)
    w_specs = [_w_spec(arr, d, tn, lambda i, j, c=c, t0=t0: t0 + tile(j, c))
               for arr, t0 in ((wu_arr, wu_tile0), (wv_arr, wv_tile0)) for c in range(tiles)]
    return pl.pallas_call(
        functools.partial(_up_kernel, tps=tps, nf=nf, tiles=tiles, emit=emit),
        grid=(m // tm, n_groups + 1),
        in_specs=[pl.BlockSpec((tm, d), lambda i, j: (i, 0)), *w_specs,
                  pl.BlockSpec((CONV_W, w), lambda i, j: (0, prev(j))),
                  pl.BlockSpec((1, w), lambda i, j: (0, prev(j))),
                  pl.BlockSpec((spt, CONV_W - 1, w), lambda i, j: (i // tps, 0, prev(j)))],
        out_specs=out_specs,
        out_shape=out_shape,
        scratch_shapes=[pltpu.VMEM((tm, w), _F32), pltpu.VMEM((tm, w), _F32),
                        pltpu.VMEM((n_groups, _SUBLANES, w), _F32)],
        compiler_params=_cparams(("arbitrary", "arbitrary"), est),
        name="up_conv_geglu_emit" if emit else "up_conv_geglu",
    )(h, *([wu_arr] * tiles), *([wv_arr] * tiles), w_conv, b_conv.reshape(1, f), conv_state)


def _layer(x, mod, pos0, ret_state, swa_kc, swa_vc, conv_state, w):
    b, t, d = x.shape
    m = b * t
    x2 = x.reshape(m, d)
    sh1, sc1, gt1, sh2, sc2, gt2 = [v.reshape(b, 1, d) for v in jnp.split(mod, 6, axis=-1)]
    bf = {}

    h = _norm_mod(x2, w["g_mix"], sc1, sh1, t)
    bq, fo, *w_bf = _in_proj(h, w["w_in"], t, pos0)
    bf.update(zip(("w_in",), w_bf))
    a_ret, s_new = _retention(bq, fo, ret_state, b, t)
    if swa_kc is None:
        o_swa = _swa_prompt(bq, fo, w["swa_sinks"], b, t)
        keep = min(WINDOW, t)
    else:
        o_swa = _swa_cache(bq, fo, swa_kc, swa_vc, w["swa_sinks"], b, t)
        keep = t
    merged, *w_bf = _merge(a_ret, o_swa, fo, w["w_ret_o"], w["w_swa_o"])
    bf.update(zip(("w_ret_o", "w_swa_o"), w_bf))
    x1, *w_bf = _proj_residual(merged, w["w_out"], x2, gt1, t, _TM, 2, "out_proj")
    bf.update(zip(("w_out",), w_bf))

    h2 = _norm_mod(x1, w["g_ffn"], sc2, sh2, t)
    f = w["b_conv"].shape[0]
    cs = jnp.zeros((b, CONV_W - 1, f), _F32) if conv_state is None else conv_state
    if "w_up_u" in w:
        w_u, w_v = (w["w_up_u"], 0), (w["w_up_v"], 0)
    else:
        w_u, w_v = (w["w_up"], 0), (w["w_up"], f // _TN_FF)
    g, tails, *w_bf = _up_conv_geglu(h2, w_u, w_v, w["w_conv"], w["b_conv"], cs, t)
    bf.update(zip(("w_up_u", "w_up_v"), w_bf))
    x3, = _proj_residual(g, w["w_down"], x1, gt2, t, _TM_DOWN, 1, "down_proj")
    y = _final_norm(x3, w["g_final"])

    skv = fo.reshape(b, t, fo.shape[1])[:, t - keep:, fo.shape[1] - 2 * SWA_KV:]
    k_new = skv[:, :, :SWA_KV].reshape(b, keep, SWA_KV_HEADS, SWA_HD)
    v_new = skv[:, :, SWA_KV:].reshape(b, keep, SWA_KV_HEADS, SWA_HD)
    segs = tails.shape[0] * tails.shape[1] // b
    tails = tails.reshape(b, segs, _SUBLANES, f)
    conv_new = tails[:, segs - 1, _SUBLANES - (CONV_W - 1):, :]
    return y.reshape(b, t, d), s_new, k_new, v_new, conv_new, bf


def kernel(x_prompt, x_sample, cache_swa_k, cache_swa_v, state_ret, state_conv, c_prompt, c_sample,
           g_mix, g_ffn, w_ada, b_ada, w_in, swa_sinks, w_ret_o, w_swa_o, w_out,
           w_up, w_conv, b_conv, w_down, g_final):
    depth = g_mix.shape[0]
    assert depth == 1, "single-layer stack"
    bp = x_prompt.shape[0]
    bs = x_sample.shape[0]
    l = 0

    c_all = jnp.concatenate([c_prompt, c_sample], axis=0)
    rows = -(-c_all.shape[0] // 16) * 16
    c_all = jnp.pad(c_all, ((0, rows - c_all.shape[0]), (0, 0)))
    mod = _ada(c_all, w_ada[l], b_ada[l])

    w = {
        "g_mix": g_mix[l], "g_ffn": g_ffn[l], "g_final": g_final, "swa_sinks": swa_sinks[l],
        "w_in": w_in[l], "w_ret_o": w_ret_o[l], "w_swa_o": w_swa_o[l], "w_out": w_out[l], "w_up": w_up[l],
        "w_down": _tile_major(w_down[l].astype(_BF16), _TN),
        "w_conv": w_conv[l], "b_conv": b_conv[l],
    }

    ys, ss, ks, vs, cs, w_bf = _layer(x_sample, mod[bp:bp + bs], PAST_LEN, state_ret[l], cache_swa_k[l],
                                      cache_swa_v[l], state_conv[l], w)
    yp, sp, kp, vp, cp, _ = _layer(x_prompt, mod[:bp], 0, None, None, None, None, {**w, **w_bf})
    stack = lambda a: a[None]
    return (yp, ys, stack(sp), stack(kp), stack(vp), stack(cp), stack(ss), stack(ks), stack(vs), stack(cs))
```

```python
import functools

import jax
import jax.numpy as jnp
import numpy as np
from jax import lax
from jax.experimental import pallas as pl
from jax.experimental.pallas import tpu as pltpu

_F32 = jnp.float32
_BF16 = jnp.bfloat16

CHUNK = 64
RET_HEADS = 8
RET_DK = 256
RET_DV = 256
RET_QK = RET_HEADS * RET_DK
RET_V = RET_HEADS * RET_DV
ROPE_BASE = 10000.0
SWA_HEADS = 32
SWA_KV_HEADS = 4
SWA_HD = 64
SWA_GROUP = SWA_HEADS // SWA_KV_HEADS
SWA_PAIRS = SWA_GROUP // 2
SWA_Q = SWA_HEADS * SWA_HD
SWA_KV = SWA_KV_HEADS * SWA_HD
WINDOW = 128
WINDOW_CHUNKS = WINDOW // CHUNK
CONV_W = 3
EPS = 1e-6
NEG_INF = -1e30
PAST_LEN = 2048

_V7X_VMEM_BYTES = 64 * 1024 * 1024
_LANES = 128
_SUBLANES = 8

_TM = 1024
_TM_DOWN = 512
_TM_NORM = 512
_TN = 2 * SWA_KV
_TN_EMIT = 512
_TN_ADA = 512
_RET_CHUNK = 256
_SWA_QB = 2
_TN_FF = 256
_UP_TILES = 2


def _cparams(sem, vmem_estimate, flags=None):
    limit = min(int(vmem_estimate * 1.15) + (4 << 20), _V7X_VMEM_BYTES - (6 << 20))
    return pltpu.CompilerParams(dimension_semantics=sem, vmem_limit_bytes=limit, flags=flags)


def _nbytes(shape, dtype):
    return int(np.prod(shape)) * jnp.dtype(dtype).itemsize


def _row_tile(m, t, cap):
    tm = min(cap, m)
    assert m % tm == 0 and (t % tm == 0 or tm % t == 0), (m, t, tm)
    return tm


def _seq_block(tm, t):
    return max(1, tm // t), max(1, t // tm)


def _mod_spec(tm, t, tn, col):
    spt, tps = _seq_block(tm, t)
    return pl.BlockSpec((spt, 1, tn), lambda i, j: (i // tps, 0, col(i, j)))


def _per_seq(val, mod, fn):
    spt = mod.shape[0]
    if spt == 1:
        return fn(val, mod[0])
    tm, n = val.shape
    return fn(val.reshape(spt, tm // spt, n), mod).reshape(tm, n)


def _ada_kernel(c_ref, w_ref, b_ref, o_ref):
    c = c_ref[...]
    a = (c * jax.nn.sigmoid(c)).astype(_BF16)
    o_ref[...] = jnp.dot(a, w_ref[...].astype(_BF16), preferred_element_type=_F32) + b_ref[...]


def _ada(c, w_ada, b_ada):
    rows, d = c.shape
    n = w_ada.shape[1]
    tn = _TN_ADA
    est = 2 * (_nbytes((rows, d), _F32) + _nbytes((d, tn), _F32) + _nbytes((rows, tn), _F32)) + _nbytes((d, tn), _BF16)
    return pl.pallas_call(
        _ada_kernel,
        grid=(n // tn,),
        in_specs=[pl.BlockSpec((rows, d), lambda j: (0, 0)),
                  pl.BlockSpec((d, tn), lambda j: (0, j)),
                  pl.BlockSpec((1, tn), lambda j: (0, j))],
        out_specs=pl.BlockSpec((rows, tn), lambda j: (0, j)),
        out_shape=jax.ShapeDtypeStruct((rows, n), _F32),
        compiler_params=_cparams(("arbitrary",), est),
        name="ada_mod",
    )(c, w_ada, b_ada.reshape(1, n))


def _norm_rows(m):
    return _TM_NORM if m >= 8 * _TM_NORM else _TM_NORM // 2


def _rms(x, g):
    y = x * lax.rsqrt(jnp.mean(x * x, axis=-1, keepdims=True) + EPS)
    return y * g


def _norm_mod_kernel(x_ref, g_ref, sc_ref, sh_ref, o_ref):
    y = _rms(x_ref[...], g_ref[...])
    h = _per_seq(y, sc_ref[...], lambda v, sc: v * (1.0 + sc))
    h = _per_seq(h, sh_ref[...], lambda v, sh: v + sh)
    o_ref[...] = h.astype(o_ref.dtype)


def _norm_mod(x, g, sc, sh, t):
    m, d = x.shape
    tm = _row_tile(m, t, _norm_rows(m))
    est = 2 * (_nbytes((tm, d), _F32) + _nbytes((tm, d), _BF16)) + 3 * _nbytes((tm, d), _F32)
    return pl.pallas_call(
        _norm_mod_kernel,
        grid=(m // tm, 1),
        in_specs=[pl.BlockSpec((tm, d), lambda i, j: (i, 0)),
                  pl.BlockSpec((1, d), lambda i, j: (0, 0)),
                  _mod_spec(tm, t, d, lambda i, j: 0),
                  _mod_spec(tm, t, d, lambda i, j: 0)],
        out_specs=pl.BlockSpec((tm, d), lambda i, j: (i, 0)),
        out_shape=jax.ShapeDtypeStruct((m, d), _BF16),
        compiler_params=_cparams(("arbitrary", "arbitrary"), est),
        name="norm_mod",
    )(x, g.reshape(1, d), sc, sh)


def _final_norm_kernel(x_ref, g_ref, o_ref):
    o_ref[...] = _rms(x_ref[...], g_ref[...])


def _final_norm(x, g):
    m, d = x.shape
    tm = min(_norm_rows(m), m)
    est = 4 * _nbytes((tm, d), _F32) + 2 * _nbytes((tm, d), _F32)
    return pl.pallas_call(
        _final_norm_kernel,
        grid=(m // tm,),
        in_specs=[pl.BlockSpec((tm, d), lambda i: (i, 0)),
                  pl.BlockSpec((1, d), lambda i: (0, 0))],
        out_specs=pl.BlockSpec((tm, d), lambda i: (i, 0)),
        out_shape=jax.ShapeDtypeStruct((m, d), _F32),
        compiler_params=_cparams(("arbitrary",), est),
        name="final_norm",
    )(x, g.reshape(1, d))


def _pair_slabs(slab, parity, fill):
    low = lax.broadcasted_iota(jnp.int32, slab.shape, 1) < SWA_HD
    if parity == 0:
        even = jnp.where(low, slab, 0.0)
        odd = pltpu.roll(even, SWA_HD, 1)
    else:
        odd = jnp.where(low, 0.0, slab)
        even = pltpu.roll(odd, SWA_HD, 1)
    if fill == 0.0:
        return even, odd
    return jnp.where(low, even, fill), jnp.where(low, fill, odd)


class _InPlan:
    def __init__(self, tn, pair, d):
        w = tn * pair
        n_skv = 2 * SWA_KV // tn
        assert RET_QK % w == 0 and SWA_Q % w == 0 and (2 * d) % w == 0 and (RET_V + 2 * d) % w == 0
        assert tn % RET_DK == 0 and (2 * SWA_KV) % tn == 0 and (pair == 1 or n_skv == 1)
        self.tn, self.pair, self.w = tn, pair, w
        r = self.r = RET_QK // w
        self.t_skv = 5 * r
        self.t_gab = self.t_skv + n_skv
        self.steps = self.t_gab + 2 * d // w
        self.skv_block = (RET_V + 2 * d) // w
        self.fo_cols = RET_V + 2 * d + 2 * SWA_KV

    def w_tile(self, t):
        p, ts, tg = self.pair, self.t_skv, self.t_gab
        return jnp.where(t < ts, p * t, jnp.where(t < tg, p * ts + (t - ts), p * ts + (tg - ts) + p * (t - tg)))

    def bq_block(self, t):
        r = self.r
        return jnp.where(t < 3 * r, t, jnp.where(t < 4 * r, 3 * r - 1, jnp.where(t < 5 * r, t - r, 4 * r - 1)))

    def fo_block(self, t):
        r, ts, tg = self.r, self.t_skv, self.t_gab
        return jnp.where(t < 3 * r, 0, jnp.where(t < 4 * r, t - 3 * r, jnp.where(
            t < ts, r - 1, jnp.where(t < tg, self.skv_block + (t - ts), r + (t - tg)))))


def _in_proj_kernel(*refs, plan, emit):
    n_w = plan.pair
    h_ref, w_refs, (cos_ref, sin_ref, bq_ref, fo_ref) = refs[0], refs[1:1 + n_w], refs[1 + n_w:5 + n_w]
    if emit:
        wbf_ref, = refs[5 + n_w:]
        _bf16_weight(w_refs[0], wbf_ref)
        w_refs = (wbf_ref,)
    t = pl.program_id(1)
    r, tn = plan.r, plan.tn
    half = RET_DK // 2

    def tiles():
        for n, w_ref in enumerate(w_refs):
            yield jnp.dot(h_ref[...], _bf16_weight(w_ref, None), preferred_element_type=_F32), n * tn

    @pl.when(t < 2 * r)
    def _rope():
        c = cos_ref[...]
        s = sin_ref[...]
        scale = jnp.where(t >= r, RET_DK ** -0.5, 1.0).astype(_F32)
        for acc, col in tiles():
            for hh in range(tn // RET_DK):
                lo = hh * RET_DK
                x1 = acc[:, lo:lo + half]
                x2 = acc[:, lo + half:lo + RET_DK]
                bq_ref[:, col + lo:col + lo + half] = ((x1 * c - x2 * s) * scale).astype(_BF16)
                bq_ref[:, col + lo + half:col + lo + RET_DK] = ((x2 * c + x1 * s) * scale).astype(_BF16)

    @pl.when(((t >= 2 * r) & (t < 3 * r)) | ((t >= 4 * r) & (t < 5 * r)))
    def _bf16():
        for acc, col in tiles():
            bq_ref[:, col:col + tn] = acc.astype(_BF16)

    @pl.when(((t >= 3 * r) & (t < 4 * r)) | (t >= plan.t_gab))
    def _f32():
        for acc, col in tiles():
            fo_ref[:, col:col + tn] = acc

    @pl.when((t >= plan.t_skv) & (t < plan.t_gab))
    def _kv():
        fo_ref[:, 0:tn] = jnp.dot(h_ref[...], _bf16_weight(w_refs[0], None), preferred_element_type=_F32)


def _rope_tables(t, pos0, tm):
    half = RET_DK // 2
    inv = ROPE_BASE ** (-jnp.arange(half, dtype=_F32) * (2.0 / RET_DK))
    pos = (pos0 + jnp.arange(t, dtype=jnp.int32)).astype(_F32)
    ang = pos[:, None] * inv[None, :]
    reps = max(1, tm // t)
    return jnp.tile(jnp.cos(ang), (reps, 1)), jnp.tile(jnp.sin(ang), (reps, 1))


def _in_proj(h, w_in, t, pos0):
    m, d = h.shape
    n_in = _w_cols(w_in)
    emit = w_in.dtype == _F32
    tm = _row_tile(m, t, _TM)
    plan = _InPlan(_TN_EMIT, 1, d) if emit else _InPlan(_TN, 2, d)
    tn, w = plan.tn, plan.w
    assert n_in == plan.fo_cols + 4 * RET_QK, (n_in, d)
    assert not emit or m == tm, "the bf16 copy is written by a single row tile"
    cos, sin = _rope_tables(t, pos0, tm)
    tab_blocks = cos.shape[0] // tm
    half = RET_DK // 2
    last_tile = n_in // tn - 1

    out_shape = [jax.ShapeDtypeStruct((m, 4 * RET_QK), _BF16), jax.ShapeDtypeStruct((m, plan.fo_cols), _F32)]
    out_specs = [pl.BlockSpec((tm, w), lambda i, j: (i, plan.bq_block(j))),
                 pl.BlockSpec((tm, w), lambda i, j: (i, plan.fo_block(j)))]
    w_specs = [_w_spec(w_in, d, tn, lambda i, j, n=n: jnp.minimum(plan.w_tile(j) + n, last_tile))
               for n in range(plan.pair)]
    est = 2 * (_nbytes((tm, d), _BF16) + plan.pair * _nbytes((d, tn), w_in.dtype) + 2 * _nbytes((tm, half), _F32)
               + _nbytes((tm, w), _BF16) + _nbytes((tm, w), _F32)) + 2 * plan.pair * _nbytes((tm, tn), _F32)
    if emit:
        copy_shape, copy_spec = _w_copy(d, n_in, tn, lambda i, j: plan.w_tile(j))
        out_shape.append(copy_shape)
        out_specs.append(copy_spec)
        est += 2 * _nbytes((d, tn), _BF16)
    return pl.pallas_call(
        functools.partial(_in_proj_kernel, plan=plan, emit=emit),
        grid=(m // tm, plan.steps),
        in_specs=[pl.BlockSpec((tm, d), lambda i, j: (i, 0)), *w_specs,
                  pl.BlockSpec((tm, half), lambda i, j: (i % tab_blocks, 0)),
                  pl.BlockSpec((tm, half), lambda i, j: (i % tab_blocks, 0))],
        out_specs=out_specs,
        out_shape=out_shape,
        compiler_params=_cparams(("arbitrary", "arbitrary"), est),
        name="in_proj_emit" if emit else "in_proj",
    )(h, *([w_in] * plan.pair), cos, sin)


def _retention_kernel(*refs, hb, cc, n_chunks, has_state):
    if has_state:
        lg_ref, q_ref, k_ref, v_ref, rg_ref, s0_ref, a_ref, sout_ref, s_scr = refs
    else:
        lg_ref, q_ref, k_ref, v_ref, rg_ref, a_ref, sout_ref, s_scr = refs
    row = lax.broadcasted_iota(jnp.int32, (cc, cc), 0)
    col = lax.broadcasted_iota(jnp.int32, (cc, cc), 1)
    diff = (row - col).astype(_F32)
    idx = lax.broadcasted_iota(jnp.int32, (cc, 1), 0).astype(_F32)
    for hh in range(hb):
        lg = lg_ref[hh][:, :1]
        dmask = jnp.where(diff >= 0, jnp.exp(lg * jnp.maximum(diff, 0.0)), 0.0)
        dq = jnp.exp(lg * (idx + 1.0))
        dk = jnp.exp(lg * (cc - 1.0 - idx))
        ds = jnp.exp(lg * float(cc))
        cols = slice(hh * RET_DK, (hh + 1) * RET_DK)
        if has_state:
            s_scr[...] = s0_ref[0, hh]
        else:
            s_scr[...] = jnp.zeros_like(s_scr)

        def body(c, carry):
            r = pl.ds(pl.multiple_of(c * cc, cc), cc)
            q = q_ref[r, cols]
            k = k_ref[r, cols]
            v = v_ref[r, cols]
            state = s_scr[...]
            scores = lax.dot_general(q, k, (((1,), (1,)), ((), ())), preferred_element_type=_F32) * dmask
            y = jnp.dot(scores.astype(_BF16), v, preferred_element_type=_F32)
            y = y + jnp.dot(q, state.astype(_BF16), preferred_element_type=_F32) * dq
            k_dec = (k.astype(_F32) * dk).astype(_BF16)
            s_scr[...] = state * ds + lax.dot_general(
                k_dec, v, (((0,), (0,)), ((), ())), preferred_element_type=_F32)
            mu = jnp.mean(y, axis=-1, keepdims=True)
            dev = y - mu
            var = jnp.mean(dev * dev, axis=-1, keepdims=True)
            yn = dev * lax.rsqrt(var + EPS)
            g = rg_ref[r, cols]
            a_ref[r, cols] = (g * jax.nn.sigmoid(g) * yn).astype(_BF16)
            return carry

        lax.fori_loop(0, n_chunks, body, 0, unroll=2 if n_chunks % 2 == 0 else 1)
        sout_ref[0, hh] = s_scr[...]


def _retention(qkv, rg, state, b, t):
    m = qkv.shape[0]
    cc = min(_RET_CHUNK, t)
    assert t % cc == 0 and cc % min(CHUNK, t) == 0
    hb = RET_HEADS if t * RET_QK * 2 <= (1 << 20) else 1
    nhb = RET_HEADS // hb
    wcols = hb * RET_DK
    h = jnp.arange(RET_HEADS, dtype=_F32)
    lg = jnp.log1p(-jnp.exp2(-5.0 - h))
    lg = jnp.broadcast_to(lg[:, None, None], (RET_HEADS, 1, _LANES))
    has_state = state is not None
    in_specs = [pl.BlockSpec((hb, 1, _LANES), lambda bi, hi: (hi, 0, 0)),
                pl.BlockSpec((t, wcols), lambda bi, hi: (bi, hi)),
                pl.BlockSpec((t, wcols), lambda bi, hi: (bi, nhb + hi)),
                pl.BlockSpec((t, wcols), lambda bi, hi: (bi, 2 * nhb + hi)),
                pl.BlockSpec((t, wcols), lambda bi, hi: (bi, hi))]
    args = [lg, qkv, qkv, qkv, rg]
    s_spec = pl.BlockSpec((1, hb, RET_DK, RET_DV), lambda bi, hi: (bi, hi, 0, 0))
    if has_state:
        in_specs.append(s_spec)
        args.append(state)
    est = 2 * (4 * _nbytes((t, wcols), _BF16) + _nbytes((t, wcols), _F32)
               + 2 * _nbytes((hb, RET_DK, RET_DV), _F32)) + 8 * _nbytes((cc, max(cc, RET_DV)), _F32)
    kern = functools.partial(_retention_kernel, hb=hb, cc=cc, n_chunks=t // cc, has_state=has_state)
    return pl.pallas_call(
        kern,
        grid=(b, nhb),
        in_specs=in_specs,
        out_specs=(pl.BlockSpec((t, wcols), lambda bi, hi: (bi, hi)), s_spec),
        out_shape=(jax.ShapeDtypeStruct((m, RET_V), _BF16),
                   jax.ShapeDtypeStruct((b, RET_HEADS, RET_DK, RET_DV), _F32)),
        scratch_shapes=[pltpu.VMEM((RET_DK, RET_DV), _F32)],
        compiler_params=_cparams(("arbitrary", "arbitrary"), est),
        name="retention",
    )(*args)


def _swa_core(qst, k_even, k_odd, v_even, v_odd, sink_even, sink_odd, valid):
    def weights(kx, sink):
        s = lax.dot_general(qst, kx, (((1,), (1,)), ((), ())), preferred_element_type=_F32) * (SWA_HD ** -0.5)
        if valid is not None:
            s = jnp.where(valid, s, NEG_INF)
        mx = jnp.maximum(jnp.max(s, axis=-1, keepdims=True), sink)
        return jnp.exp(s - mx).astype(_BF16), jnp.exp(sink - mx)

    e_even, t_even = weights(k_even, sink_even)
    e_odd, t_odd = weights(k_odd, sink_odd)
    o_even = jnp.dot(e_even, v_even, preferred_element_type=_F32)
    o_odd = jnp.dot(e_odd, v_odd, preferred_element_type=_F32)
    low = lax.broadcasted_iota(jnp.int32, o_even.shape, 1) < SWA_HD
    num = jnp.where(low, o_even, o_odd)
    den = jnp.where(low, pltpu.roll(o_even, SWA_HD, 1) + t_even, pltpu.roll(o_odd, SWA_HD, 1) + t_odd)
    return num / den


def _swa_prompt_kernel(q_ref, k_ref, v_ref, sink_ref, o_ref, slab_ref, *, n_blocks, unroll):
    gw = SWA_GROUP * SWA_HD
    rows = _SWA_QB * CHUNK
    win = _SWA_QB + WINDOW_CHUNKS

    for par in range(2):
        slabs = _pair_slabs(k_ref[...], par, fill=0.0) + _pair_slabs(v_ref[...], par, fill=1.0)
        for c, x in enumerate(slabs):
            slab_ref[par, c] = x.astype(_BF16)

    def valid_mask(n_key_chunks, first):
        shape = (SWA_PAIRS * rows, n_key_chunks * CHUNK)
        qc = (lax.broadcasted_iota(jnp.int32, shape, 0) // CHUNK) % _SWA_QB
        kc = lax.broadcasted_iota(jnp.int32, shape, 1) // CHUNK
        if first:
            return kc <= qc
        return (kc >= qc) & (kc <= qc + WINDOW_CHUNKS)

    valid_first = valid_mask(_SWA_QB, True)
    valid = valid_mask(win, False)
    for par in range(2):
        sink_even = sink_ref[par, 0]
        sink_odd = sink_ref[par, 1]
        lane0 = par * gw

        def block(r, kr, mask):
            qst = jnp.concatenate(
                [q_ref[r, lane0 + p * _LANES:lane0 + (p + 1) * _LANES] for p in range(SWA_PAIRS)], axis=0)
            o = _swa_core(qst, slab_ref[par, 0, kr, :], slab_ref[par, 1, kr, :],
                          slab_ref[par, 2, kr, :], slab_ref[par, 3, kr, :], sink_even, sink_odd, mask)
            for p in range(SWA_PAIRS):
                o_ref[r, lane0 + p * _LANES:lane0 + (p + 1) * _LANES] = (
                    o[p * rows:(p + 1) * rows].astype(o_ref.dtype))

        block(pl.ds(0, rows), pl.ds(0, rows), valid_first)
        if n_blocks > 1:
            def body(b, carry):
                r = pl.ds(pl.multiple_of(b * rows, rows), rows)
                kr = pl.ds(pl.multiple_of(b * rows - WINDOW_CHUNKS * CHUNK, CHUNK), win * CHUNK)
                block(r, kr, valid)
                return carry

            lax.fori_loop(1, n_blocks, body, 0, unroll=unroll)


def _sink_columns(sinks, rows):
    s = sinks.astype(_F32).reshape(SWA_KV_HEADS, SWA_PAIRS, 2).transpose(0, 2, 1)
    return jnp.repeat(s, rows, axis=-1)[..., None]


def _swa_prompt(bq, fo, sinks, b, t):
    m = bq.shape[0]
    gw2 = 2 * SWA_GROUP * SWA_HD
    rows = _SWA_QB * CHUNK
    assert t % rows == 0 and WINDOW_CHUNKS == _SWA_QB
    n_blocks = t // rows
    unroll = next(u for u in (5, 3, 2, 1) if (n_blocks - 1) % u == 0)
    sink_cols = _sink_columns(sinks, rows)
    q0 = 3 * RET_QK // gw2
    k0 = (fo.shape[1] - 2 * SWA_KV) // _LANES
    v0 = k0 + SWA_KV // _LANES
    est = (2 * (2 * _nbytes((t, gw2), _BF16) + 2 * _nbytes((t, _LANES), _F32))
           + _nbytes((2, 4, t, _LANES), _BF16) + (8 << 20))
    return pl.pallas_call(
        functools.partial(_swa_prompt_kernel, n_blocks=n_blocks, unroll=unroll),
        grid=(b, SWA_KV_HEADS // 2),
        in_specs=[pl.BlockSpec((t, gw2), lambda bi, kp: (bi, q0 + kp)),
                  pl.BlockSpec((t, _LANES), lambda bi, kp: (bi, k0 + kp)),
                  pl.BlockSpec((t, _LANES), lambda bi, kp: (bi, v0 + kp)),
                  pl.BlockSpec((2, 2, SWA_PAIRS * rows, 1), lambda bi, kp: (kp, 0, 0, 0))],
        out_specs=pl.BlockSpec((t, gw2), lambda bi, kp: (bi, kp)),
        out_shape=jax.ShapeDtypeStruct((m, SWA_Q), _BF16),
        scratch_shapes=[pltpu.VMEM((2, 4, t, _LANES), _BF16)],
        compiler_params=_cparams(("arbitrary", "arbitrary"), est),
        name="swa_prompt",
    )(bq, fo, fo, sink_cols)


def _swa_cache_kernel(q_ref, kvn_ref, kc_ref, vc_ref, sink_ref, o_ref, *, L):
    gw = SWA_GROUP * SWA_HD
    kvn = kvn_ref[...]
    k_all = jnp.concatenate([kc_ref[0], kvn[:, :SWA_KV]], axis=0)
    v_all = jnp.concatenate([vc_ref[0], kvn[:, SWA_KV:]], axis=0)
    for kv in range(SWA_KV_HEADS):
        lanes = slice(_LANES * (kv // 2), _LANES * (kv // 2 + 1))
        slabs = _pair_slabs(k_all[:, lanes], kv % 2, fill=0.0) + _pair_slabs(v_all[:, lanes], kv % 2, fill=1.0)
        qst = jnp.concatenate(
            [q_ref[:, kv * gw + p * _LANES:kv * gw + (p + 1) * _LANES] for p in range(SWA_PAIRS)], axis=0)
        o = _swa_core(qst, *[x.astype(_BF16) for x in slabs], sink_ref[kv, 0], sink_ref[kv, 1], None)
        for p in range(SWA_PAIRS):
            lo = kv * gw + p * _LANES
            o_ref[:, lo:lo + _LANES] = o[p * L:(p + 1) * L].astype(o_ref.dtype)


def _swa_cache(bq, fo, k_cache, v_cache, sinks, b, t):
    m = bq.shape[0]
    rows = k_cache.shape[1]
    kc = k_cache.reshape(b, rows, SWA_KV)
    vc = v_cache.reshape(b, rows, SWA_KV)
    sink_cols = _sink_columns(sinks, t)
    assert (fo.shape[1] - 2 * SWA_KV) % (2 * SWA_KV) == 0
    kv0 = fo.shape[1] // (2 * SWA_KV) - 1
    est = 2 * (2 * _nbytes((t, SWA_Q), _BF16) + _nbytes((t, 2 * SWA_KV), _F32)
               + 2 * _nbytes((rows, SWA_KV), _F32)) + (4 << 20)
    return pl.pallas_call(
        functools.partial(_swa_cache_kernel, L=t),
        grid=(b,),
        in_specs=[pl.BlockSpec((t, SWA_Q), lambda bi: (bi, 3 * RET_QK // SWA_Q)),
                  pl.BlockSpec((t, 2 * SWA_KV), lambda bi: (bi, kv0)),
                  pl.BlockSpec((1, rows, SWA_KV), lambda bi: (bi, 0, 0)),
                  pl.BlockSpec((1, rows, SWA_KV), lambda bi: (bi, 0, 0)),
                  pl.BlockSpec((SWA_KV_HEADS, 2, SWA_PAIRS * t, 1), lambda bi: (0, 0, 0, 0))],
        out_specs=pl.BlockSpec((t, SWA_Q), lambda bi: (bi, 0)),
        out_shape=jax.ShapeDtypeStruct((m, SWA_Q), _BF16),
        compiler_params=_cparams(("arbitrary",), est),
        name="swa_cache",
    )(bq, fo, kc, vc, sink_cols)


def _bf16_weight(w_ref, wbf_ref):
    if wbf_ref is None:
        return w_ref[...]
    wbf_ref[...] = w_ref[...].astype(_BF16)
    return wbf_ref[...]


def _w_cols(w):
    return w.shape[1]


def _w_spec(w, k, tn, tile):
    assert w.shape[0] == k
    return pl.BlockSpec((k, tn), lambda i, j: (0, tile(i, j)))


def _w_copy(k, n, tn, tile):
    return jax.ShapeDtypeStruct((k, n), _BF16), pl.BlockSpec((k, tn), lambda i, j: (0, tile(i, j)))


def _merge_kernel(a_ref, o_ref, wr_ref, ws_ref, ga_ref, gb_ref, out_ref, wrb_ref=None, wsb_ref=None):
    yr = jnp.dot(a_ref[...], _bf16_weight(wr_ref, wrb_ref), preferred_element_type=_F32)
    ys = jnp.dot(o_ref[...], _bf16_weight(ws_ref, wsb_ref), preferred_element_type=_F32)
    out_ref[...] = (jax.nn.sigmoid(ga_ref[...]) * yr + jax.nn.sigmoid(gb_ref[...]) * ys).astype(out_ref.dtype)


def _merge(a_ret, o_swa, fo, w_ret_o, w_swa_o):
    m = a_ret.shape[0]
    d = _w_cols(w_ret_o)
    emit = w_ret_o.dtype == _F32
    tm = min(_TM, m)
    tn = _TN
    nd = d // tn
    g0 = RET_V // tn
    col = lambda i, j: j
    assert not emit or m == tm, "the bf16 copies are written by a single row tile"
    est = 2 * (_nbytes((tm, RET_V), _BF16) + _nbytes((tm, SWA_Q), _BF16) + _nbytes((RET_V, tn), w_ret_o.dtype)
               + _nbytes((SWA_Q, tn), w_swa_o.dtype) + 2 * _nbytes((tm, tn), _F32) + _nbytes((tm, tn), _BF16)
               ) + 4 * _nbytes((tm, tn), _F32)
    out_specs = [pl.BlockSpec((tm, tn), lambda i, j: (i, j))]
    out_shape = [jax.ShapeDtypeStruct((m, d), _BF16)]
    if emit:
        for k in (RET_V, SWA_Q):
            copy_shape, copy_spec = _w_copy(k, d, tn, col)
            out_shape.append(copy_shape)
            out_specs.append(copy_spec)
        est += 2 * (_nbytes((RET_V, tn), _BF16) + _nbytes((SWA_Q, tn), _BF16))
    return pl.pallas_call(
        _merge_kernel,
        grid=(m // tm, nd),
        in_specs=[pl.BlockSpec((tm, RET_V), lambda i, j: (i, 0)),
                  pl.BlockSpec((tm, SWA_Q), lambda i, j: (i, 0)),
                  _w_spec(w_ret_o, RET_V, tn, col),
                  _w_spec(w_swa_o, SWA_Q, tn, col),
                  pl.BlockSpec((tm, tn), lambda i, j: (i, g0 + j)),
                  pl.BlockSpec((tm, tn), lambda i, j: (i, g0 + nd + j))],
        out_specs=out_specs,
        out_shape=out_shape,
        compiler_params=_cparams(("arbitrary", "arbitrary"), est),
        name="merge_emit" if emit else "merge",
    )(a_ret, o_swa, w_ret_o, w_swa_o, fo, fo)


def _proj_residual_kernel(*refs, tiles):
    a_ref, w_refs, (x_ref, gt_ref, o_ref) = refs[0], refs[1:1 + tiles], refs[1 + tiles:4 + tiles]
    wbf_ref = refs[4 + tiles] if len(refs) > 4 + tiles else None
    tn = w_refs[0].shape[-1]
    for c, w_ref in enumerate(w_refs):
        lanes = slice(c * tn, (c + 1) * tn)
        f = jnp.dot(a_ref[...], _bf16_weight(w_ref, wbf_ref), preferred_element_type=_F32)
        o_ref[:, lanes] = x_ref[:, lanes] + _per_seq(f, gt_ref[:, :, lanes], lambda v, gt: gt * v)


def _proj_residual(a, w, x, gt, t, tm_cap, tiles, name):
    m, k = a.shape
    d = _w_cols(w)
    emit = w.dtype == _F32
    tm = _row_tile(m, t, tm_cap)
    tn = _TN
    tiles = 1 if emit else min(tiles, d // tn)
    wd = tiles * tn
    assert d % wd == 0
    assert not emit or m == tm, "the bf16 copy is written by a single row tile"
    est = 2 * (_nbytes((tm, k), _BF16) + tiles * _nbytes((k, tn), w.dtype) + 2 * _nbytes((tm, wd), _F32)
               ) + 2 * _nbytes((tm, tn), _F32)
    out_specs = [pl.BlockSpec((tm, wd), lambda i, j: (i, j))]
    out_shape = [jax.ShapeDtypeStruct((m, d), _F32)]
    if emit:
        copy_shape, copy_spec = _w_copy(k, d, tn, lambda i, j: j)
        out_shape.append(copy_shape)
        out_specs.append(copy_spec)
        est += 2 * _nbytes((k, tn), _BF16)
    return pl.pallas_call(
        functools.partial(_proj_residual_kernel, tiles=tiles),
        grid=(m // tm, d // wd),
        in_specs=[pl.BlockSpec((tm, k), lambda i, j: (i, 0)),
                  *[_w_spec(w, k, tn, lambda i, j, c=c: tiles * j + c) for c in range(tiles)],
                  pl.BlockSpec((tm, wd), lambda i, j: (i, j)),
                  _mod_spec(tm, t, wd, lambda i, j: j)],
        out_specs=out_specs,
        out_shape=out_shape,
        compiler_params=_cparams(("arbitrary", "arbitrary"), est),
        name=name + "_emit" if emit else name,
    )(a, *([w] * tiles), x, gt)


def _up_kernel(*refs, tps, nf, tiles, emit):
    h_ref, wu_refs, wv_refs = refs[0], refs[1:1 + tiles], refs[1 + tiles:1 + 2 * tiles]
    wc_ref, bc_ref, cs_ref, g_ref, tail_ref = refs[1 + 2 * tiles:6 + 2 * tiles]
    wub_ref, wvb_ref = refs[6 + 2 * tiles:8 + 2 * tiles] if emit else (None, None)
    u_scr, v_scr, carry_ref = refs[-3:]
    i = pl.program_id(0)
    j = pl.program_id(1)
    tm = u_scr.shape[0]
    tn = u_scr.shape[1] // tiles
    n_full, rem = divmod(nf, tiles)

    def ordered_after(x, token):
        bits = pltpu.bitcast(token, jnp.uint32)
        zero = pltpu.bitcast((bits >> 16) >> 16, _F32)
        return (x.reshape(tm // _SUBLANES, _SUBLANES, tn) + zero[None]).reshape(tm, tn)

    def matmul(k, token=None):
        h = h_ref[...]
        for c in range(k):
            lanes = slice(c * tn, (c + 1) * tn)
            u = jnp.dot(h, _bf16_weight(wu_refs[c], wub_ref), preferred_element_type=_F32)
            u_scr[:, lanes] = u if (token is None or c > 0) else ordered_after(u, token)
            v_scr[:, lanes] = jnp.dot(h, _bf16_weight(wv_refs[c], wvb_ref), preferred_element_type=_F32)

    def epilogue_tile(lanes):
        jp = j - 1
        u = u_scr[:, lanes]
        v = v_scr[:, lanes]
        cs = cs_ref[:, :, lanes]
        spt = cs.shape[0]
        rows = tm // spt
        c0 = cs[:, 0:1, :]
        c1 = cs[:, 1:2, :]
        if tps > 1:
            first = (i % tps) == 0
            prev = carry_ref[jp, :, lanes]
            c0 = jnp.where(first, c0, prev[_SUBLANES - 2:_SUBLANES - 1][None])
            c1 = jnp.where(first, c1, prev[_SUBLANES - 1:_SUBLANES][None])
            carry_ref[jp, :, lanes] = u[tm - _SUBLANES:tm]
        wc = wc_ref[:, lanes]
        bc = bc_ref[:, lanes]
        if spt == 1:
            r = lax.broadcasted_iota(jnp.int32, (tm, tn), 0)
            p1 = jnp.where(r == 0, c1[0], pltpu.roll(u, 1, 0))
            p2 = jnp.where(r == 0, c0[0], jnp.where(r == 1, c1[0], pltpu.roll(u, 2, 0)))
            uc = wc[0:1] * p2 + wc[1:2] * p1 + wc[2:3] * u + bc
            g = jax.nn.gelu(uc) * v
            tail_ref[0, 0, :, lanes] = u[tm - _SUBLANES:tm]
        else:
            u3 = u.reshape(spt, rows, tn)
            r = lax.broadcasted_iota(jnp.int32, (spt, rows, tn), 1)
            p1 = jnp.where(r == 0, c1, pltpu.roll(u3, 1, 1))
            p2 = jnp.where(r == 0, c0, jnp.where(r == 1, c1, pltpu.roll(u3, 2, 1)))
            uc = wc[0:1][None] * p2 + wc[1:2][None] * p1 + wc[2:3][None] * u3 + bc[None]
            g = (jax.nn.gelu(uc) * v.reshape(spt, rows, tn)).reshape(tm, tn)
            tail_ref[0, :, :, lanes] = u3[:, rows - _SUBLANES:rows, :]
        g_ref[:, lanes] = g.astype(g_ref.dtype)
        return jnp.max(g.reshape(tm // _SUBLANES, _SUBLANES, tn), axis=0)

    def epilogue(k):
        token = None
        for c in range(k):
            part = epilogue_tile(slice(c * tn, (c + 1) * tn))
            token = part if token is None else jnp.maximum(token, part)
        return token

    @pl.when(j == 0)
    def _first():
        if tps > 1:
            @pl.when(i == 0)
            def _init():
                carry_ref[...] = jnp.zeros_like(carry_ref)
        matmul(tiles)

    @pl.when((j > 0) & (j < n_full))
    def _steady():
        token = epilogue(tiles)
        matmul(tiles, None if emit else token)

    if rem:
        @pl.when(j == n_full)
        def _short():
            token = epilogue(tiles)
            matmul(rem, None if emit else token)

    @pl.when(j == n_full + (1 if rem else 0))
    def _drain():
        epilogue(rem if rem else tiles)


def _up_conv_geglu(h, w_u, w_v, w_conv, b_conv, conv_state, t):
    m, d = h.shape
    f = w_conv.shape[1]
    (wu_arr, wu_tile0), (wv_arr, wv_tile0) = w_u, w_v
    emit = wu_arr.dtype == _F32
    tm = _row_tile(m, t, _TM)
    tn = _TN_FF
    assert f % tn == 0, (f, tn)
    assert not emit or m == tm, "the bf16 copies are written by a single row tile"
    nf = f // tn
    tiles = 1 if emit else min(_UP_TILES, nf)
    n_groups = -(-nf // tiles)
    w = tiles * tn
    spt, tps = _seq_block(tm, t)
    rows = tm // spt
    assert rows >= _SUBLANES and conv_state.shape[1] == CONV_W - 1
    cur = lambda j: jnp.minimum(j, n_groups - 1)
    prev = lambda j: jnp.maximum(j - 1, 0)
    tile = lambda j, c: jnp.minimum(tiles * cur(j) + c, nf - 1)
    est = 2 * (_nbytes((tm, d), _BF16) + 2 * tiles * _nbytes((d, tn), wu_arr.dtype) + _nbytes((tm, w), _BF16)
               + _nbytes((spt, _SUBLANES, w), _F32)) + _nbytes((n_groups, _SUBLANES, w), _F32) + (
        2 * _nbytes((tm, w), _F32) + 10 * _nbytes((tm, tn), _F32))
    out_specs = [pl.BlockSpec((tm, w), lambda i, j: (i, prev(j))),
                 pl.BlockSpec((1, spt, _SUBLANES, w), lambda i, j: (i, 0, 0, prev(j)))]
    out_shape = [jax.ShapeDtypeStruct((m, f), _BF16),
                 jax.ShapeDtypeStruct((m // tm, spt, _SUBLANES, f), _F32)]
    if emit:
        for _ in range(2):
            copy_shape, copy_spec = _w_copy(d, f, tn, lambda i, j: cur(j))
            out_shape.append(copy_shape)
            out_specs.append(copy_spec)
        est += 4 * _nbytes((d, tn), _BF16)
    w_specs = [_w_spec(arr, d, tn, lambda i, j, c=c, t0=t0: t0 + tile(j, c))
               for arr, t0 in ((wu_arr, wu_tile0), (wv_arr, wv_tile0)) for c in range(tiles)]
    return pl.pallas_call(
        functools.partial(_up_kernel, tps=tps, nf=nf, tiles=tiles, emit=emit),
        grid=(m // tm, n_groups + 1),
        in_specs=[pl.BlockSpec((tm, d), lambda i, j: (i, 0)), *w_specs,
                  pl.BlockSpec((CONV_W, w), lambda i, j: (0, prev(j))),
                  pl.BlockSpec((1, w), lambda i, j: (0, prev(j))),
                  pl.BlockSpec((spt, CONV_W - 1, w), lambda i, j: (i // tps, 0, prev(j)))],
        out_specs=out_specs,
        out_shape=out_shape,
        scratch_shapes=[pltpu.VMEM((tm, w), _F32), pltpu.VMEM((tm, w), _F32),
                        pltpu.VMEM((n_groups, _SUBLANES, w), _F32)],
        compiler_params=_cparams(("arbitrary", "arbitrary"), est),
        name="up_conv_geglu_emit" if emit else "up_conv_geglu",
    )(h, *([wu_arr] * tiles), *([wv_arr] * tiles), w_conv, b_conv.reshape(1, f), conv_state)


def _layer(x, mod, pos0, ret_state, swa_kc, swa_vc, conv_state, w):
    b, t, d = x.shape
    m = b * t
    x2 = x.reshape(m, d)
    sh1, sc1, gt1, sh2, sc2, gt2 = [v.reshape(b, 1, d) for v in jnp.split(mod, 6, axis=-1)]
    bf = {}

    h = _norm_mod(x2, w["g_mix"], sc1, sh1, t)
    bq, fo, *w_bf = _in_proj(h, w["w_in"], t, pos0)
    bf.update(zip(("w_in",), w_bf))
    a_ret, s_new = _retention(bq, fo, ret_state, b, t)
    if swa_kc is None:
        o_swa = _swa_prompt(bq, fo, w["swa_sinks"], b, t)
        keep = min(WINDOW, t)
    else:
        o_swa = _swa_cache(bq, fo, swa_kc, swa_vc, w["swa_sinks"], b, t)
        keep = t
    merged, *w_bf = _merge(a_ret, o_swa, fo, w["w_ret_o"], w["w_swa_o"])
    bf.update(zip(("w_ret_o", "w_swa_o"), w_bf))
    x1, *w_bf = _proj_residual(merged, w["w_out"], x2, gt1, t, _TM, 2, "out_proj")
    bf.update(zip(("w_out",), w_bf))

    h2 = _norm_mod(x1, w["g_ffn"], sc2, sh2, t)
    f = w["b_conv"].shape[0]
    cs = jnp.zeros((b, CONV_W - 1, f), _F32) if conv_state is None else conv_state
    if "w_up_u" in w:
        w_u, w_v = (w["w_up_u"], 0), (w["w_up_v"], 0)
    else:
        w_u, w_v = (w["w_up"], 0), (w["w_up"], f // _TN_FF)
    g, tails, *w_bf = _up_conv_geglu(h2, w_u, w_v, w["w_conv"], w["b_conv"], cs, t)
    bf.update(zip(("w_up_u", "w_up_v"), w_bf))
    x3, = _proj_residual(g, w["w_down"], x1, gt2, t, _TM_DOWN, 1, "down_proj")
    y = _final_norm(x3, w["g_final"])

    skv = fo.reshape(b, t, fo.shape[1])[:, t - keep:, fo.shape[1] - 2 * SWA_KV:]
    k_new = skv[:, :, :SWA_KV].reshape(b, keep, SWA_KV_HEADS, SWA_HD)
    v_new = skv[:, :, SWA_KV:].reshape(b, keep, SWA_KV_HEADS, SWA_HD)
    segs = tails.shape[0] * tails.shape[1] // b
    tails = tails.reshape(b, segs, _SUBLANES, f)
    conv_new = tails[:, segs - 1, _SUBLANES - (CONV_W - 1):, :]
    return y.reshape(b, t, d), s_new, k_new, v_new, conv_new, bf


def kernel(x_prompt, x_sample, cache_swa_k, cache_swa_v, state_ret, state_conv, c_prompt, c_sample,
           g_mix, g_ffn, w_ada, b_ada, w_in, swa_sinks, w_ret_o, w_swa_o, w_out,
           w_up, w_conv, b_conv, w_down, g_final):
    depth = g_mix.shape[0]
    assert depth == 1, "single-layer stack"
    bp = x_prompt.shape[0]
    bs = x_sample.shape[0]
    l = 0

    c_all = jnp.concatenate([c_prompt, c_sample], axis=0)
    rows = -(-c_all.shape[0] // 16) * 16
    c_all = jnp.pad(c_all, ((0, rows - c_all.shape[0]), (0, 0)))
    mod = _ada(c_all, w_ada[l], b_ada[l])

    w = {
        "g_mix": g_mix[l], "g_ffn": g_ffn[l], "g_final": g_final, "swa_sinks": swa_sinks[l],
        "w_in": w_in[l], "w_ret_o": w_ret_o[l], "w_swa_o": w_swa_o[l], "w_out": w_out[l], "w_up": w_up[l],
        "w_down": w_down[l].astype(_BF16),
        "w_conv": w_conv[l], "b_conv": b_conv[l],
    }

    ys, ss, ks, vs, cs, w_bf = _layer(x_sample, mod[bp:bp + bs], PAST_LEN, state_ret[l], cache_swa_k[l],
                                      cache_swa_v[l], state_conv[l], w)
    yp, sp, kp, vp, cp, _ = _layer(x_prompt, mod[:bp], 0, None, None, None, None, {**w, **w_bf})
    stack = lambda a: a[None]
    return (yp, ys, stack(sp), stack(kp), stack(vp), stack(cp), stack(ss), stack(ks), stack(vs), stack(cs))
```

```python
import functools

import jax
import jax.numpy as jnp
import numpy as np
from jax import lax
from jax.experimental import pallas as pl
from jax.experimental.pallas import tpu as pltpu

_F32 = jnp.float32
_BF16 = jnp.bfloat16

CHUNK = 64
RET_HEADS = 8
RET_DK = 256
RET_DV = 256
RET_QK = RET_HEADS * RET_DK
RET_V = RET_HEADS * RET_DV
ROPE_BASE = 10000.0
SWA_HEADS = 32
SWA_KV_HEADS = 4
SWA_HD = 64
SWA_GROUP = SWA_HEADS // SWA_KV_HEADS
SWA_PAIRS = SWA_GROUP // 2
SWA_Q = SWA_HEADS * SWA_HD
SWA_KV = SWA_KV_HEADS * SWA_HD
WINDOW = 128
WINDOW_CHUNKS = WINDOW // CHUNK
CONV_W = 3
EPS = 1e-6
NEG_INF = -1e30
PAST_LEN = 2048

_V7X_VMEM_BYTES = 64 * 1024 * 1024
_LANES = 128
_SUBLANES = 8

_TM = 1024
_TM_DOWN = 512
_TM_NORM = 512
_TN = 2 * SWA_KV
_TN_EMIT = 512
_TN_ADA = 512
_RET_CHUNK = 256
_SWA_QB = 2
_TN_FF = 256
_UP_TILES = 2
_SIDE_ROWS = 128


def _cparams(sem, vmem_estimate, flags=None):
    limit = min(int(vmem_estimate * 1.15) + (4 << 20), _V7X_VMEM_BYTES - (6 << 20))
    return pltpu.CompilerParams(dimension_semantics=sem, vmem_limit_bytes=limit, flags=flags)


def _nbytes(shape, dtype):
    return int(np.prod(shape)) * jnp.dtype(dtype).itemsize


def _row_tile(m, t, cap):
    tm = min(cap, m)
    assert m % tm == 0 and (t % tm == 0 or tm % t == 0), (m, t, tm)
    return tm


def _seq_block(tm, t):
    return max(1, tm // t), max(1, t // tm)


def _mod_spec(tm, t, tn, col):
    spt, tps = _seq_block(tm, t)
    return pl.BlockSpec((spt, 1, tn), lambda i, j: (i // tps, 0, col(i, j)))


def _per_seq(val, mod, fn):
    spt = mod.shape[0]
    if spt == 1:
        return fn(val, mod[0])
    tm, n = val.shape
    return fn(val.reshape(spt, tm // spt, n), mod).reshape(tm, n)


def _ada_kernel(c_ref, w_ref, b_ref, o_ref):
    c = c_ref[...]
    a = (c * jax.nn.sigmoid(c)).astype(_BF16)
    o_ref[...] = jnp.dot(a, w_ref[...].astype(_BF16), preferred_element_type=_F32) + b_ref[...]


def _ada(c, w_ada, b_ada):
    rows, d = c.shape
    n = w_ada.shape[1]
    tn = _TN_ADA
    est = 2 * (_nbytes((rows, d), _F32) + _nbytes((d, tn), _F32) + _nbytes((rows, tn), _F32)) + _nbytes((d, tn), _BF16)
    return pl.pallas_call(
        _ada_kernel,
        grid=(n // tn,),
        in_specs=[pl.BlockSpec((rows, d), lambda j: (0, 0)),
                  pl.BlockSpec((d, tn), lambda j: (0, j)),
                  pl.BlockSpec((1, tn), lambda j: (0, j))],
        out_specs=pl.BlockSpec((rows, tn), lambda j: (0, j)),
        out_shape=jax.ShapeDtypeStruct((rows, n), _F32),
        compiler_params=_cparams(("arbitrary",), est),
        name="ada_mod",
    )(c, w_ada, b_ada.reshape(1, n))


def _norm_rows(m):
    return _TM_NORM if m >= 8 * _TM_NORM else _TM_NORM // 2


def _rms(x, g):
    y = x * lax.rsqrt(jnp.mean(x * x, axis=-1, keepdims=True) + EPS)
    return y * g


def _norm_mod_kernel(x_ref, g_ref, sc_ref, sh_ref, o_ref):
    y = _rms(x_ref[...], g_ref[...])
    h = _per_seq(y, sc_ref[...], lambda v, sc: v * (1.0 + sc))
    h = _per_seq(h, sh_ref[...], lambda v, sh: v + sh)
    o_ref[...] = h.astype(o_ref.dtype)


def _norm_mod(x, g, sc, sh, t):
    m, d = x.shape
    tm = _row_tile(m, t, _norm_rows(m))
    est = 2 * (_nbytes((tm, d), _F32) + _nbytes((tm, d), _BF16)) + 3 * _nbytes((tm, d), _F32)
    return pl.pallas_call(
        _norm_mod_kernel,
        grid=(m // tm, 1),
        in_specs=[pl.BlockSpec((tm, d), lambda i, j: (i, 0)),
                  pl.BlockSpec((1, d), lambda i, j: (0, 0)),
                  _mod_spec(tm, t, d, lambda i, j: 0),
                  _mod_spec(tm, t, d, lambda i, j: 0)],
        out_specs=pl.BlockSpec((tm, d), lambda i, j: (i, 0)),
        out_shape=jax.ShapeDtypeStruct((m, d), _BF16),
        compiler_params=_cparams(("arbitrary", "arbitrary"), est),
        name="norm_mod",
    )(x, g.reshape(1, d), sc, sh)


def _final_norm_kernel(x_ref, g_ref, o_ref):
    o_ref[...] = _rms(x_ref[...], g_ref[...])


def _final_norm(x, g):
    m, d = x.shape
    tm = min(_norm_rows(m), m)
    est = 4 * _nbytes((tm, d), _F32) + 2 * _nbytes((tm, d), _F32)
    return pl.pallas_call(
        _final_norm_kernel,
        grid=(m // tm,),
        in_specs=[pl.BlockSpec((tm, d), lambda i: (i, 0)),
                  pl.BlockSpec((1, d), lambda i: (0, 0))],
        out_specs=pl.BlockSpec((tm, d), lambda i: (i, 0)),
        out_shape=jax.ShapeDtypeStruct((m, d), _F32),
        compiler_params=_cparams(("arbitrary",), est),
        name="final_norm",
    )(x, g.reshape(1, d))


def _pair_slabs(slab, parity, fill):
    low = lax.broadcasted_iota(jnp.int32, slab.shape, 1) < SWA_HD
    if parity == 0:
        even = jnp.where(low, slab, 0.0)
        odd = pltpu.roll(even, SWA_HD, 1)
    else:
        odd = jnp.where(low, 0.0, slab)
        even = pltpu.roll(odd, SWA_HD, 1)
    if fill == 0.0:
        return even, odd
    return jnp.where(low, even, fill), jnp.where(low, fill, odd)


class _InPlan:
    def __init__(self, tn, pair, d):
        w = tn * pair
        n_skv = 2 * SWA_KV // tn
        assert RET_QK % w == 0 and SWA_Q % w == 0 and (2 * d) % w == 0 and (RET_V + 2 * d) % w == 0
        assert tn % RET_DK == 0 and (2 * SWA_KV) % tn == 0 and (pair == 1 or n_skv == 1)
        self.tn, self.pair, self.w = tn, pair, w
        r = self.r = RET_QK // w
        self.t_skv = 5 * r
        self.t_gab = self.t_skv + n_skv
        self.steps = self.t_gab + 2 * d // w
        self.skv_block = (RET_V + 2 * d) // w
        self.fo_cols = RET_V + 2 * d + 2 * SWA_KV

    def w_tile(self, t):
        p, ts, tg = self.pair, self.t_skv, self.t_gab
        return jnp.where(t < ts, p * t, jnp.where(t < tg, p * ts + (t - ts), p * ts + (tg - ts) + p * (t - tg)))

    def bq_block(self, t):
        r = self.r
        return jnp.where(t < 3 * r, t, jnp.where(t < 4 * r, 3 * r - 1, jnp.where(t < 5 * r, t - r, 4 * r - 1)))

    def fo_block(self, t):
        r, ts, tg = self.r, self.t_skv, self.t_gab
        return jnp.where(t < 3 * r, 0, jnp.where(t < 4 * r, t - 3 * r, jnp.where(
            t < ts, r - 1, jnp.where(t < tg, self.skv_block + (t - ts), r + (t - tg)))))


def _in_proj_kernel(*refs, plan, emit, side_cast):
    n_w = plan.pair
    h_ref, w_refs, (cos_ref, sin_ref) = refs[0], refs[1:1 + n_w], refs[1 + n_w:3 + n_w]
    rest = list(refs[3 + n_w:])
    side_ref = rest.pop(0) if side_cast else None
    bq_ref, fo_ref = rest[:2]
    if emit:
        wbf_ref = rest[2]
        _bf16_weight(w_refs[0], wbf_ref)
        w_refs = (wbf_ref,)
    side_out_ref = rest[-1] if side_cast else None
    t = pl.program_id(1)
    r, tn = plan.r, plan.tn
    half = RET_DK // 2

    def tiles():
        if side_cast:
            side_out_ref[...] = side_ref[...].astype(_BF16)
        for n, w_ref in enumerate(w_refs):
            yield jnp.dot(h_ref[...], _bf16_weight(w_ref, None), preferred_element_type=_F32), n * tn

    @pl.when(t < 2 * r)
    def _rope():
        c = cos_ref[...]
        s = sin_ref[...]
        scale = jnp.where(t >= r, RET_DK ** -0.5, 1.0).astype(_F32)
        for acc, col in tiles():
            for hh in range(tn // RET_DK):
                lo = hh * RET_DK
                x1 = acc[:, lo:lo + half]
                x2 = acc[:, lo + half:lo + RET_DK]
                bq_ref[:, col + lo:col + lo + half] = ((x1 * c - x2 * s) * scale).astype(_BF16)
                bq_ref[:, col + lo + half:col + lo + RET_DK] = ((x2 * c + x1 * s) * scale).astype(_BF16)

    @pl.when(((t >= 2 * r) & (t < 3 * r)) | ((t >= 4 * r) & (t < 5 * r)))
    def _bf16():
        for acc, col in tiles():
            bq_ref[:, col:col + tn] = acc.astype(_BF16)

    @pl.when(((t >= 3 * r) & (t < 4 * r)) | (t >= plan.t_gab))
    def _f32():
        for acc, col in tiles():
            fo_ref[:, col:col + tn] = acc

    @pl.when((t >= plan.t_skv) & (t < plan.t_gab))
    def _kv():
        if side_cast:
            side_out_ref[...] = side_ref[...].astype(_BF16)
        fo_ref[:, 0:tn] = jnp.dot(h_ref[...], _bf16_weight(w_refs[0], None), preferred_element_type=_F32)


def _rope_tables(t, pos0, tm):
    half = RET_DK // 2
    inv = ROPE_BASE ** (-jnp.arange(half, dtype=_F32) * (2.0 / RET_DK))
    pos = (pos0 + jnp.arange(t, dtype=jnp.int32)).astype(_F32)
    ang = pos[:, None] * inv[None, :]
    reps = max(1, tm // t)
    return jnp.tile(jnp.cos(ang), (reps, 1)), jnp.tile(jnp.sin(ang), (reps, 1))


def _in_proj(h, w_in, t, pos0, side_cast=None):
    m, d = h.shape
    n_in = _w_cols(w_in)
    emit = w_in.dtype == _F32
    tm = _row_tile(m, t, _TM)
    plan = _InPlan(_TN_EMIT, 1, d) if emit else _InPlan(_TN, 2, d)
    tn, w = plan.tn, plan.w
    assert n_in == plan.fo_cols + 4 * RET_QK, (n_in, d)
    assert not emit or m == tm, "the bf16 copy is written by a single row tile"
    cos, sin = _rope_tables(t, pos0, tm)
    tab_blocks = cos.shape[0] // tm
    half = RET_DK // 2
    last_tile = n_in // tn - 1

    out_shape = [jax.ShapeDtypeStruct((m, 4 * RET_QK), _BF16), jax.ShapeDtypeStruct((m, plan.fo_cols), _F32)]
    out_specs = [pl.BlockSpec((tm, w), lambda i, j: (i, plan.bq_block(j))),
                 pl.BlockSpec((tm, w), lambda i, j: (i, plan.fo_block(j)))]
    w_specs = [_w_spec(w_in, d, tn, lambda i, j, n=n: jnp.minimum(plan.w_tile(j) + n, last_tile))
               for n in range(plan.pair)]
    est = 2 * (_nbytes((tm, d), _BF16) + plan.pair * _nbytes((d, tn), w_in.dtype) + 2 * _nbytes((tm, half), _F32)
               + _nbytes((tm, w), _BF16) + _nbytes((tm, w), _F32)) + 2 * plan.pair * _nbytes((tm, tn), _F32)
    if emit:
        copy_shape, copy_spec = _w_copy(d, n_in, tn, lambda i, j: plan.w_tile(j))
        out_shape.append(copy_shape)
        out_specs.append(copy_spec)
        est += 2 * _nbytes((d, tn), _BF16)
    side_args, side_specs = [], []
    if side_cast is not None:
        rows, cols = side_cast.shape
        n_blocks = rows // _SIDE_ROWS
        assert not emit and rows % _SIDE_ROWS == 0 and n_blocks <= (m // tm) * plan.steps
        side_spec = pl.BlockSpec((_SIDE_ROWS, cols), lambda i, j: (jnp.minimum(i * plan.steps + j, n_blocks - 1), 0))
        side_args, side_specs = [side_cast], [side_spec]
        out_shape.append(jax.ShapeDtypeStruct((rows, cols), _BF16))
        out_specs.append(side_spec)
        est += 2 * (_nbytes((_SIDE_ROWS, cols), _F32) + _nbytes((_SIDE_ROWS, cols), _BF16))
    return pl.pallas_call(
        functools.partial(_in_proj_kernel, plan=plan, emit=emit, side_cast=side_cast is not None),
        grid=(m // tm, plan.steps),
        in_specs=[pl.BlockSpec((tm, d), lambda i, j: (i, 0)), *w_specs,
                  pl.BlockSpec((tm, half), lambda i, j: (i % tab_blocks, 0)),
                  pl.BlockSpec((tm, half), lambda i, j: (i % tab_blocks, 0)), *side_specs],
        out_specs=out_specs,
        out_shape=out_shape,
        compiler_params=_cparams(("arbitrary", "arbitrary"), est),
        name="in_proj_emit" if emit else "in_proj",
    )(h, *([w_in] * plan.pair), cos, sin, *side_args)


def _retention_kernel(*refs, hb, cc, n_chunks, has_state):
    if has_state:
        lg_ref, q_ref, k_ref, v_ref, rg_ref, s0_ref, a_ref, sout_ref, s_scr = refs
    else:
        lg_ref, q_ref, k_ref, v_ref, rg_ref, a_ref, sout_ref, s_scr = refs
    row = lax.broadcasted_iota(jnp.int32, (cc, cc), 0)
    col = lax.broadcasted_iota(jnp.int32, (cc, cc), 1)
    diff = (row - col).astype(_F32)
    idx = lax.broadcasted_iota(jnp.int32, (cc, 1), 0).astype(_F32)
    for hh in range(hb):
        lg = lg_ref[hh][:, :1]
        dmask = jnp.where(diff >= 0, jnp.exp(lg * jnp.maximum(diff, 0.0)), 0.0)
        dq = jnp.exp(lg * (idx + 1.0))
        dk = jnp.exp(lg * (cc - 1.0 - idx))
        ds = jnp.exp(lg * float(cc))
        cols = slice(hh * RET_DK, (hh + 1) * RET_DK)
        if has_state:
            s_scr[...] = s0_ref[0, hh]
        else:
            s_scr[...] = jnp.zeros_like(s_scr)

        def body(c, carry):
            r = pl.ds(pl.multiple_of(c * cc, cc), cc)
            q = q_ref[r, cols]
            k = k_ref[r, cols]
            v = v_ref[r, cols]
            state = s_scr[...]
            scores = lax.dot_general(q, k, (((1,), (1,)), ((), ())), preferred_element_type=_F32) * dmask
            y = jnp.dot(scores.astype(_BF16), v, preferred_element_type=_F32)
            y = y + jnp.dot(q, state.astype(_BF16), preferred_element_type=_F32) * dq
            k_dec = (k.astype(_F32) * dk).astype(_BF16)
            s_scr[...] = state * ds + lax.dot_general(
                k_dec, v, (((0,), (0,)), ((), ())), preferred_element_type=_F32)
            mu = jnp.mean(y, axis=-1, keepdims=True)
            dev = y - mu
            var = jnp.mean(dev * dev, axis=-1, keepdims=True)
            yn = dev * lax.rsqrt(var + EPS)
            g = rg_ref[r, cols]
            a_ref[r, cols] = (g * jax.nn.sigmoid(g) * yn).astype(_BF16)
            return carry

        lax.fori_loop(0, n_chunks, body, 0, unroll=2 if n_chunks % 2 == 0 else 1)
        sout_ref[0, hh] = s_scr[...]


def _retention(qkv, rg, state, b, t):
    m = qkv.shape[0]
    cc = min(_RET_CHUNK, t)
    assert t % cc == 0 and cc % min(CHUNK, t) == 0
    hb = RET_HEADS if t * RET_QK * 2 <= (1 << 20) else 1
    nhb = RET_HEADS // hb
    wcols = hb * RET_DK
    h = jnp.arange(RET_HEADS, dtype=_F32)
    lg = jnp.log1p(-jnp.exp2(-5.0 - h))
    lg = jnp.broadcast_to(lg[:, None, None], (RET_HEADS, 1, _LANES))
    has_state = state is not None
    in_specs = [pl.BlockSpec((hb, 1, _LANES), lambda bi, hi: (hi, 0, 0)),
                pl.BlockSpec((t, wcols), lambda bi, hi: (bi, hi)),
                pl.BlockSpec((t, wcols), lambda bi, hi: (bi, nhb + hi)),
                pl.BlockSpec((t, wcols), lambda bi, hi: (bi, 2 * nhb + hi)),
                pl.BlockSpec((t, wcols), lambda bi, hi: (bi, hi))]
    args = [lg, qkv, qkv, qkv, rg]
    s_spec = pl.BlockSpec((1, hb, RET_DK, RET_DV), lambda bi, hi: (bi, hi, 0, 0))
    if has_state:
        in_specs.append(s_spec)
        args.append(state)
    est = 2 * (4 * _nbytes((t, wcols), _BF16) + _nbytes((t, wcols), _F32)
               + 2 * _nbytes((hb, RET_DK, RET_DV), _F32)) + 8 * _nbytes((cc, max(cc, RET_DV)), _F32)
    kern = functools.partial(_retention_kernel, hb=hb, cc=cc, n_chunks=t // cc, has_state=has_state)
    return pl.pallas_call(
        kern,
        grid=(b, nhb),
        in_specs=in_specs,
        out_specs=(pl.BlockSpec((t, wcols), lambda bi, hi: (bi, hi)), s_spec),
        out_shape=(jax.ShapeDtypeStruct((m, RET_V), _BF16),
                   jax.ShapeDtypeStruct((b, RET_HEADS, RET_DK, RET_DV), _F32)),
        scratch_shapes=[pltpu.VMEM((RET_DK, RET_DV), _F32)],
        compiler_params=_cparams(("arbitrary", "arbitrary"), est),
        name="retention",
    )(*args)


def _swa_core(qst, k_even, k_odd, v_even, v_odd, sink_even, sink_odd, valid):
    def weights(kx, sink):
        s = lax.dot_general(qst, kx, (((1,), (1,)), ((), ())), preferred_element_type=_F32) * (SWA_HD ** -0.5)
        if valid is not None:
            s = jnp.where(valid, s, NEG_INF)
        mx = jnp.maximum(jnp.max(s, axis=-1, keepdims=True), sink)
        return jnp.exp(s - mx).astype(_BF16), jnp.exp(sink - mx)

    e_even, t_even = weights(k_even, sink_even)
    e_odd, t_odd = weights(k_odd, sink_odd)
    o_even = jnp.dot(e_even, v_even, preferred_element_type=_F32)
    o_odd = jnp.dot(e_odd, v_odd, preferred_element_type=_F32)
    low = lax.broadcasted_iota(jnp.int32, o_even.shape, 1) < SWA_HD
    num = jnp.where(low, o_even, o_odd)
    den = jnp.where(low, pltpu.roll(o_even, SWA_HD, 1) + t_even, pltpu.roll(o_odd, SWA_HD, 1) + t_odd)
    return num / den


def _swa_prompt_kernel(q_ref, k_ref, v_ref, sink_ref, o_ref, slab_ref, *, n_blocks, unroll):
    gw = SWA_GROUP * SWA_HD
    rows = _SWA_QB * CHUNK
    win = _SWA_QB + WINDOW_CHUNKS

    for par in range(2):
        slabs = _pair_slabs(k_ref[...], par, fill=0.0) + _pair_slabs(v_ref[...], par, fill=1.0)
        for c, x in enumerate(slabs):
            slab_ref[par, c] = x.astype(_BF16)

    def valid_mask(n_key_chunks, first):
        shape = (SWA_PAIRS * rows, n_key_chunks * CHUNK)
        qc = (lax.broadcasted_iota(jnp.int32, shape, 0) // CHUNK) % _SWA_QB
        kc = lax.broadcasted_iota(jnp.int32, shape, 1) // CHUNK
        if first:
            return kc <= qc
        return (kc >= qc) & (kc <= qc + WINDOW_CHUNKS)

    valid_first = valid_mask(_SWA_QB, True)
    valid = valid_mask(win, False)
    for par in range(2):
        sink_even = sink_ref[par, 0]
        sink_odd = sink_ref[par, 1]
        lane0 = par * gw

        def block(r, kr, mask):
            qst = jnp.concatenate(
                [q_ref[r, lane0 + p * _LANES:lane0 + (p + 1) * _LANES] for p in range(SWA_PAIRS)], axis=0)
            o = _swa_core(qst, slab_ref[par, 0, kr, :], slab_ref[par, 1, kr, :],
                          slab_ref[par, 2, kr, :], slab_ref[par, 3, kr, :], sink_even, sink_odd, mask)
            for p in range(SWA_PAIRS):
                o_ref[r, lane0 + p * _LANES:lane0 + (p + 1) * _LANES] = (
                    o[p * rows:(p + 1) * rows].astype(o_ref.dtype))

        block(pl.ds(0, rows), pl.ds(0, rows), valid_first)
        if n_blocks > 1:
            def body(b, carry):
                r = pl.ds(pl.multiple_of(b * rows, rows), rows)
                kr = pl.ds(pl.multiple_of(b * rows - WINDOW_CHUNKS * CHUNK, CHUNK), win * CHUNK)
                block(r, kr, valid)
                return carry

            lax.fori_loop(1, n_blocks, body, 0, unroll=unroll)


def _sink_columns(sinks, rows):
    s = sinks.astype(_F32).reshape(SWA_KV_HEADS, SWA_PAIRS, 2).transpose(0, 2, 1)
    return jnp.repeat(s, rows, axis=-1)[..., None]


def _swa_prompt(bq, fo, sinks, b, t):
    m = bq.shape[0]
    gw2 = 2 * SWA_GROUP * SWA_HD
    rows = _SWA_QB * CHUNK
    assert t % rows == 0 and WINDOW_CHUNKS == _SWA_QB
    n_blocks = t // rows
    unroll = next(u for u in (5, 3, 2, 1) if (n_blocks - 1) % u == 0)
    sink_cols = _sink_columns(sinks, rows)
    q0 = 3 * RET_QK // gw2
    k0 = (fo.shape[1] - 2 * SWA_KV) // _LANES
    v0 = k0 + SWA_KV // _LANES
    est = (2 * (2 * _nbytes((t, gw2), _BF16) + 2 * _nbytes((t, _LANES), _F32))
           + _nbytes((2, 4, t, _LANES), _BF16) + (8 << 20))
    return pl.pallas_call(
        functools.partial(_swa_prompt_kernel, n_blocks=n_blocks, unroll=unroll),
        grid=(b, SWA_KV_HEADS // 2),
        in_specs=[pl.BlockSpec((t, gw2), lambda bi, kp: (bi, q0 + kp)),
                  pl.BlockSpec((t, _LANES), lambda bi, kp: (bi, k0 + kp)),
                  pl.BlockSpec((t, _LANES), lambda bi, kp: (bi, v0 + kp)),
                  pl.BlockSpec((2, 2, SWA_PAIRS * rows, 1), lambda bi, kp: (kp, 0, 0, 0))],
        out_specs=pl.BlockSpec((t, gw2), lambda bi, kp: (bi, kp)),
        out_shape=jax.ShapeDtypeStruct((m, SWA_Q), _BF16),
        scratch_shapes=[pltpu.VMEM((2, 4, t, _LANES), _BF16)],
        compiler_params=_cparams(("arbitrary", "arbitrary"), est),
        name="swa_prompt",
    )(bq, fo, fo, sink_cols)


def _swa_cache_kernel(q_ref, kvn_ref, kc_ref, vc_ref, sink_ref, o_ref, *, L):
    gw = SWA_GROUP * SWA_HD
    kvn = kvn_ref[...]
    k_all = jnp.concatenate([kc_ref[0], kvn[:, :SWA_KV]], axis=0)
    v_all = jnp.concatenate([vc_ref[0], kvn[:, SWA_KV:]], axis=0)
    for kv in range(SWA_KV_HEADS):
        lanes = slice(_LANES * (kv // 2), _LANES * (kv // 2 + 1))
        slabs = _pair_slabs(k_all[:, lanes], kv % 2, fill=0.0) + _pair_slabs(v_all[:, lanes], kv % 2, fill=1.0)
        qst = jnp.concatenate(
            [q_ref[:, kv * gw + p * _LANES:kv * gw + (p + 1) * _LANES] for p in range(SWA_PAIRS)], axis=0)
        o = _swa_core(qst, *[x.astype(_BF16) for x in slabs], sink_ref[kv, 0], sink_ref[kv, 1], None)
        for p in range(SWA_PAIRS):
            lo = kv * gw + p * _LANES
            o_ref[:, lo:lo + _LANES] = o[p * L:(p + 1) * L].astype(o_ref.dtype)


def _swa_cache(bq, fo, k_cache, v_cache, sinks, b, t):
    m = bq.shape[0]
    rows = k_cache.shape[1]
    kc = k_cache.reshape(b, rows, SWA_KV)
    vc = v_cache.reshape(b, rows, SWA_KV)
    sink_cols = _sink_columns(sinks, t)
    assert (fo.shape[1] - 2 * SWA_KV) % (2 * SWA_KV) == 0
    kv0 = fo.shape[1] // (2 * SWA_KV) - 1
    est = 2 * (2 * _nbytes((t, SWA_Q), _BF16) + _nbytes((t, 2 * SWA_KV), _F32)
               + 2 * _nbytes((rows, SWA_KV), _F32)) + (4 << 20)
    return pl.pallas_call(
        functools.partial(_swa_cache_kernel, L=t),
        grid=(b,),
        in_specs=[pl.BlockSpec((t, SWA_Q), lambda bi: (bi, 3 * RET_QK // SWA_Q)),
                  pl.BlockSpec((t, 2 * SWA_KV), lambda bi: (bi, kv0)),
                  pl.BlockSpec((1, rows, SWA_KV), lambda bi: (bi, 0, 0)),
                  pl.BlockSpec((1, rows, SWA_KV), lambda bi: (bi, 0, 0)),
                  pl.BlockSpec((SWA_KV_HEADS, 2, SWA_PAIRS * t, 1), lambda bi: (0, 0, 0, 0))],
        out_specs=pl.BlockSpec((t, SWA_Q), lambda bi: (bi, 0)),
        out_shape=jax.ShapeDtypeStruct((m, SWA_Q), _BF16),
        compiler_params=_cparams(("arbitrary",), est),
        name="swa_cache",
    )(bq, fo, kc, vc, sink_cols)


def _bf16_weight(w_ref, wbf_ref):
    if wbf_ref is None:
        return w_ref[...]
    wbf_ref[...] = w_ref[...].astype(_BF16)
    return wbf_ref[...]


def _w_cols(w):
    return w.shape[1]


def _w_spec(w, k, tn, tile):
    assert w.shape[0] == k
    return pl.BlockSpec((k, tn), lambda i, j: (0, tile(i, j)))


def _w_copy(k, n, tn, tile):
    return jax.ShapeDtypeStruct((k, n), _BF16), pl.BlockSpec((k, tn), lambda i, j: (0, tile(i, j)))


def _merge_kernel(a_ref, o_ref, wr_ref, ws_ref, ga_ref, gb_ref, out_ref, wrb_ref=None, wsb_ref=None):
    yr = jnp.dot(a_ref[...], _bf16_weight(wr_ref, wrb_ref), preferred_element_type=_F32)
    ys = jnp.dot(o_ref[...], _bf16_weight(ws_ref, wsb_ref), preferred_element_type=_F32)
    out_ref[...] = (jax.nn.sigmoid(ga_ref[...]) * yr + jax.nn.sigmoid(gb_ref[...]) * ys).astype(out_ref.dtype)


def _merge(a_ret, o_swa, fo, w_ret_o, w_swa_o):
    m = a_ret.shape[0]
    d = _w_cols(w_ret_o)
    emit = w_ret_o.dtype == _F32
    tm = min(_TM, m)
    tn = _TN
    nd = d // tn
    g0 = RET_V // tn
    col = lambda i, j: j
    assert not emit or m == tm, "the bf16 copies are written by a single row tile"
    est = 2 * (_nbytes((tm, RET_V), _BF16) + _nbytes((tm, SWA_Q), _BF16) + _nbytes((RET_V, tn), w_ret_o.dtype)
               + _nbytes((SWA_Q, tn), w_swa_o.dtype) + 2 * _nbytes((tm, tn), _F32) + _nbytes((tm, tn), _BF16)
               ) + 4 * _nbytes((tm, tn), _F32)
    out_specs = [pl.BlockSpec((tm, tn), lambda i, j: (i, j))]
    out_shape = [jax.ShapeDtypeStruct((m, d), _BF16)]
    if emit:
        for k in (RET_V, SWA_Q):
            copy_shape, copy_spec = _w_copy(k, d, tn, col)
            out_shape.append(copy_shape)
            out_specs.append(copy_spec)
        est += 2 * (_nbytes((RET_V, tn), _BF16) + _nbytes((SWA_Q, tn), _BF16))
    return pl.pallas_call(
        _merge_kernel,
        grid=(m // tm, nd),
        in_specs=[pl.BlockSpec((tm, RET_V), lambda i, j: (i, 0)),
                  pl.BlockSpec((tm, SWA_Q), lambda i, j: (i, 0)),
                  _w_spec(w_ret_o, RET_V, tn, col),
                  _w_spec(w_swa_o, SWA_Q, tn, col),
                  pl.BlockSpec((tm, tn), lambda i, j: (i, g0 + j)),
                  pl.BlockSpec((tm, tn), lambda i, j: (i, g0 + nd + j))],
        out_specs=out_specs,
        out_shape=out_shape,
        compiler_params=_cparams(("arbitrary", "arbitrary"), est),
        name="merge_emit" if emit else "merge",
    )(a_ret, o_swa, w_ret_o, w_swa_o, fo, fo)


def _proj_residual_kernel(*refs, tiles):
    a_ref, w_refs, (x_ref, gt_ref, o_ref) = refs[0], refs[1:1 + tiles], refs[1 + tiles:4 + tiles]
    wbf_ref = refs[4 + tiles] if len(refs) > 4 + tiles else None
    tn = w_refs[0].shape[-1]
    for c, w_ref in enumerate(w_refs):
        lanes = slice(c * tn, (c + 1) * tn)
        f = jnp.dot(a_ref[...], _bf16_weight(w_ref, wbf_ref), preferred_element_type=_F32)
        o_ref[:, lanes] = x_ref[:, lanes] + _per_seq(f, gt_ref[:, :, lanes], lambda v, gt: gt * v)


def _proj_residual(a, w, x, gt, t, tm_cap, tiles, name):
    m, k = a.shape
    d = _w_cols(w)
    emit = w.dtype == _F32
    tm = _row_tile(m, t, tm_cap)
    tn = _TN
    tiles = 1 if emit else min(tiles, d // tn)
    wd = tiles * tn
    assert d % wd == 0
    assert not emit or m == tm, "the bf16 copy is written by a single row tile"
    est = 2 * (_nbytes((tm, k), _BF16) + tiles * _nbytes((k, tn), w.dtype) + 2 * _nbytes((tm, wd), _F32)
               ) + 2 * _nbytes((tm, tn), _F32)
    out_specs = [pl.BlockSpec((tm, wd), lambda i, j: (i, j))]
    out_shape = [jax.ShapeDtypeStruct((m, d), _F32)]
    if emit:
        copy_shape, copy_spec = _w_copy(k, d, tn, lambda i, j: j)
        out_shape.append(copy_shape)
        out_specs.append(copy_spec)
        est += 2 * _nbytes((k, tn), _BF16)
    return pl.pallas_call(
        functools.partial(_proj_residual_kernel, tiles=tiles),
        grid=(m // tm, d // wd),
        in_specs=[pl.BlockSpec((tm, k), lambda i, j: (i, 0)),
                  *[_w_spec(w, k, tn, lambda i, j, c=c: tiles * j + c) for c in range(tiles)],
                  pl.BlockSpec((tm, wd), lambda i, j: (i, j)),
                  _mod_spec(tm, t, wd, lambda i, j: j)],
        out_specs=out_specs,
        out_shape=out_shape,
        compiler_params=_cparams(("arbitrary", "arbitrary"), est),
        name=name + "_emit" if emit else name,
    )(a, *([w] * tiles), x, gt)


def _up_kernel(*refs, tps, nf, tiles, emit):
    h_ref, wu_refs, wv_refs = refs[0], refs[1:1 + tiles], refs[1 + tiles:1 + 2 * tiles]
    wc_ref, bc_ref, cs_ref, g_ref, tail_ref = refs[1 + 2 * tiles:6 + 2 * tiles]
    wub_ref, wvb_ref = refs[6 + 2 * tiles:8 + 2 * tiles] if emit else (None, None)
    u_scr, v_scr, carry_ref = refs[-3:]
    i = pl.program_id(0)
    j = pl.program_id(1)
    tm = u_scr.shape[0]
    tn = u_scr.shape[1] // tiles
    n_full, rem = divmod(nf, tiles)

    def ordered_after(x, token):
        bits = pltpu.bitcast(token, jnp.uint32)
        zero = pltpu.bitcast((bits >> 16) >> 16, _F32)
        return (x.reshape(tm // _SUBLANES, _SUBLANES, tn) + zero[None]).reshape(tm, tn)

    def matmul(k, token=None):
        h = h_ref[...]
        for c in range(k):
            lanes = slice(c * tn, (c + 1) * tn)
            u = jnp.dot(h, _bf16_weight(wu_refs[c], wub_ref), preferred_element_type=_F32)
            u_scr[:, lanes] = u if (token is None or c > 0) else ordered_after(u, token)
            v_scr[:, lanes] = jnp.dot(h, _bf16_weight(wv_refs[c], wvb_ref), preferred_element_type=_F32)

    def epilogue_tile(lanes):
        jp = j - 1
        u = u_scr[:, lanes]
        v = v_scr[:, lanes]
        cs = cs_ref[:, :, lanes]
        spt = cs.shape[0]
        rows = tm // spt
        c0 = cs[:, 0:1, :]
        c1 = cs[:, 1:2, :]
        if tps > 1:
            first = (i % tps) == 0
            prev = carry_ref[jp, :, lanes]
            c0 = jnp.where(first, c0, prev[_SUBLANES - 2:_SUBLANES - 1][None])
            c1 = jnp.where(first, c1, prev[_SUBLANES - 1:_SUBLANES][None])
            carry_ref[jp, :, lanes] = u[tm - _SUBLANES:tm]
        wc = wc_ref[:, lanes]
        bc = bc_ref[:, lanes]
        if spt == 1:
            r = lax.broadcasted_iota(jnp.int32, (tm, tn), 0)
            p1 = jnp.where(r == 0, c1[0], pltpu.roll(u, 1, 0))
            p2 = jnp.where(r == 0, c0[0], jnp.where(r == 1, c1[0], pltpu.roll(u, 2, 0)))
            uc = wc[0:1] * p2 + wc[1:2] * p1 + wc[2:3] * u + bc
            g = jax.nn.gelu(uc) * v
            tail_ref[0, 0, :, lanes] = u[tm - _SUBLANES:tm]
        else:
            u3 = u.reshape(spt, rows, tn)
            r = lax.broadcasted_iota(jnp.int32, (spt, rows, tn), 1)
            p1 = jnp.where(r == 0, c1, pltpu.roll(u3, 1, 1))
            p2 = jnp.where(r == 0, c0, jnp.where(r == 1, c1, pltpu.roll(u3, 2, 1)))
            uc = wc[0:1][None] * p2 + wc[1:2][None] * p1 + wc[2:3][None] * u3 + bc[None]
            g = (jax.nn.gelu(uc) * v.reshape(spt, rows, tn)).reshape(tm, tn)
            tail_ref[0, :, :, lanes] = u3[:, rows - _SUBLANES:rows, :]
        g_ref[:, lanes] = g.astype(g_ref.dtype)
        return jnp.max(g.reshape(tm // _SUBLANES, _SUBLANES, tn), axis=0)

    def epilogue(k):
        token = None
        for c in range(k):
            part = epilogue_tile(slice(c * tn, (c + 1) * tn))
            token = part if token is None else jnp.maximum(token, part)
        return token

    @pl.when(j == 0)
    def _first():
        if tps > 1:
            @pl.when(i == 0)
            def _init():
                carry_ref[...] = jnp.zeros_like(carry_ref)
        matmul(tiles)

    @pl.when((j > 0) & (j < n_full))
    def _steady():
        token = epilogue(tiles)
        matmul(tiles, None if emit else token)

    if rem:
        @pl.when(j == n_full)
        def _short():
            token = epilogue(tiles)
            matmul(rem, None if emit else token)

    @pl.when(j == n_full + (1 if rem else 0))
    def _drain():
        epilogue(rem if rem else tiles)


def _up_conv_geglu(h, w_u, w_v, w_conv, b_conv, conv_state, t):
    m, d = h.shape
    f = w_conv.shape[1]
    (wu_arr, wu_tile0), (wv_arr, wv_tile0) = w_u, w_v
    emit = wu_arr.dtype == _F32
    tm = _row_tile(m, t, _TM)
    tn = _TN_FF
    assert f % tn == 0, (f, tn)
    assert not emit or m == tm, "the bf16 copies are written by a single row tile"
    nf = f // tn
    tiles = 1 if emit else min(_UP_TILES, nf)
    n_groups = -(-nf // tiles)
    w = tiles * tn
    spt, tps = _seq_block(tm, t)
    rows = tm // spt
    assert rows >= _SUBLANES and conv_state.shape[1] == CONV_W - 1
    cur = lambda j: jnp.minimum(j, n_groups - 1)
    prev = lambda j: jnp.maximum(j - 1, 0)
    tile = lambda j, c: jnp.minimum(tiles * cur(j) + c, nf - 1)
    est = 2 * (_nbytes((tm, d), _BF16) + 2 * tiles * _nbytes((d, tn), wu_arr.dtype) + _nbytes((tm, w), _BF16)
               + _nbytes((spt, _SUBLANES, w), _F32)) + _nbytes((n_groups, _SUBLANES, w), _F32) + (
        2 * _nbytes((tm, w), _F32) + 10 * _nbytes((tm, tn), _F32))
    out_specs = [pl.BlockSpec((tm, w), lambda i, j: (i, prev(j))),
                 pl.BlockSpec((1, spt, _SUBLANES, w), lambda i, j: (i, 0, 0, prev(j)))]
    out_shape = [jax.ShapeDtypeStruct((m, f), _BF16),
                 jax.ShapeDtypeStruct((m // tm, spt, _SUBLANES, f), _F32)]
    if emit:
        for _ in range(2):
            copy_shape, copy_spec = _w_copy(d, f, tn, lambda i, j: cur(j))
            out_shape.append(copy_shape)
            out_specs.append(copy_spec)
        est += 4 * _nbytes((d, tn), _BF16)
    w_specs = [_w_spec(arr, d, tn, lambda i, j, c=c, t0=t0: t0 + tile(j, c))
               for arr, t0 in ((wu_arr, wu_tile0), (wv_arr, wv_tile0)) for c in range(tiles)]
    return pl.pallas_call(
        functools.partial(_up_kernel, tps=tps, nf=nf, tiles=tiles, emit=emit),
        grid=(m // tm, n_groups + 1),
        in_specs=[pl.BlockSpec((tm, d), lambda i, j: (i, 0)), *w_specs,
                  pl.BlockSpec((CONV_W, w), lambda i, j: (0, prev(j))),
                  pl.BlockSpec((1, w), lambda i, j: (0, prev(j))),
                  pl.BlockSpec((spt, CONV_W - 1, w), lambda i, j: (i // tps, 0, prev(j)))],
        out_specs=out_specs,
        out_shape=out_shape,
        scratch_shapes=[pltpu.VMEM((tm, w), _F32), pltpu.VMEM((tm, w), _F32),
                        pltpu.VMEM((n_groups, _SUBLANES, w), _F32)],
        compiler_params=_cparams(("arbitrary", "arbitrary"), est),
        name="up_conv_geglu_emit" if emit else "up_conv_geglu",
    )(h, *([wu_arr] * tiles), *([wv_arr] * tiles), w_conv, b_conv.reshape(1, f), conv_state)


def _layer_front(x, mod, pos0, ret_state, swa_kc, swa_vc, conv_state, w, side_cast=None):
    b, t, d = x.shape
    m = b * t
    x2 = x.reshape(m, d)
    sh1, sc1, gt1, sh2, sc2, gt2 = [v.reshape(b, 1, d) for v in jnp.split(mod, 6, axis=-1)]
    bf = {}

    h = _norm_mod(x2, w["g_mix"], sc1, sh1, t)
    bq, fo, *w_bf = _in_proj(h, w["w_in"], t, pos0, None if side_cast is None else w[side_cast])
    bf.update(zip(("w_in",) if side_cast is None else (side_cast,), w_bf))
    a_ret, s_new = _retention(bq, fo, ret_state, b, t)
    if swa_kc is None:
        o_swa = _swa_prompt(bq, fo, w["swa_sinks"], b, t)
        keep = min(WINDOW, t)
    else:
        o_swa = _swa_cache(bq, fo, swa_kc, swa_vc, w["swa_sinks"], b, t)
        keep = t
    merged, *w_bf = _merge(a_ret, o_swa, fo, w["w_ret_o"], w["w_swa_o"])
    bf.update(zip(("w_ret_o", "w_swa_o"), w_bf))
    x1, *w_bf = _proj_residual(merged, w["w_out"], x2, gt1, t, _TM, 2, "out_proj")
    bf.update(zip(("w_out",), w_bf))

    h2 = _norm_mod(x1, w["g_ffn"], sc2, sh2, t)
    f = w["b_conv"].shape[0]
    cs = jnp.zeros((b, CONV_W - 1, f), _F32) if conv_state is None else conv_state
    if "w_up_u" in w:
        w_u, w_v = (w["w_up_u"], 0), (w["w_up_v"], 0)
    else:
        w_u, w_v = (w["w_up"], 0), (w["w_up"], f // _TN_FF)
    g, tails, *w_bf = _up_conv_geglu(h2, w_u, w_v, w["w_conv"], w["b_conv"], cs, t)
    bf.update(zip(("w_up_u", "w_up_v"), w_bf))

    skv = fo.reshape(b, t, fo.shape[1])[:, t - keep:, fo.shape[1] - 2 * SWA_KV:]
    k_new = skv[:, :, :SWA_KV].reshape(b, keep, SWA_KV_HEADS, SWA_HD)
    v_new = skv[:, :, SWA_KV:].reshape(b, keep, SWA_KV_HEADS, SWA_HD)
    segs = tails.shape[0] * tails.shape[1] // b
    tails = tails.reshape(b, segs, _SUBLANES, f)
    conv_new = tails[:, segs - 1, _SUBLANES - (CONV_W - 1):, :]
    return (g, x1, gt2, (b, t, d)), (s_new, k_new, v_new, conv_new), bf


def _layer_back(front, w):
    g, x1, gt2, (b, t, d) = front
    x3, = _proj_residual(g, w["w_down"], x1, gt2, t, _TM_DOWN, 1, "down_proj")
    return _final_norm(x3, w["g_final"]).reshape(b, t, d)


def kernel(x_prompt, x_sample, cache_swa_k, cache_swa_v, state_ret, state_conv, c_prompt, c_sample,
           g_mix, g_ffn, w_ada, b_ada, w_in, swa_sinks, w_ret_o, w_swa_o, w_out,
           w_up, w_conv, b_conv, w_down, g_final):
    depth = g_mix.shape[0]
    assert depth == 1, "single-layer stack"
    bp = x_prompt.shape[0]
    bs = x_sample.shape[0]
    l = 0

    c_all = jnp.concatenate([c_prompt, c_sample], axis=0)
    rows = -(-c_all.shape[0] // 16) * 16
    c_all = jnp.pad(c_all, ((0, rows - c_all.shape[0]), (0, 0)))
    mod = _ada(c_all, w_ada[l], b_ada[l])

    w = {
        "g_mix": g_mix[l], "g_ffn": g_ffn[l], "g_final": g_final, "swa_sinks": swa_sinks[l],
        "w_in": w_in[l], "w_ret_o": w_ret_o[l], "w_swa_o": w_swa_o[l], "w_out": w_out[l], "w_up": w_up[l],
        "w_down": w_down[l],
        "w_conv": w_conv[l], "b_conv": b_conv[l],
    }

    front_s, (ss, ks, vs, cs), w_bf = _layer_front(x_sample, mod[bp:bp + bs], PAST_LEN, state_ret[l],
                                                   cache_swa_k[l], cache_swa_v[l], state_conv[l], w)
    w = {**w, **w_bf}
    front_p, (sp, kp, vp, cp), w_bf = _layer_front(x_prompt, mod[:bp], 0, None, None, None, None, w,
                                                   side_cast="w_down")
    w = {**w, **w_bf}
    ys = _layer_back(front_s, w)
    yp = _layer_back(front_p, w)
    stack = lambda a: a[None]
    return (yp, ys, stack(sp), stack(kp), stack(vp), stack(cp), stack(ss), stack(ks), stack(vs), stack(cs))
```

```python
import functools

import jax
import jax.numpy as jnp
import numpy as np
from jax import lax
from jax.experimental import pallas as pl
from jax.experimental.pallas import tpu as pltpu

_F32 = jnp.float32
_BF16 = jnp.bfloat16

CHUNK = 64
RET_HEADS = 8
RET_DK = 256
RET_DV = 256
RET_QK = RET_HEADS * RET_DK
RET_V = RET_HEADS * RET_DV
ROPE_BASE = 10000.0
SWA_HEADS = 32
SWA_KV_HEADS = 4
SWA_HD = 64
SWA_GROUP = SWA_HEADS // SWA_KV_HEADS
SWA_PAIRS = SWA_GROUP // 2
SWA_Q = SWA_HEADS * SWA_HD
SWA_KV = SWA_KV_HEADS * SWA_HD
WINDOW = 128
WINDOW_CHUNKS = WINDOW // CHUNK
CONV_W = 3
EPS = 1e-6
NEG_INF = -1e30
PAST_LEN = 2048

_V7X_VMEM_BYTES = 64 * 1024 * 1024
_LANES = 128
_SUBLANES = 8

_TM = 1024
_TM_DOWN = 512
_TM_NORM = 512
_TN = 2 * SWA_KV
_TN_EMIT = 512
_TN_ADA = 512
_RET_CHUNK = 256
_SWA_QB = 2
_TN_FF = 256
_UP_TILES = 2
_SIDE_ROWS = 128


def _cparams(sem, vmem_estimate, flags=None):
    limit = min(int(vmem_estimate * 1.15) + (4 << 20), _V7X_VMEM_BYTES - (6 << 20))
    return pltpu.CompilerParams(dimension_semantics=sem, vmem_limit_bytes=limit, flags=flags)


def _nbytes(shape, dtype):
    return int(np.prod(shape)) * jnp.dtype(dtype).itemsize


def _row_tile(m, t, cap):
    tm = min(cap, m)
    assert m % tm == 0 and (t % tm == 0 or tm % t == 0), (m, t, tm)
    return tm


def _seq_block(tm, t):
    return max(1, tm // t), max(1, t // tm)


def _mod_spec(tm, t, tn, col):
    spt, tps = _seq_block(tm, t)
    return pl.BlockSpec((spt, 1, tn), lambda i, j: (i // tps, 0, col(i, j)))


def _per_seq(val, mod, fn):
    spt = mod.shape[0]
    if spt == 1:
        return fn(val, mod[0])
    tm, n = val.shape
    return fn(val.reshape(spt, tm // spt, n), mod).reshape(tm, n)


def _ada_kernel(c_ref, w_ref, b_ref, o_ref):
    c = c_ref[...]
    a = (c * jax.nn.sigmoid(c)).astype(_BF16)
    o_ref[...] = jnp.dot(a, w_ref[...].astype(_BF16), preferred_element_type=_F32) + b_ref[...]


def _ada(c, w_ada, b_ada):
    rows, d = c.shape
    n = w_ada.shape[1]
    tn = _TN_ADA
    est = 2 * (_nbytes((rows, d), _F32) + _nbytes((d, tn), _F32) + _nbytes((rows, tn), _F32)) + _nbytes((d, tn), _BF16)
    return pl.pallas_call(
        _ada_kernel,
        grid=(n // tn,),
        in_specs=[pl.BlockSpec((rows, d), lambda j: (0, 0)),
                  pl.BlockSpec((d, tn), lambda j: (0, j)),
                  pl.BlockSpec((1, tn), lambda j: (0, j))],
        out_specs=pl.BlockSpec((rows, tn), lambda j: (0, j)),
        out_shape=jax.ShapeDtypeStruct((rows, n), _F32),
        compiler_params=_cparams(("arbitrary",), est),
        name="ada_mod",
    )(c, w_ada, b_ada.reshape(1, n))


def _norm_rows(m):
    return _TM_NORM if m >= 8 * _TM_NORM else _TM_NORM // 2


def _rms(x, g):
    y = x * lax.rsqrt(jnp.mean(x * x, axis=-1, keepdims=True) + EPS)
    return y * g


def _norm_mod_kernel(x_ref, g_ref, sc_ref, sh_ref, o_ref):
    y = _rms(x_ref[...], g_ref[...])
    h = _per_seq(y, sc_ref[...], lambda v, sc: v * (1.0 + sc))
    h = _per_seq(h, sh_ref[...], lambda v, sh: v + sh)
    o_ref[...] = h.astype(o_ref.dtype)


def _norm_mod(x, g, sc, sh, t):
    m, d = x.shape
    tm = _row_tile(m, t, _norm_rows(m))
    est = 2 * (_nbytes((tm, d), _F32) + _nbytes((tm, d), _BF16)) + 3 * _nbytes((tm, d), _F32)
    return pl.pallas_call(
        _norm_mod_kernel,
        grid=(m // tm, 1),
        in_specs=[pl.BlockSpec((tm, d), lambda i, j: (i, 0)),
                  pl.BlockSpec((1, d), lambda i, j: (0, 0)),
                  _mod_spec(tm, t, d, lambda i, j: 0),
                  _mod_spec(tm, t, d, lambda i, j: 0)],
        out_specs=pl.BlockSpec((tm, d), lambda i, j: (i, 0)),
        out_shape=jax.ShapeDtypeStruct((m, d), _BF16),
        compiler_params=_cparams(("arbitrary", "arbitrary"), est),
        name="norm_mod",
    )(x, g.reshape(1, d), sc, sh)


def _final_norm_kernel(x_ref, g_ref, o_ref):
    o_ref[...] = _rms(x_ref[...], g_ref[...])


def _final_norm(x, g):
    m, d = x.shape
    tm = min(_norm_rows(m), m)
    est = 4 * _nbytes((tm, d), _F32) + 2 * _nbytes((tm, d), _F32)
    return pl.pallas_call(
        _final_norm_kernel,
        grid=(m // tm,),
        in_specs=[pl.BlockSpec((tm, d), lambda i: (i, 0)),
                  pl.BlockSpec((1, d), lambda i: (0, 0))],
        out_specs=pl.BlockSpec((tm, d), lambda i: (i, 0)),
        out_shape=jax.ShapeDtypeStruct((m, d), _F32),
        compiler_params=_cparams(("arbitrary",), est),
        name="final_norm",
    )(x, g.reshape(1, d))


def _pair_slabs(slab, parity, fill):
    low = lax.broadcasted_iota(jnp.int32, slab.shape, 1) < SWA_HD
    if parity == 0:
        even = jnp.where(low, slab, 0.0)
        odd = pltpu.roll(even, SWA_HD, 1)
    else:
        odd = jnp.where(low, 0.0, slab)
        even = pltpu.roll(odd, SWA_HD, 1)
    if fill == 0.0:
        return even, odd
    return jnp.where(low, even, fill), jnp.where(low, fill, odd)


class _InPlan:
    def __init__(self, tn, pair, d):
        w = tn * pair
        n_skv = 2 * SWA_KV // tn
        assert RET_QK % w == 0 and SWA_Q % w == 0 and (2 * d) % w == 0 and (RET_V + 2 * d) % w == 0
        assert tn % RET_DK == 0 and (2 * SWA_KV) % tn == 0 and (pair == 1 or n_skv == 1)
        self.tn, self.pair, self.w = tn, pair, w
        r = self.r = RET_QK // w
        self.t_skv = 5 * r
        self.t_gab = self.t_skv + n_skv
        self.steps = self.t_gab + 2 * d // w
        self.skv_block = (RET_V + 2 * d) // w
        self.fo_cols = RET_V + 2 * d + 2 * SWA_KV

    def w_tile(self, t):
        p, ts, tg = self.pair, self.t_skv, self.t_gab
        return jnp.where(t < ts, p * t, jnp.where(t < tg, p * ts + (t - ts), p * ts + (tg - ts) + p * (t - tg)))

    def bq_block(self, t):
        r = self.r
        return jnp.where(t < 3 * r, t, jnp.where(t < 4 * r, 3 * r - 1, jnp.where(t < 5 * r, t - r, 4 * r - 1)))

    def fo_block(self, t):
        r, ts, tg = self.r, self.t_skv, self.t_gab
        return jnp.where(t < 3 * r, 0, jnp.where(t < 4 * r, t - 3 * r, jnp.where(
            t < ts, r - 1, jnp.where(t < tg, self.skv_block + (t - ts), r + (t - tg)))))


def _in_proj_kernel(*refs, plan, emit, side_cast):
    n_w = plan.pair
    h_ref, w_refs, (cos_ref, sin_ref) = refs[0], refs[1:1 + n_w], refs[1 + n_w:3 + n_w]
    rest = list(refs[3 + n_w:])
    side_ref = rest.pop(0) if side_cast else None
    bq_ref, fo_ref = rest[:2]
    if emit:
        wbf_ref = rest[2]
        _bf16_weight(w_refs[0], wbf_ref)
        w_refs = (wbf_ref,)
    side_out_ref = rest[-1] if side_cast else None
    t = pl.program_id(1)
    r, tn = plan.r, plan.tn
    half = RET_DK // 2

    def tiles():
        if side_cast:
            side_out_ref[...] = side_ref[...].astype(_BF16)
        for n, w_ref in enumerate(w_refs):
            yield jnp.dot(h_ref[...], _bf16_weight(w_ref, None), preferred_element_type=_F32), n * tn

    @pl.when(t < 2 * r)
    def _rope():
        c = cos_ref[...]
        s = sin_ref[...]
        scale = jnp.where(t >= r, RET_DK ** -0.5, 1.0).astype(_F32)
        for acc, col in tiles():
            for hh in range(tn // RET_DK):
                lo = hh * RET_DK
                x1 = acc[:, lo:lo + half]
                x2 = acc[:, lo + half:lo + RET_DK]
                bq_ref[:, col + lo:col + lo + half] = ((x1 * c - x2 * s) * scale).astype(_BF16)
                bq_ref[:, col + lo + half:col + lo + RET_DK] = ((x2 * c + x1 * s) * scale).astype(_BF16)

    @pl.when(((t >= 2 * r) & (t < 3 * r)) | ((t >= 4 * r) & (t < 5 * r)))
    def _bf16():
        for acc, col in tiles():
            bq_ref[:, col:col + tn] = acc.astype(_BF16)

    @pl.when(((t >= 3 * r) & (t < 4 * r)) | (t >= plan.t_gab))
    def _f32():
        for acc, col in tiles():
            fo_ref[:, col:col + tn] = acc

    @pl.when((t >= plan.t_skv) & (t < plan.t_gab))
    def _kv():
        if side_cast:
            side_out_ref[...] = side_ref[...].astype(_BF16)
        fo_ref[:, 0:tn] = jnp.dot(h_ref[...], _bf16_weight(w_refs[0], None), preferred_element_type=_F32)


def _rope_tables(t, pos0, tm):
    half = RET_DK // 2
    inv = ROPE_BASE ** (-jnp.arange(half, dtype=_F32) * (2.0 / RET_DK))
    pos = (pos0 + jnp.arange(t, dtype=jnp.int32)).astype(_F32)
    ang = pos[:, None] * inv[None, :]
    reps = max(1, tm // t)
    return jnp.tile(jnp.cos(ang), (reps, 1)), jnp.tile(jnp.sin(ang), (reps, 1))


def _in_proj(h, w_in, t, pos0, side_cast=None):
    m, d = h.shape
    n_in = _w_cols(w_in)
    emit = w_in.dtype == _F32
    tm = _row_tile(m, t, _TM)
    plan = _InPlan(_TN_EMIT, 1, d) if emit else _InPlan(_TN, 2, d)
    tn, w = plan.tn, plan.w
    assert n_in == plan.fo_cols + 4 * RET_QK, (n_in, d)
    assert not emit or m == tm, "the bf16 copy is written by a single row tile"
    cos, sin = _rope_tables(t, pos0, tm)
    tab_blocks = cos.shape[0] // tm
    half = RET_DK // 2
    last_tile = n_in // tn - 1

    out_shape = [jax.ShapeDtypeStruct((m, 4 * RET_QK), _BF16), jax.ShapeDtypeStruct((m, plan.fo_cols), _F32)]
    out_specs = [pl.BlockSpec((tm, w), lambda i, j: (i, plan.bq_block(j))),
                 pl.BlockSpec((tm, w), lambda i, j: (i, plan.fo_block(j)))]
    w_specs = [_w_spec(w_in, d, tn, lambda i, j, n=n: jnp.minimum(plan.w_tile(j) + n, last_tile))
               for n in range(plan.pair)]
    est = 2 * (_nbytes((tm, d), _BF16) + plan.pair * _nbytes((d, tn), w_in.dtype) + 2 * _nbytes((tm, half), _F32)
               + _nbytes((tm, w), _BF16) + _nbytes((tm, w), _F32)) + 2 * plan.pair * _nbytes((tm, tn), _F32)
    if emit:
        copy_shape, copy_spec = _w_copy(d, n_in, tn, lambda i, j: plan.w_tile(j))
        out_shape.append(copy_shape)
        out_specs.append(copy_spec)
        est += 2 * _nbytes((d, tn), _BF16)
    side_args, side_specs = [], []
    if side_cast is not None:
        rows, cols = side_cast.shape
        n_blocks = rows // _SIDE_ROWS
        assert not emit and rows % _SIDE_ROWS == 0 and n_blocks <= (m // tm) * plan.steps
        side_spec = pl.BlockSpec((_SIDE_ROWS, cols), lambda i, j: (jnp.minimum(i * plan.steps + j, n_blocks - 1), 0))
        side_args, side_specs = [side_cast], [side_spec]
        out_shape.append(jax.ShapeDtypeStruct((rows, cols), _BF16))
        out_specs.append(side_spec)
        est += 2 * (_nbytes((_SIDE_ROWS, cols), _F32) + _nbytes((_SIDE_ROWS, cols), _BF16))
    return pl.pallas_call(
        functools.partial(_in_proj_kernel, plan=plan, emit=emit, side_cast=side_cast is not None),
        grid=(m // tm, plan.steps),
        in_specs=[pl.BlockSpec((tm, d), lambda i, j: (i, 0)), *w_specs,
                  pl.BlockSpec((tm, half), lambda i, j: (i % tab_blocks, 0)),
                  pl.BlockSpec((tm, half), lambda i, j: (i % tab_blocks, 0)), *side_specs],
        out_specs=out_specs,
        out_shape=out_shape,
        compiler_params=_cparams(("arbitrary", "arbitrary"), est),
        name="in_proj_emit" if emit else "in_proj",
    )(h, *([w_in] * plan.pair), cos, sin, *side_args)


def _retention_kernel(*refs, hb, cc, n_chunks, has_state):
    if has_state:
        lg_ref, q_ref, k_ref, v_ref, rg_ref, s0_ref, a_ref, sout_ref, s_scr = refs
    else:
        lg_ref, q_ref, k_ref, v_ref, rg_ref, a_ref, sout_ref, s_scr = refs
    row = lax.broadcasted_iota(jnp.int32, (cc, cc), 0)
    col = lax.broadcasted_iota(jnp.int32, (cc, cc), 1)
    diff = (row - col).astype(_F32)
    idx = lax.broadcasted_iota(jnp.int32, (cc, 1), 0).astype(_F32)
    for hh in range(hb):
        lg = lg_ref[hh][:, :1]
        dmask = jnp.where(diff >= 0, jnp.exp(lg * jnp.maximum(diff, 0.0)), 0.0)
        dq = jnp.exp(lg * (idx + 1.0))
        dk = jnp.exp(lg * (cc - 1.0 - idx))
        ds = jnp.exp(lg * float(cc))
        cols = slice(hh * RET_DK, (hh + 1) * RET_DK)
        if has_state:
            s_scr[...] = s0_ref[0, hh]
        else:
            s_scr[...] = jnp.zeros_like(s_scr)

        def body(c, carry):
            r = pl.ds(pl.multiple_of(c * cc, cc), cc)
            q = q_ref[r, cols]
            k = k_ref[r, cols]
            v = v_ref[r, cols]
            state = s_scr[...]
            scores = lax.dot_general(q, k, (((1,), (1,)), ((), ())), preferred_element_type=_F32) * dmask
            y = jnp.dot(scores.astype(_BF16), v, preferred_element_type=_F32)
            y = y + jnp.dot(q, state.astype(_BF16), preferred_element_type=_F32) * dq
            k_dec = (k.astype(_F32) * dk).astype(_BF16)
            s_scr[...] = state * ds + lax.dot_general(
                k_dec, v, (((0,), (0,)), ((), ())), preferred_element_type=_F32)
            mu = jnp.mean(y, axis=-1, keepdims=True)
            dev = y - mu
            var = jnp.mean(dev * dev, axis=-1, keepdims=True)
            yn = dev * lax.rsqrt(var + EPS)
            g = rg_ref[r, cols]
            a_ref[r, cols] = (g * jax.nn.sigmoid(g) * yn).astype(_BF16)
            return carry

        lax.fori_loop(0, n_chunks, body, 0, unroll=next(u for u in (4, 2, 1) if n_chunks % u == 0))
        sout_ref[0, hh] = s_scr[...]


def _retention(qkv, rg, state, b, t):
    m = qkv.shape[0]
    cc = min(_RET_CHUNK, t)
    assert t % cc == 0 and cc % min(CHUNK, t) == 0
    hb = RET_HEADS if t * RET_QK * 2 <= (1 << 20) else 1
    nhb = RET_HEADS // hb
    wcols = hb * RET_DK
    h = jnp.arange(RET_HEADS, dtype=_F32)
    lg = jnp.log1p(-jnp.exp2(-5.0 - h))
    lg = jnp.broadcast_to(lg[:, None, None], (RET_HEADS, 1, _LANES))
    has_state = state is not None
    in_specs = [pl.BlockSpec((hb, 1, _LANES), lambda bi, hi: (hi, 0, 0)),
                pl.BlockSpec((t, wcols), lambda bi, hi: (bi, hi)),
                pl.BlockSpec((t, wcols), lambda bi, hi: (bi, nhb + hi)),
                pl.BlockSpec((t, wcols), lambda bi, hi: (bi, 2 * nhb + hi)),
                pl.BlockSpec((t, wcols), lambda bi, hi: (bi, hi))]
    args = [lg, qkv, qkv, qkv, rg]
    s_spec = pl.BlockSpec((1, hb, RET_DK, RET_DV), lambda bi, hi: (bi, hi, 0, 0))
    if has_state:
        in_specs.append(s_spec)
        args.append(state)
    est = 2 * (4 * _nbytes((t, wcols), _BF16) + _nbytes((t, wcols), _F32)
               + 2 * _nbytes((hb, RET_DK, RET_DV), _F32)) + 8 * _nbytes((cc, max(cc, RET_DV)), _F32)
    kern = functools.partial(_retention_kernel, hb=hb, cc=cc, n_chunks=t // cc, has_state=has_state)
    return pl.pallas_call(
        kern,
        grid=(b, nhb),
        in_specs=in_specs,
        out_specs=(pl.BlockSpec((t, wcols), lambda bi, hi: (bi, hi)), s_spec),
        out_shape=(jax.ShapeDtypeStruct((m, RET_V), _BF16),
                   jax.ShapeDtypeStruct((b, RET_HEADS, RET_DK, RET_DV), _F32)),
        scratch_shapes=[pltpu.VMEM((RET_DK, RET_DV), _F32)],
        compiler_params=_cparams(("arbitrary", "arbitrary"), est),
        name="retention",
    )(*args)


def _swa_core(qst, k_even, k_odd, v_even, v_odd, sink_even, sink_odd, valid):
    def weights(kx, sink):
        s = lax.dot_general(qst, kx, (((1,), (1,)), ((), ())), preferred_element_type=_F32) * (SWA_HD ** -0.5)
        if valid is not None:
            s = jnp.where(valid, s, NEG_INF)
        mx = jnp.maximum(jnp.max(s, axis=-1, keepdims=True), sink)
        return jnp.exp(s - mx).astype(_BF16), jnp.exp(sink - mx)

    e_even, t_even = weights(k_even, sink_even)
    e_odd, t_odd = weights(k_odd, sink_odd)
    o_even = jnp.dot(e_even, v_even, preferred_element_type=_F32)
    o_odd = jnp.dot(e_odd, v_odd, preferred_element_type=_F32)
    low = lax.broadcasted_iota(jnp.int32, o_even.shape, 1) < SWA_HD
    num = jnp.where(low, o_even, o_odd)
    den = jnp.where(low, pltpu.roll(o_even, SWA_HD, 1) + t_even, pltpu.roll(o_odd, SWA_HD, 1) + t_odd)
    return num / den


def _swa_prompt_kernel(q_ref, k_ref, v_ref, sink_ref, o_ref, slab_ref, *, n_blocks, unroll):
    gw = SWA_GROUP * SWA_HD
    rows = _SWA_QB * CHUNK
    win = _SWA_QB + WINDOW_CHUNKS

    for par in range(2):
        slabs = _pair_slabs(k_ref[...], par, fill=0.0) + _pair_slabs(v_ref[...], par, fill=1.0)
        for c, x in enumerate(slabs):
            slab_ref[par, c] = x.astype(_BF16)

    def valid_mask(n_key_chunks, first):
        shape = (SWA_PAIRS * rows, n_key_chunks * CHUNK)
        qc = (lax.broadcasted_iota(jnp.int32, shape, 0) // CHUNK) % _SWA_QB
        kc = lax.broadcasted_iota(jnp.int32, shape, 1) // CHUNK
        if first:
            return kc <= qc
        return (kc >= qc) & (kc <= qc + WINDOW_CHUNKS)

    valid_first = valid_mask(_SWA_QB, True)
    valid = valid_mask(win, False)
    for par in range(2):
        sink_even = sink_ref[par, 0]
        sink_odd = sink_ref[par, 1]
        lane0 = par * gw

        def block(r, kr, mask):
            qst = jnp.concatenate(
                [q_ref[r, lane0 + p * _LANES:lane0 + (p + 1) * _LANES] for p in range(SWA_PAIRS)], axis=0)
            o = _swa_core(qst, slab_ref[par, 0, kr, :], slab_ref[par, 1, kr, :],
                          slab_ref[par, 2, kr, :], slab_ref[par, 3, kr, :], sink_even, sink_odd, mask)
            for p in range(SWA_PAIRS):
                o_ref[r, lane0 + p * _LANES:lane0 + (p + 1) * _LANES] = (
                    o[p * rows:(p + 1) * rows].astype(o_ref.dtype))

        block(pl.ds(0, rows), pl.ds(0, rows), valid_first)
        if n_blocks > 1:
            def body(b, carry):
                r = pl.ds(pl.multiple_of(b * rows, rows), rows)
                kr = pl.ds(pl.multiple_of(b * rows - WINDOW_CHUNKS * CHUNK, CHUNK), win * CHUNK)
                block(r, kr, valid)
                return carry

            lax.fori_loop(1, n_blocks, body, 0, unroll=unroll)


def _sink_columns(sinks, rows):
    s = sinks.astype(_F32).reshape(SWA_KV_HEADS, SWA_PAIRS, 2).transpose(0, 2, 1)
    return jnp.repeat(s, rows, axis=-1)[..., None]


def _swa_prompt(bq, fo, sinks, b, t):
    m = bq.shape[0]
    gw2 = 2 * SWA_GROUP * SWA_HD
    rows = _SWA_QB * CHUNK
    assert t % rows == 0 and WINDOW_CHUNKS == _SWA_QB
    n_blocks = t // rows
    unroll = next(u for u in (15, 5, 3, 2, 1) if (n_blocks - 1) % u == 0)
    sink_cols = _sink_columns(sinks, rows)
    q0 = 3 * RET_QK // gw2
    k0 = (fo.shape[1] - 2 * SWA_KV) // _LANES
    v0 = k0 + SWA_KV // _LANES
    est = (2 * (2 * _nbytes((t, gw2), _BF16) + 2 * _nbytes((t, _LANES), _F32))
           + _nbytes((2, 4, t, _LANES), _BF16) + (8 << 20))
    return pl.pallas_call(
        functools.partial(_swa_prompt_kernel, n_blocks=n_blocks, unroll=unroll),
        grid=(b, SWA_KV_HEADS // 2),
        in_specs=[pl.BlockSpec((t, gw2), lambda bi, kp: (bi, q0 + kp)),
                  pl.BlockSpec((t, _LANES), lambda bi, kp: (bi, k0 + kp)),
                  pl.BlockSpec((t, _LANES), lambda bi, kp: (bi, v0 + kp)),
                  pl.BlockSpec((2, 2, SWA_PAIRS * rows, 1), lambda bi, kp: (kp, 0, 0, 0))],
        out_specs=pl.BlockSpec((t, gw2), lambda bi, kp: (bi, kp)),
        out_shape=jax.ShapeDtypeStruct((m, SWA_Q), _BF16),
        scratch_shapes=[pltpu.VMEM((2, 4, t, _LANES), _BF16)],
        compiler_params=_cparams(("arbitrary", "arbitrary"), est),
        name="swa_prompt",
    )(bq, fo, fo, sink_cols)


def _swa_cache_kernel(q_ref, kvn_ref, kc_ref, vc_ref, sink_ref, o_ref, *, L):
    gw = SWA_GROUP * SWA_HD
    kvn = kvn_ref[...]
    k_all = jnp.concatenate([kc_ref[0], kvn[:, :SWA_KV]], axis=0)
    v_all = jnp.concatenate([vc_ref[0], kvn[:, SWA_KV:]], axis=0)
    for kv in range(SWA_KV_HEADS):
        lanes = slice(_LANES * (kv // 2), _LANES * (kv // 2 + 1))
        slabs = _pair_slabs(k_all[:, lanes], kv % 2, fill=0.0) + _pair_slabs(v_all[:, lanes], kv % 2, fill=1.0)
        qst = jnp.concatenate(
            [q_ref[:, kv * gw + p * _LANES:kv * gw + (p + 1) * _LANES] for p in range(SWA_PAIRS)], axis=0)
        o = _swa_core(qst, *[x.astype(_BF16) for x in slabs], sink_ref[kv, 0], sink_ref[kv, 1], None)
        for p in range(SWA_PAIRS):
            lo = kv * gw + p * _LANES
            o_ref[:, lo:lo + _LANES] = o[p * L:(p + 1) * L].astype(o_ref.dtype)


def _swa_cache(bq, fo, k_cache, v_cache, sinks, b, t):
    m = bq.shape[0]
    rows = k_cache.shape[1]
    kc = k_cache.reshape(b, rows, SWA_KV)
    vc = v_cache.reshape(b, rows, SWA_KV)
    sink_cols = _sink_columns(sinks, t)
    assert (fo.shape[1] - 2 * SWA_KV) % (2 * SWA_KV) == 0
    kv0 = fo.shape[1] // (2 * SWA_KV) - 1
    est = 2 * (2 * _nbytes((t, SWA_Q), _BF16) + _nbytes((t, 2 * SWA_KV), _F32)
               + 2 * _nbytes((rows, SWA_KV), _F32)) + (4 << 20)
    return pl.pallas_call(
        functools.partial(_swa_cache_kernel, L=t),
        grid=(b,),
        in_specs=[pl.BlockSpec((t, SWA_Q), lambda bi: (bi, 3 * RET_QK // SWA_Q)),
                  pl.BlockSpec((t, 2 * SWA_KV), lambda bi: (bi, kv0)),
                  pl.BlockSpec((1, rows, SWA_KV), lambda bi: (bi, 0, 0)),
                  pl.BlockSpec((1, rows, SWA_KV), lambda bi: (bi, 0, 0)),
                  pl.BlockSpec((SWA_KV_HEADS, 2, SWA_PAIRS * t, 1), lambda bi: (0, 0, 0, 0))],
        out_specs=pl.BlockSpec((t, SWA_Q), lambda bi: (bi, 0)),
        out_shape=jax.ShapeDtypeStruct((m, SWA_Q), _BF16),
        compiler_params=_cparams(("arbitrary",), est),
        name="swa_cache",
    )(bq, fo, kc, vc, sink_cols)


def _bf16_weight(w_ref, wbf_ref):
    if wbf_ref is None:
        return w_ref[...]
    wbf_ref[...] = w_ref[...].astype(_BF16)
    return wbf_ref[...]


def _w_cols(w):
    return w.shape[1]


def _w_spec(w, k, tn, tile):
    assert w.shape[0] == k
    return pl.BlockSpec((k, tn), lambda i, j: (0, tile(i, j)))


def _w_copy(k, n, tn, tile):
    return jax.ShapeDtypeStruct((k, n), _BF16), pl.BlockSpec((k, tn), lambda i, j: (0, tile(i, j)))


def _merge_kernel(a_ref, o_ref, wr_ref, ws_ref, ga_ref, gb_ref, out_ref, wrb_ref=None, wsb_ref=None):
    yr = jnp.dot(a_ref[...], _bf16_weight(wr_ref, wrb_ref), preferred_element_type=_F32)
    ys = jnp.dot(o_ref[...], _bf16_weight(ws_ref, wsb_ref), preferred_element_type=_F32)
    out_ref[...] = (jax.nn.sigmoid(ga_ref[...]) * yr + jax.nn.sigmoid(gb_ref[...]) * ys).astype(out_ref.dtype)


def _merge(a_ret, o_swa, fo, w_ret_o, w_swa_o):
    m = a_ret.shape[0]
    d = _w_cols(w_ret_o)
    emit = w_ret_o.dtype == _F32
    tm = min(_TM, m)
    tn = _TN
    nd = d // tn
    g0 = RET_V // tn
    col = lambda i, j: j
    assert not emit or m == tm, "the bf16 copies are written by a single row tile"
    est = 2 * (_nbytes((tm, RET_V), _BF16) + _nbytes((tm, SWA_Q), _BF16) + _nbytes((RET_V, tn), w_ret_o.dtype)
               + _nbytes((SWA_Q, tn), w_swa_o.dtype) + 2 * _nbytes((tm, tn), _F32) + _nbytes((tm, tn), _BF16)
               ) + 4 * _nbytes((tm, tn), _F32)
    out_specs = [pl.BlockSpec((tm, tn), lambda i, j: (i, j))]
    out_shape = [jax.ShapeDtypeStruct((m, d), _BF16)]
    if emit:
        for k in (RET_V, SWA_Q):
            copy_shape, copy_spec = _w_copy(k, d, tn, col)
            out_shape.append(copy_shape)
            out_specs.append(copy_spec)
        est += 2 * (_nbytes((RET_V, tn), _BF16) + _nbytes((SWA_Q, tn), _BF16))
    return pl.pallas_call(
        _merge_kernel,
        grid=(m // tm, nd),
        in_specs=[pl.BlockSpec((tm, RET_V), lambda i, j: (i, 0)),
                  pl.BlockSpec((tm, SWA_Q), lambda i, j: (i, 0)),
                  _w_spec(w_ret_o, RET_V, tn, col),
                  _w_spec(w_swa_o, SWA_Q, tn, col),
                  pl.BlockSpec((tm, tn), lambda i, j: (i, g0 + j)),
                  pl.BlockSpec((tm, tn), lambda i, j: (i, g0 + nd + j))],
        out_specs=out_specs,
        out_shape=out_shape,
        compiler_params=_cparams(("arbitrary", "arbitrary"), est),
        name="merge_emit" if emit else "merge",
    )(a_ret, o_swa, w_ret_o, w_swa_o, fo, fo)


def _proj_residual_kernel(*refs, tiles):
    a_ref, w_refs, (x_ref, gt_ref, o_ref) = refs[0], refs[1:1 + tiles], refs[1 + tiles:4 + tiles]
    wbf_ref = refs[4 + tiles] if len(refs) > 4 + tiles else None
    tn = w_refs[0].shape[-1]
    for c, w_ref in enumerate(w_refs):
        lanes = slice(c * tn, (c + 1) * tn)
        f = jnp.dot(a_ref[...], _bf16_weight(w_ref, wbf_ref), preferred_element_type=_F32)
        o_ref[:, lanes] = x_ref[:, lanes] + _per_seq(f, gt_ref[:, :, lanes], lambda v, gt: gt * v)


def _proj_residual(a, w, x, gt, t, tm_cap, tiles, name):
    m, k = a.shape
    d = _w_cols(w)
    emit = w.dtype == _F32
    tm = _row_tile(m, t, tm_cap)
    tn = _TN
    tiles = 1 if emit else min(tiles, d // tn)
    wd = tiles * tn
    assert d % wd == 0
    assert not emit or m == tm, "the bf16 copy is written by a single row tile"
    est = 2 * (_nbytes((tm, k), _BF16) + tiles * _nbytes((k, tn), w.dtype) + 2 * _nbytes((tm, wd), _F32)
               ) + 2 * _nbytes((tm, tn), _F32)
    out_specs = [pl.BlockSpec((tm, wd), lambda i, j: (i, j))]
    out_shape = [jax.ShapeDtypeStruct((m, d), _F32)]
    if emit:
        copy_shape, copy_spec = _w_copy(k, d, tn, lambda i, j: j)
        out_shape.append(copy_shape)
        out_specs.append(copy_spec)
        est += 2 * _nbytes((k, tn), _BF16)
    return pl.pallas_call(
        functools.partial(_proj_residual_kernel, tiles=tiles),
        grid=(m // tm, d // wd),
        in_specs=[pl.BlockSpec((tm, k), lambda i, j: (i, 0)),
                  *[_w_spec(w, k, tn, lambda i, j, c=c: tiles * j + c) for c in range(tiles)],
                  pl.BlockSpec((tm, wd), lambda i, j: (i, j)),
                  _mod_spec(tm, t, wd, lambda i, j: j)],
        out_specs=out_specs,
        out_shape=out_shape,
        compiler_params=_cparams(("arbitrary", "arbitrary"), est),
        name=name + "_emit" if emit else name,
    )(a, *([w] * tiles), x, gt)


def _up_kernel(*refs, tps, nf, tiles, emit):
    h_ref, wu_refs, wv_refs = refs[0], refs[1:1 + tiles], refs[1 + tiles:1 + 2 * tiles]
    wc_ref, bc_ref, cs_ref, g_ref, tail_ref = refs[1 + 2 * tiles:6 + 2 * tiles]
    wub_ref, wvb_ref = refs[6 + 2 * tiles:8 + 2 * tiles] if emit else (None, None)
    u_scr, v_scr, carry_ref = refs[-3:]
    i = pl.program_id(0)
    j = pl.program_id(1)
    tm = u_scr.shape[0]
    tn = u_scr.shape[1] // tiles
    n_full, rem = divmod(nf, tiles)

    def ordered_after(x, token):
        bits = pltpu.bitcast(token, jnp.uint32)
        zero = pltpu.bitcast((bits >> 16) >> 16, _F32)
        return (x.reshape(tm // _SUBLANES, _SUBLANES, tn) + zero[None]).reshape(tm, tn)

    def matmul(k, token=None):
        h = h_ref[...]
        for c in range(k):
            lanes = slice(c * tn, (c + 1) * tn)
            u = jnp.dot(h, _bf16_weight(wu_refs[c], wub_ref), preferred_element_type=_F32)
            u_scr[:, lanes] = u if (token is None or c > 0) else ordered_after(u, token)
            v_scr[:, lanes] = jnp.dot(h, _bf16_weight(wv_refs[c], wvb_ref), preferred_element_type=_F32)

    def epilogue_tile(lanes):
        jp = j - 1
        u = u_scr[:, lanes]
        v = v_scr[:, lanes]
        cs = cs_ref[:, :, lanes]
        spt = cs.shape[0]
        rows = tm // spt
        c0 = cs[:, 0:1, :]
        c1 = cs[:, 1:2, :]
        if tps > 1:
            first = (i % tps) == 0
            prev = carry_ref[jp, :, lanes]
            c0 = jnp.where(first, c0, prev[_SUBLANES - 2:_SUBLANES - 1][None])
            c1 = jnp.where(first, c1, prev[_SUBLANES - 1:_SUBLANES][None])
            carry_ref[jp, :, lanes] = u[tm - _SUBLANES:tm]
        wc = wc_ref[:, lanes]
        bc = bc_ref[:, lanes]
        if spt == 1:
            r = lax.broadcasted_iota(jnp.int32, (tm, tn), 0)
            p1 = jnp.where(r == 0, c1[0], pltpu.roll(u, 1, 0))
            p2 = jnp.where(r == 0, c0[0], jnp.where(r == 1, c1[0], pltpu.roll(u, 2, 0)))
            uc = wc[0:1] * p2 + wc[1:2] * p1 + wc[2:3] * u + bc
            g = jax.nn.gelu(uc) * v
            tail_ref[0, 0, :, lanes] = u[tm - _SUBLANES:tm]
        else:
            u3 = u.reshape(spt, rows, tn)
            r = lax.broadcasted_iota(jnp.int32, (spt, rows, tn), 1)
            p1 = jnp.where(r == 0, c1, pltpu.roll(u3, 1, 1))
            p2 = jnp.where(r == 0, c0, jnp.where(r == 1, c1, pltpu.roll(u3, 2, 1)))
            uc = wc[0:1][None] * p2 + wc[1:2][None] * p1 + wc[2:3][None] * u3 + bc[None]
            g = (jax.nn.gelu(uc) * v.reshape(spt, rows, tn)).reshape(tm, tn)
            tail_ref[0, :, :, lanes] = u3[:, rows - _SUBLANES:rows, :]
        g_ref[:, lanes] = g.astype(g_ref.dtype)
        return jnp.max(g.reshape(tm // _SUBLANES, _SUBLANES, tn), axis=0)

    def epilogue(k):
        token = None
        for c in range(k):
            part = epilogue_tile(slice(c * tn, (c + 1) * tn))
            token = part if token is None else jnp.maximum(token, part)
        return token

    @pl.when(j == 0)
    def _first():
        if tps > 1:
            @pl.when(i == 0)
            def _init():
                carry_ref[...] = jnp.zeros_like(carry_ref)
        matmul(tiles)

    @pl.when((j > 0) & (j < n_full))
    def _steady():
        token = epilogue(tiles)
        matmul(tiles, None if emit else token)

    if rem:
        @pl.when(j == n_full)
        def _short():
            token = epilogue(tiles)
            matmul(rem, None if emit else token)

    @pl.when(j == n_full + (1 if rem else 0))
    def _drain():
        epilogue(rem if rem else tiles)


def _up_conv_geglu(h, w_u, w_v, w_conv, b_conv, conv_state, t):
    m, d = h.shape
    f = w_conv.shape[1]
    (wu_arr, wu_tile0), (wv_arr, wv_tile0) = w_u, w_v
    emit = wu_arr.dtype == _F32
    tm = _row_tile(m, t, _TM)
    tn = _TN_FF
    assert f % tn == 0, (f, tn)
    assert not emit or m == tm, "the bf16 copies are written by a single row tile"
    nf = f // tn
    tiles = 1 if emit else min(_UP_TILES, nf)
    n_groups = -(-nf // tiles)
    w = tiles * tn
    spt, tps = _seq_block(tm, t)
    rows = tm // spt
    assert rows >= _SUBLANES and conv_state.shape[1] == CONV_W - 1
    cur = lambda j: jnp.minimum(j, n_groups - 1)
    prev = lambda j: jnp.maximum(j - 1, 0)
    tile = lambda j, c: jnp.minimum(tiles * cur(j) + c, nf - 1)
    est = 2 * (_nbytes((tm, d), _BF16) + 2 * tiles * _nbytes((d, tn), wu_arr.dtype) + _nbytes((tm, w), _BF16)
               + _nbytes((spt, _SUBLANES, w), _F32)) + _nbytes((n_groups, _SUBLANES, w), _F32) + (
        2 * _nbytes((tm, w), _F32) + 10 * _nbytes((tm, tn), _F32))
    out_specs = [pl.BlockSpec((tm, w), lambda i, j: (i, prev(j))),
                 pl.BlockSpec((1, spt, _SUBLANES, w), lambda i, j: (i, 0, 0, prev(j)))]
    out_shape = [jax.ShapeDtypeStruct((m, f), _BF16),
                 jax.ShapeDtypeStruct((m // tm, spt, _SUBLANES, f), _F32)]
    if emit:
        for _ in range(2):
            copy_shape, copy_spec = _w_copy(d, f, tn, lambda i, j: cur(j))
            out_shape.append(copy_shape)
            out_specs.append(copy_spec)
        est += 4 * _nbytes((d, tn), _BF16)
    w_specs = [_w_spec(arr, d, tn, lambda i, j, c=c, t0=t0: t0 + tile(j, c))
               for arr, t0 in ((wu_arr, wu_tile0), (wv_arr, wv_tile0)) for c in range(tiles)]
    return pl.pallas_call(
        functools.partial(_up_kernel, tps=tps, nf=nf, tiles=tiles, emit=emit),
        grid=(m // tm, n_groups + 1),
        in_specs=[pl.BlockSpec((tm, d), lambda i, j: (i, 0)), *w_specs,
                  pl.BlockSpec((CONV_W, w), lambda i, j: (0, prev(j))),
                  pl.BlockSpec((1, w), lambda i, j: (0, prev(j))),
                  pl.BlockSpec((spt, CONV_W - 1, w), lambda i, j: (i // tps, 0, prev(j)))],
        out_specs=out_specs,
        out_shape=out_shape,
        scratch_shapes=[pltpu.VMEM((tm, w), _F32), pltpu.VMEM((tm, w), _F32),
                        pltpu.VMEM((n_groups, _SUBLANES, w), _F32)],
        compiler_params=_cparams(("arbitrary", "arbitrary"), est),
        name="up_conv_geglu_emit" if emit else "up_conv_geglu",
    )(h, *([wu_arr] * tiles), *([wv_arr] * tiles), w_conv, b_conv.reshape(1, f), conv_state)


def _layer_front(x, mod, pos0, ret_state, swa_kc, swa_vc, conv_state, w, side_cast=None):
    b, t, d = x.shape
    m = b * t
    x2 = x.reshape(m, d)
    sh1, sc1, gt1, sh2, sc2, gt2 = [v.reshape(b, 1, d) for v in jnp.split(mod, 6, axis=-1)]
    bf = {}

    h = _norm_mod(x2, w["g_mix"], sc1, sh1, t)
    bq, fo, *w_bf = _in_proj(h, w["w_in"], t, pos0, None if side_cast is None else w[side_cast])
    bf.update(zip(("w_in",) if side_cast is None else (side_cast,), w_bf))
    a_ret, s_new = _retention(bq, fo, ret_state, b, t)
    if swa_kc is None:
        o_swa = _swa_prompt(bq, fo, w["swa_sinks"], b, t)
        keep = min(WINDOW, t)
    else:
        o_swa = _swa_cache(bq, fo, swa_kc, swa_vc, w["swa_sinks"], b, t)
        keep = t
    merged, *w_bf = _merge(a_ret, o_swa, fo, w["w_ret_o"], w["w_swa_o"])
    bf.update(zip(("w_ret_o", "w_swa_o"), w_bf))
    x1, *w_bf = _proj_residual(merged, w["w_out"], x2, gt1, t, _TM, 2, "out_proj")
    bf.update(zip(("w_out",), w_bf))

    h2 = _norm_mod(x1, w["g_ffn"], sc2, sh2, t)
    f = w["b_conv"].shape[0]
    cs = jnp.zeros((b, CONV_W - 1, f), _F32) if conv_state is None else conv_state
    if "w_up_u" in w:
        w_u, w_v = (w["w_up_u"], 0), (w["w_up_v"], 0)
    else:
        w_u, w_v = (w["w_up"], 0), (w["w_up"], f // _TN_FF)
    g, tails, *w_bf = _up_conv_geglu(h2, w_u, w_v, w["w_conv"], w["b_conv"], cs, t)
    bf.update(zip(("w_up_u", "w_up_v"), w_bf))

    skv = fo.reshape(b, t, fo.shape[1])[:, t - keep:, fo.shape[1] - 2 * SWA_KV:]
    k_new = skv[:, :, :SWA_KV].reshape(b, keep, SWA_KV_HEADS, SWA_HD)
    v_new = skv[:, :, SWA_KV:].reshape(b, keep, SWA_KV_HEADS, SWA_HD)
    segs = tails.shape[0] * tails.shape[1] // b
    tails = tails.reshape(b, segs, _SUBLANES, f)
    conv_new = tails[:, segs - 1, _SUBLANES - (CONV_W - 1):, :]
    return (g, x1, gt2, (b, t, d)), (s_new, k_new, v_new, conv_new), bf


def _layer_back(front, w):
    g, x1, gt2, (b, t, d) = front
    x3, = _proj_residual(g, w["w_down"], x1, gt2, t, _TM_DOWN, 1, "down_proj")
    return _final_norm(x3, w["g_final"]).reshape(b, t, d)


def kernel(x_prompt, x_sample, cache_swa_k, cache_swa_v, state_ret, state_conv, c_prompt, c_sample,
           g_mix, g_ffn, w_ada, b_ada, w_in, swa_sinks, w_ret_o, w_swa_o, w_out,
           w_up, w_conv, b_conv, w_down, g_final):
    depth = g_mix.shape[0]
    assert depth == 1, "single-layer stack"
    bp = x_prompt.shape[0]
    bs = x_sample.shape[0]
    l = 0

    c_all = jnp.concatenate([c_prompt, c_sample], axis=0)
    rows = -(-c_all.shape[0] // 16) * 16
    c_all = jnp.pad(c_all, ((0, rows - c_all.shape[0]), (0, 0)))
    mod = _ada(c_all, w_ada[l], b_ada[l])

    w = {
        "g_mix": g_mix[l], "g_ffn": g_ffn[l], "g_final": g_final, "swa_sinks": swa_sinks[l],
        "w_in": w_in[l], "w_ret_o": w_ret_o[l], "w_swa_o": w_swa_o[l], "w_out": w_out[l], "w_up": w_up[l],
        "w_down": w_down[l],
        "w_conv": w_conv[l], "b_conv": b_conv[l],
    }

    front_s, (ss, ks, vs, cs), w_bf = _layer_front(x_sample, mod[bp:bp + bs], PAST_LEN, state_ret[l],
                                                   cache_swa_k[l], cache_swa_v[l], state_conv[l], w)
    w = {**w, **w_bf}
    front_p, (sp, kp, vp, cp), w_bf = _layer_front(x_prompt, mod[:bp], 0, None, None, None, None, w,
                                                   side_cast="w_down")
    w = {**w, **w_bf}
    ys = _layer_back(front_s, w)
    yp = _layer_back(front_p, w)
    stack = lambda a: a[None]
    return (yp, ys, stack(sp), stack(kp), stack(vp), stack(cp), stack(ss), stack(ks), stack(vs), stack(cs))
```

```python
import functools

import jax
import jax.numpy as jnp
import numpy as np
from jax import lax
from jax.experimental import pallas as pl
from jax.experimental.pallas import tpu as pltpu

_F32 = jnp.float32
_BF16 = jnp.bfloat16

CHUNK = 64
RET_HEADS = 8
RET_DK = 256
RET_DV = 256
RET_QK = RET_HEADS * RET_DK
RET_V = RET_HEADS * RET_DV
ROPE_BASE = 10000.0
SWA_HEADS = 32
SWA_KV_HEADS = 4
SWA_HD = 64
SWA_GROUP = SWA_HEADS // SWA_KV_HEADS
SWA_PAIRS = SWA_GROUP // 2
SWA_Q = SWA_HEADS * SWA_HD
SWA_KV = SWA_KV_HEADS * SWA_HD
WINDOW = 128
WINDOW_CHUNKS = WINDOW // CHUNK
CONV_W = 3
EPS = 1e-6
NEG_INF = -1e30
PAST_LEN = 2048

_V7X_VMEM_BYTES = 64 * 1024 * 1024
_LANES = 128
_SUBLANES = 8

_TM = 1024
_TM_DOWN = 512
_TM_NORM = 512
_TN = 2 * SWA_KV
_TN_EMIT = 512
_TN_ADA = 512
_RET_CHUNK = 256
_SWA_QB = 2
_TN_FF = 256
_UP_TILES = 2
_SIDE_ROWS = 128


def _cparams(sem, vmem_estimate, flags=None):
    limit = min(int(vmem_estimate * 1.15) + (4 << 20), _V7X_VMEM_BYTES - (6 << 20))
    return pltpu.CompilerParams(dimension_semantics=sem, vmem_limit_bytes=limit, flags=flags)


def _nbytes(shape, dtype):
    return int(np.prod(shape)) * jnp.dtype(dtype).itemsize


def _row_tile(m, t, cap):
    tm = min(cap, m)
    assert m % tm == 0 and (t % tm == 0 or tm % t == 0), (m, t, tm)
    return tm


def _seq_block(tm, t):
    return max(1, tm // t), max(1, t // tm)


def _mod_spec(tm, t, tn, col):
    spt, tps = _seq_block(tm, t)
    return pl.BlockSpec((spt, 1, tn), lambda i, j: (i // tps, 0, col(i, j)))


def _per_seq(val, mod, fn):
    spt = mod.shape[0]
    if spt == 1:
        return fn(val, mod[0])
    tm, n = val.shape
    return fn(val.reshape(spt, tm // spt, n), mod).reshape(tm, n)


def _ada_kernel(c_ref, w_ref, b_ref, o_ref):
    c = c_ref[...]
    a = (c * jax.nn.sigmoid(c)).astype(_BF16)
    o_ref[...] = jnp.dot(a, w_ref[...].astype(_BF16), preferred_element_type=_F32) + b_ref[...]


def _ada(c, w_ada, b_ada):
    rows, d = c.shape
    n = w_ada.shape[1]
    tn = _TN_ADA
    est = 2 * (_nbytes((rows, d), _F32) + _nbytes((d, tn), _F32) + _nbytes((rows, tn), _F32)) + _nbytes((d, tn), _BF16)
    return pl.pallas_call(
        _ada_kernel,
        grid=(n // tn,),
        in_specs=[pl.BlockSpec((rows, d), lambda j: (0, 0)),
                  pl.BlockSpec((d, tn), lambda j: (0, j)),
                  pl.BlockSpec((1, tn), lambda j: (0, j))],
        out_specs=pl.BlockSpec((rows, tn), lambda j: (0, j)),
        out_shape=jax.ShapeDtypeStruct((rows, n), _F32),
        compiler_params=_cparams(("arbitrary",), est),
        name="ada_mod",
    )(c, w_ada, b_ada.reshape(1, n))


def _norm_rows(m):
    return _TM_NORM if m >= 8 * _TM_NORM else _TM_NORM // 2


def _rms(x, g):
    y = x * lax.rsqrt(jnp.mean(x * x, axis=-1, keepdims=True) + EPS)
    return y * g


def _norm_mod_kernel(x_ref, g_ref, sc_ref, sh_ref, o_ref):
    y = _rms(x_ref[...], g_ref[...])
    h = _per_seq(y, sc_ref[...], lambda v, sc: v * (1.0 + sc))
    h = _per_seq(h, sh_ref[...], lambda v, sh: v + sh)
    o_ref[...] = h.astype(o_ref.dtype)


def _norm_mod(x, g, sc, sh, t):
    m, d = x.shape
    tm = _row_tile(m, t, _norm_rows(m))
    est = 2 * (_nbytes((tm, d), _F32) + _nbytes((tm, d), _BF16)) + 3 * _nbytes((tm, d), _F32)
    return pl.pallas_call(
        _norm_mod_kernel,
        grid=(m // tm, 1),
        in_specs=[pl.BlockSpec((tm, d), lambda i, j: (i, 0)),
                  pl.BlockSpec((1, d), lambda i, j: (0, 0)),
                  _mod_spec(tm, t, d, lambda i, j: 0),
                  _mod_spec(tm, t, d, lambda i, j: 0)],
        out_specs=pl.BlockSpec((tm, d), lambda i, j: (i, 0)),
        out_shape=jax.ShapeDtypeStruct((m, d), _BF16),
        compiler_params=_cparams(("arbitrary", "arbitrary"), est),
        name="norm_mod",
    )(x, g.reshape(1, d), sc, sh)


def _final_norm_kernel(x_ref, g_ref, o_ref):
    o_ref[...] = _rms(x_ref[...], g_ref[...])


def _final_norm(x, g):
    m, d = x.shape
    tm = min(_norm_rows(m), m)
    est = 4 * _nbytes((tm, d), _F32) + 2 * _nbytes((tm, d), _F32)
    return pl.pallas_call(
        _final_norm_kernel,
        grid=(m // tm,),
        in_specs=[pl.BlockSpec((tm, d), lambda i: (i, 0)),
                  pl.BlockSpec((1, d), lambda i: (0, 0))],
        out_specs=pl.BlockSpec((tm, d), lambda i: (i, 0)),
        out_shape=jax.ShapeDtypeStruct((m, d), _F32),
        compiler_params=_cparams(("arbitrary",), est),
        name="final_norm",
    )(x, g.reshape(1, d))


def _pair_slabs(slab, parity, fill):
    low = lax.broadcasted_iota(jnp.int32, slab.shape, 1) < SWA_HD
    if parity == 0:
        even = jnp.where(low, slab, 0.0)
        odd = pltpu.roll(even, SWA_HD, 1)
    else:
        odd = jnp.where(low, 0.0, slab)
        even = pltpu.roll(odd, SWA_HD, 1)
    if fill == 0.0:
        return even, odd
    return jnp.where(low, even, fill), jnp.where(low, fill, odd)


class _InPlan:
    def __init__(self, tn, pair, d):
        w = tn * pair
        n_skv = 2 * SWA_KV // tn
        assert RET_QK % w == 0 and SWA_Q % w == 0 and (2 * d) % w == 0 and (RET_V + 2 * d) % w == 0
        assert tn % RET_DK == 0 and (2 * SWA_KV) % tn == 0 and (pair == 1 or n_skv == 1)
        self.tn, self.pair, self.w = tn, pair, w
        r = self.r = RET_QK // w
        self.t_skv = 5 * r
        self.t_gab = self.t_skv + n_skv
        self.steps = self.t_gab + 2 * d // w
        self.skv_block = (RET_V + 2 * d) // w
        self.fo_cols = RET_V + 2 * d + 2 * SWA_KV

    def w_tile(self, t):
        p, ts, tg = self.pair, self.t_skv, self.t_gab
        return jnp.where(t < ts, p * t, jnp.where(t < tg, p * ts + (t - ts), p * ts + (tg - ts) + p * (t - tg)))

    def bq_block(self, t):
        r = self.r
        return jnp.where(t < 3 * r, t, jnp.where(t < 4 * r, 3 * r - 1, jnp.where(t < 5 * r, t - r, 4 * r - 1)))

    def fo_block(self, t):
        r, ts, tg = self.r, self.t_skv, self.t_gab
        return jnp.where(t < 3 * r, 0, jnp.where(t < 4 * r, t - 3 * r, jnp.where(
            t < ts, r - 1, jnp.where(t < tg, self.skv_block + (t - ts), r + (t - tg)))))


def _in_proj_kernel(*refs, plan, emit, side_cast):
    n_w = plan.pair
    h_ref, w_refs, (cos_ref, sin_ref) = refs[0], refs[1:1 + n_w], refs[1 + n_w:3 + n_w]
    rest = list(refs[3 + n_w:])
    side_ref = rest.pop(0) if side_cast else None
    bq_ref, fo_ref = rest[:2]
    if emit:
        wbf_ref = rest[2]
        _bf16_weight(w_refs[0], wbf_ref)
        w_refs = (wbf_ref,)
    side_out_ref = rest[-1] if side_cast else None
    t = pl.program_id(1)
    r, tn = plan.r, plan.tn
    half = RET_DK // 2

    def tiles():
        if side_cast:
            side_out_ref[...] = side_ref[...].astype(_BF16)
        for n, w_ref in enumerate(w_refs):
            yield jnp.dot(h_ref[...], _bf16_weight(w_ref, None), preferred_element_type=_F32), n * tn

    @pl.when(t < 2 * r)
    def _rope():
        c = cos_ref[...]
        s = sin_ref[...]
        scale = jnp.where(t >= r, RET_DK ** -0.5, 1.0).astype(_F32)
        for acc, col in tiles():
            for hh in range(tn // RET_DK):
                lo = hh * RET_DK
                x1 = acc[:, lo:lo + half]
                x2 = acc[:, lo + half:lo + RET_DK]
                bq_ref[:, col + lo:col + lo + half] = ((x1 * c - x2 * s) * scale).astype(_BF16)
                bq_ref[:, col + lo + half:col + lo + RET_DK] = ((x2 * c + x1 * s) * scale).astype(_BF16)

    @pl.when(((t >= 2 * r) & (t < 3 * r)) | ((t >= 4 * r) & (t < 5 * r)))
    def _bf16():
        for acc, col in tiles():
            bq_ref[:, col:col + tn] = acc.astype(_BF16)

    @pl.when(((t >= 3 * r) & (t < 4 * r)) | (t >= plan.t_gab))
    def _f32():
        for acc, col in tiles():
            fo_ref[:, col:col + tn] = acc

    @pl.when((t >= plan.t_skv) & (t < plan.t_gab))
    def _kv():
        if side_cast:
            side_out_ref[...] = side_ref[...].astype(_BF16)
        fo_ref[:, 0:tn] = jnp.dot(h_ref[...], _bf16_weight(w_refs[0], None), preferred_element_type=_F32)


def _rope_tables(t, pos0, tm):
    half = RET_DK // 2
    inv = ROPE_BASE ** (-jnp.arange(half, dtype=_F32) * (2.0 / RET_DK))
    pos = (pos0 + jnp.arange(t, dtype=jnp.int32)).astype(_F32)
    ang = pos[:, None] * inv[None, :]
    reps = max(1, tm // t)
    return jnp.tile(jnp.cos(ang), (reps, 1)), jnp.tile(jnp.sin(ang), (reps, 1))


def _in_proj(h, w_in, t, pos0, side_cast=None):
    m, d = h.shape
    n_in = _w_cols(w_in)
    emit = w_in.dtype == _F32
    tm = _row_tile(m, t, _TM)
    plan = _InPlan(_TN_EMIT, 1, d) if emit else _InPlan(_TN, 2, d)
    tn, w = plan.tn, plan.w
    assert n_in == plan.fo_cols + 4 * RET_QK, (n_in, d)
    assert not emit or m == tm, "the bf16 copy is written by a single row tile"
    cos, sin = _rope_tables(t, pos0, tm)
    tab_blocks = cos.shape[0] // tm
    half = RET_DK // 2
    last_tile = n_in // tn - 1

    out_shape = [jax.ShapeDtypeStruct((m, 4 * RET_QK), _BF16), jax.ShapeDtypeStruct((m, plan.fo_cols), _F32)]
    out_specs = [pl.BlockSpec((tm, w), lambda i, j: (i, plan.bq_block(j))),
                 pl.BlockSpec((tm, w), lambda i, j: (i, plan.fo_block(j)))]
    w_specs = [_w_spec(w_in, d, tn, lambda i, j, n=n: jnp.minimum(plan.w_tile(j) + n, last_tile))
               for n in range(plan.pair)]
    est = 2 * (_nbytes((tm, d), _BF16) + plan.pair * _nbytes((d, tn), w_in.dtype) + 2 * _nbytes((tm, half), _F32)
               + _nbytes((tm, w), _BF16) + _nbytes((tm, w), _F32)) + 2 * plan.pair * _nbytes((tm, tn), _F32)
    if emit:
        copy_shape, copy_spec = _w_copy(d, n_in, tn, lambda i, j: plan.w_tile(j))
        out_shape.append(copy_shape)
        out_specs.append(copy_spec)
        est += 2 * _nbytes((d, tn), _BF16)
    side_args, side_specs = [], []
    if side_cast is not None:
        rows, cols = side_cast.shape
        n_blocks = rows // _SIDE_ROWS
        assert not emit and rows % _SIDE_ROWS == 0 and n_blocks <= (m // tm) * plan.steps
        side_spec = pl.BlockSpec((_SIDE_ROWS, cols), lambda i, j: (jnp.minimum(i * plan.steps + j, n_blocks - 1), 0))
        side_args, side_specs = [side_cast], [side_spec]
        out_shape.append(jax.ShapeDtypeStruct((rows, cols), _BF16))
        out_specs.append(side_spec)
        est += 2 * (_nbytes((_SIDE_ROWS, cols), _F32) + _nbytes((_SIDE_ROWS, cols), _BF16))
    return pl.pallas_call(
        functools.partial(_in_proj_kernel, plan=plan, emit=emit, side_cast=side_cast is not None),
        grid=(m // tm, plan.steps),
        in_specs=[pl.BlockSpec((tm, d), lambda i, j: (i, 0)), *w_specs,
                  pl.BlockSpec((tm, half), lambda i, j: (i % tab_blocks, 0)),
                  pl.BlockSpec((tm, half), lambda i, j: (i % tab_blocks, 0)), *side_specs],
        out_specs=out_specs,
        out_shape=out_shape,
        compiler_params=_cparams(("arbitrary", "arbitrary"), est),
        name="in_proj_emit" if emit else "in_proj",
    )(h, *([w_in] * plan.pair), cos, sin, *side_args)


def _retention_kernel(*refs, hb, cc, n_chunks, has_state):
    if has_state:
        lg_ref, q_ref, k_ref, v_ref, rg_ref, s0_ref, a_ref, sout_ref, s_scr = refs
    else:
        lg_ref, q_ref, k_ref, v_ref, rg_ref, a_ref, sout_ref, s_scr = refs
    row = lax.broadcasted_iota(jnp.int32, (cc, cc), 0)
    col = lax.broadcasted_iota(jnp.int32, (cc, cc), 1)
    diff = (row - col).astype(_F32)
    idx = lax.broadcasted_iota(jnp.int32, (cc, 1), 0).astype(_F32)
    for hh in range(hb):
        lg = lg_ref[hh][:, :1]
        dmask = jnp.where(diff >= 0, jnp.exp(lg * jnp.maximum(diff, 0.0)), 0.0)
        dq = jnp.exp(lg * (idx + 1.0))
        dk = jnp.exp(lg * (cc - 1.0 - idx))
        ds = jnp.exp(lg * float(cc))
        cols = slice(hh * RET_DK, (hh + 1) * RET_DK)
        if has_state:
            s_scr[...] = s0_ref[0, hh]
        else:
            s_scr[...] = jnp.zeros_like(s_scr)

        def body(c, carry):
            r = pl.ds(pl.multiple_of(c * cc, cc), cc)
            q = q_ref[r, cols]
            k = k_ref[r, cols]
            v = v_ref[r, cols]
            state = s_scr[...]
            scores = lax.dot_general(q, k, (((1,), (1,)), ((), ())), preferred_element_type=_F32) * dmask
            y = jnp.dot(scores.astype(_BF16), v, preferred_element_type=_F32)
            y = y + jnp.dot(q, state.astype(_BF16), preferred_element_type=_F32) * dq
            k_dec = (k.astype(_F32) * dk).astype(_BF16)
            s_scr[...] = state * ds + lax.dot_general(
                k_dec, v, (((0,), (0,)), ((), ())), preferred_element_type=_F32)
            mu = jnp.mean(y, axis=-1, keepdims=True)
            dev = y - mu
            var = jnp.mean(dev * dev, axis=-1, keepdims=True)
            yn = dev * lax.rsqrt(var + EPS)
            g = rg_ref[r, cols]
            a_ref[r, cols] = (g * jax.nn.sigmoid(g) * yn).astype(_BF16)
            return carry

        lax.fori_loop(0, n_chunks, body, 0, unroll=next(u for u in (4, 2, 1) if n_chunks % u == 0))
        sout_ref[0, hh] = s_scr[...]


def _retention(qkv, rg, state, b, t):
    m = qkv.shape[0]
    cc = min(_RET_CHUNK, t)
    assert t % cc == 0 and cc % min(CHUNK, t) == 0
    hb = RET_HEADS if t * RET_QK * 2 <= (1 << 20) else 2
    nhb = RET_HEADS // hb
    wcols = hb * RET_DK
    h = jnp.arange(RET_HEADS, dtype=_F32)
    lg = jnp.log1p(-jnp.exp2(-5.0 - h))
    lg = jnp.broadcast_to(lg[:, None, None], (RET_HEADS, 1, _LANES))
    has_state = state is not None
    in_specs = [pl.BlockSpec((hb, 1, _LANES), lambda bi, hi: (hi, 0, 0)),
                pl.BlockSpec((t, wcols), lambda bi, hi: (bi, hi)),
                pl.BlockSpec((t, wcols), lambda bi, hi: (bi, nhb + hi)),
                pl.BlockSpec((t, wcols), lambda bi, hi: (bi, 2 * nhb + hi)),
                pl.BlockSpec((t, wcols), lambda bi, hi: (bi, hi))]
    args = [lg, qkv, qkv, qkv, rg]
    s_spec = pl.BlockSpec((1, hb, RET_DK, RET_DV), lambda bi, hi: (bi, hi, 0, 0))
    if has_state:
        in_specs.append(s_spec)
        args.append(state)
    est = 2 * (4 * _nbytes((t, wcols), _BF16) + _nbytes((t, wcols), _F32)
               + 2 * _nbytes((hb, RET_DK, RET_DV), _F32)) + 8 * _nbytes((cc, max(cc, RET_DV)), _F32)
    kern = functools.partial(_retention_kernel, hb=hb, cc=cc, n_chunks=t // cc, has_state=has_state)
    return pl.pallas_call(
        kern,
        grid=(b, nhb),
        in_specs=in_specs,
        out_specs=(pl.BlockSpec((t, wcols), lambda bi, hi: (bi, hi)), s_spec),
        out_shape=(jax.ShapeDtypeStruct((m, RET_V), _BF16),
                   jax.ShapeDtypeStruct((b, RET_HEADS, RET_DK, RET_DV), _F32)),
        scratch_shapes=[pltpu.VMEM((RET_DK, RET_DV), _F32)],
        compiler_params=_cparams(("arbitrary", "arbitrary"), est),
        name="retention",
    )(*args)


def _swa_core(qst, k_even, k_odd, v_even, v_odd, sink_even, sink_odd, valid):
    def weights(kx, sink):
        s = lax.dot_general(qst, kx, (((1,), (1,)), ((), ())), preferred_element_type=_F32) * (SWA_HD ** -0.5)
        if valid is not None:
            s = jnp.where(valid, s, NEG_INF)
        mx = jnp.maximum(jnp.max(s, axis=-1, keepdims=True), sink)
        return jnp.exp(s - mx).astype(_BF16), jnp.exp(sink - mx)

    e_even, t_even = weights(k_even, sink_even)
    e_odd, t_odd = weights(k_odd, sink_odd)
    o_even = jnp.dot(e_even, v_even, preferred_element_type=_F32)
    o_odd = jnp.dot(e_odd, v_odd, preferred_element_type=_F32)
    low = lax.broadcasted_iota(jnp.int32, o_even.shape, 1) < SWA_HD
    num = jnp.where(low, o_even, o_odd)
    den = jnp.where(low, pltpu.roll(o_even, SWA_HD, 1) + t_even, pltpu.roll(o_odd, SWA_HD, 1) + t_odd)
    return num / den


def _swa_prompt_kernel(q_ref, k_ref, v_ref, sink_ref, o_ref, slab_ref, *, n_blocks, unroll):
    gw = SWA_GROUP * SWA_HD
    rows = _SWA_QB * CHUNK
    win = _SWA_QB + WINDOW_CHUNKS

    for par in range(2):
        slabs = _pair_slabs(k_ref[...], par, fill=0.0) + _pair_slabs(v_ref[...], par, fill=1.0)
        for c, x in enumerate(slabs):
            slab_ref[par, c] = x.astype(_BF16)

    def valid_mask(n_key_chunks, first):
        shape = (SWA_PAIRS * rows, n_key_chunks * CHUNK)
        qc = (lax.broadcasted_iota(jnp.int32, shape, 0) // CHUNK) % _SWA_QB
        kc = lax.broadcasted_iota(jnp.int32, shape, 1) // CHUNK
        if first:
            return kc <= qc
        return (kc >= qc) & (kc <= qc + WINDOW_CHUNKS)

    valid_first = valid_mask(_SWA_QB, True)
    valid = valid_mask(win, False)
    for par in range(2):
        sink_even = sink_ref[par, 0]
        sink_odd = sink_ref[par, 1]
        lane0 = par * gw

        def block(r, kr, mask):
            qst = jnp.concatenate(
                [q_ref[r, lane0 + p * _LANES:lane0 + (p + 1) * _LANES] for p in range(SWA_PAIRS)], axis=0)
            o = _swa_core(qst, slab_ref[par, 0, kr, :], slab_ref[par, 1, kr, :],
                          slab_ref[par, 2, kr, :], slab_ref[par, 3, kr, :], sink_even, sink_odd, mask)
            for p in range(SWA_PAIRS):
                o_ref[r, lane0 + p * _LANES:lane0 + (p + 1) * _LANES] = (
                    o[p * rows:(p + 1) * rows].astype(o_ref.dtype))

        block(pl.ds(0, rows), pl.ds(0, rows), valid_first)
        if n_blocks > 1:
            def body(b, carry):
                r = pl.ds(pl.multiple_of(b * rows, rows), rows)
                kr = pl.ds(pl.multiple_of(b * rows - WINDOW_CHUNKS * CHUNK, CHUNK), win * CHUNK)
                block(r, kr, valid)
                return carry

            lax.fori_loop(1, n_blocks, body, 0, unroll=unroll)


def _sink_columns(sinks, rows):
    s = sinks.astype(_F32).reshape(SWA_KV_HEADS, SWA_PAIRS, 2).transpose(0, 2, 1)
    return jnp.repeat(s, rows, axis=-1)[..., None]


def _swa_prompt(bq, fo, sinks, b, t):
    m = bq.shape[0]
    gw2 = 2 * SWA_GROUP * SWA_HD
    rows = _SWA_QB * CHUNK
    assert t % rows == 0 and WINDOW_CHUNKS == _SWA_QB
    n_blocks = t // rows
    unroll = next(u for u in (15, 5, 3, 2, 1) if (n_blocks - 1) % u == 0)
    sink_cols = _sink_columns(sinks, rows)
    q0 = 3 * RET_QK // gw2
    k0 = (fo.shape[1] - 2 * SWA_KV) // _LANES
    v0 = k0 + SWA_KV // _LANES
    est = (2 * (2 * _nbytes((t, gw2), _BF16) + 2 * _nbytes((t, _LANES), _F32))
           + _nbytes((2, 4, t, _LANES), _BF16) + (8 << 20))
    return pl.pallas_call(
        functools.partial(_swa_prompt_kernel, n_blocks=n_blocks, unroll=unroll),
        grid=(b, SWA_KV_HEADS // 2),
        in_specs=[pl.BlockSpec((t, gw2), lambda bi, kp: (bi, q0 + kp)),
                  pl.BlockSpec((t, _LANES), lambda bi, kp: (bi, k0 + kp)),
                  pl.BlockSpec((t, _LANES), lambda bi, kp: (bi, v0 + kp)),
                  pl.BlockSpec((2, 2, SWA_PAIRS * rows, 1), lambda bi, kp: (kp, 0, 0, 0))],
        out_specs=pl.BlockSpec((t, gw2), lambda bi, kp: (bi, kp)),
        out_shape=jax.ShapeDtypeStruct((m, SWA_Q), _BF16),
        scratch_shapes=[pltpu.VMEM((2, 4, t, _LANES), _BF16)],
        compiler_params=_cparams(("arbitrary", "arbitrary"), est),
        name="swa_prompt",
    )(bq, fo, fo, sink_cols)


def _swa_cache_kernel(q_ref, kvn_ref, kc_ref, vc_ref, sink_ref, o_ref, *, L):
    gw = SWA_GROUP * SWA_HD
    kvn = kvn_ref[...]
    k_all = jnp.concatenate([kc_ref[0], kvn[:, :SWA_KV]], axis=0)
    v_all = jnp.concatenate([vc_ref[0], kvn[:, SWA_KV:]], axis=0)
    for kv in range(SWA_KV_HEADS):
        lanes = slice(_LANES * (kv // 2), _LANES * (kv // 2 + 1))
        slabs = _pair_slabs(k_all[:, lanes], kv % 2, fill=0.0) + _pair_slabs(v_all[:, lanes], kv % 2, fill=1.0)
        qst = jnp.concatenate(
            [q_ref[:, kv * gw + p * _LANES:kv * gw + (p + 1) * _LANES] for p in range(SWA_PAIRS)], axis=0)
        o = _swa_core(qst, *[x.astype(_BF16) for x in slabs], sink_ref[kv, 0], sink_ref[kv, 1], None)
        for p in range(SWA_PAIRS):
            lo = kv * gw + p * _LANES
            o_ref[:, lo:lo + _LANES] = o[p * L:(p + 1) * L].astype(o_ref.dtype)


def _swa_cache(bq, fo, k_cache, v_cache, sinks, b, t):
    m = bq.shape[0]
    rows = k_cache.shape[1]
    kc = k_cache.reshape(b, rows, SWA_KV)
    vc = v_cache.reshape(b, rows, SWA_KV)
    sink_cols = _sink_columns(sinks, t)
    assert (fo.shape[1] - 2 * SWA_KV) % (2 * SWA_KV) == 0
    kv0 = fo.shape[1] // (2 * SWA_KV) - 1
    est = 2 * (2 * _nbytes((t, SWA_Q), _BF16) + _nbytes((t, 2 * SWA_KV), _F32)
               + 2 * _nbytes((rows, SWA_KV), _F32)) + (4 << 20)
    return pl.pallas_call(
        functools.partial(_swa_cache_kernel, L=t),
        grid=(b,),
        in_specs=[pl.BlockSpec((t, SWA_Q), lambda bi: (bi, 3 * RET_QK // SWA_Q)),
                  pl.BlockSpec((t, 2 * SWA_KV), lambda bi: (bi, kv0)),
                  pl.BlockSpec((1, rows, SWA_KV), lambda bi: (bi, 0, 0)),
                  pl.BlockSpec((1, rows, SWA_KV), lambda bi: (bi, 0, 0)),
                  pl.BlockSpec((SWA_KV_HEADS, 2, SWA_PAIRS * t, 1), lambda bi: (0, 0, 0, 0))],
        out_specs=pl.BlockSpec((t, SWA_Q), lambda bi: (bi, 0)),
        out_shape=jax.ShapeDtypeStruct((m, SWA_Q), _BF16),
        compiler_params=_cparams(("arbitrary",), est),
        name="swa_cache",
    )(bq, fo, kc, vc, sink_cols)


def _bf16_weight(w_ref, wbf_ref):
    if wbf_ref is None:
        return w_ref[...]
    wbf_ref[...] = w_ref[...].astype(_BF16)
    return wbf_ref[...]


def _w_cols(w):
    return w.shape[1]


def _w_spec(w, k, tn, tile):
    assert w.shape[0] == k
    return pl.BlockSpec((k, tn), lambda i, j: (0, tile(i, j)))


def _w_copy(k, n, tn, tile):
    return jax.ShapeDtypeStruct((k, n), _BF16), pl.BlockSpec((k, tn), lambda i, j: (0, tile(i, j)))


def _merge_kernel(a_ref, o_ref, wr_ref, ws_ref, ga_ref, gb_ref, out_ref, wrb_ref=None, wsb_ref=None):
    yr = jnp.dot(a_ref[...], _bf16_weight(wr_ref, wrb_ref), preferred_element_type=_F32)
    ys = jnp.dot(o_ref[...], _bf16_weight(ws_ref, wsb_ref), preferred_element_type=_F32)
    out_ref[...] = (jax.nn.sigmoid(ga_ref[...]) * yr + jax.nn.sigmoid(gb_ref[...]) * ys).astype(out_ref.dtype)


def _merge(a_ret, o_swa, fo, w_ret_o, w_swa_o):
    m = a_ret.shape[0]
    d = _w_cols(w_ret_o)
    emit = w_ret_o.dtype == _F32
    tm = min(_TM, m)
    tn = _TN
    nd = d // tn
    g0 = RET_V // tn
    col = lambda i, j: j
    assert not emit or m == tm, "the bf16 copies are written by a single row tile"
    est = 2 * (_nbytes((tm, RET_V), _BF16) + _nbytes((tm, SWA_Q), _BF16) + _nbytes((RET_V, tn), w_ret_o.dtype)
               + _nbytes((SWA_Q, tn), w_swa_o.dtype) + 2 * _nbytes((tm, tn), _F32) + _nbytes((tm, tn), _BF16)
               ) + 4 * _nbytes((tm, tn), _F32)
    out_specs = [pl.BlockSpec((tm, tn), lambda i, j: (i, j))]
    out_shape = [jax.ShapeDtypeStruct((m, d), _BF16)]
    if emit:
        for k in (RET_V, SWA_Q):
            copy_shape, copy_spec = _w_copy(k, d, tn, col)
            out_shape.append(copy_shape)
            out_specs.append(copy_spec)
        est += 2 * (_nbytes((RET_V, tn), _BF16) + _nbytes((SWA_Q, tn), _BF16))
    return pl.pallas_call(
        _merge_kernel,
        grid=(m // tm, nd),
        in_specs=[pl.BlockSpec((tm, RET_V), lambda i, j: (i, 0)),
                  pl.BlockSpec((tm, SWA_Q), lambda i, j: (i, 0)),
                  _w_spec(w_ret_o, RET_V, tn, col),
                  _w_spec(w_swa_o, SWA_Q, tn, col),
                  pl.BlockSpec((tm, tn), lambda i, j: (i, g0 + j)),
                  pl.BlockSpec((tm, tn), lambda i, j: (i, g0 + nd + j))],
        out_specs=out_specs,
        out_shape=out_shape,
        compiler_params=_cparams(("arbitrary", "arbitrary"), est),
        name="merge_emit" if emit else "merge",
    )(a_ret, o_swa, w_ret_o, w_swa_o, fo, fo)


def _proj_residual_kernel(*refs, tiles):
    a_ref, w_refs, (x_ref, gt_ref, o_ref) = refs[0], refs[1:1 + tiles], refs[1 + tiles:4 + tiles]
    wbf_ref = refs[4 + tiles] if len(refs) > 4 + tiles else None
    tn = w_refs[0].shape[-1]
    for c, w_ref in enumerate(w_refs):
        lanes = slice(c * tn, (c + 1) * tn)
        f = jnp.dot(a_ref[...], _bf16_weight(w_ref, wbf_ref), preferred_element_type=_F32)
        o_ref[:, lanes] = x_ref[:, lanes] + _per_seq(f, gt_ref[:, :, lanes], lambda v, gt: gt * v)


def _proj_residual(a, w, x, gt, t, tm_cap, tiles, name):
    m, k = a.shape
    d = _w_cols(w)
    emit = w.dtype == _F32
    tm = _row_tile(m, t, tm_cap)
    tn = _TN
    tiles = 1 if emit else min(tiles, d // tn)
    wd = tiles * tn
    assert d % wd == 0
    assert not emit or m == tm, "the bf16 copy is written by a single row tile"
    est = 2 * (_nbytes((tm, k), _BF16) + tiles * _nbytes((k, tn), w.dtype) + 2 * _nbytes((tm, wd), _F32)
               ) + 2 * _nbytes((tm, tn), _F32)
    out_specs = [pl.BlockSpec((tm, wd), lambda i, j: (i, j))]
    out_shape = [jax.ShapeDtypeStruct((m, d), _F32)]
    if emit:
        copy_shape, copy_spec = _w_copy(k, d, tn, lambda i, j: j)
        out_shape.append(copy_shape)
        out_specs.append(copy_spec)
        est += 2 * _nbytes((k, tn), _BF16)
    return pl.pallas_call(
        functools.partial(_proj_residual_kernel, tiles=tiles),
        grid=(m // tm, d // wd),
        in_specs=[pl.BlockSpec((tm, k), lambda i, j: (i, 0)),
                  *[_w_spec(w, k, tn, lambda i, j, c=c: tiles * j + c) for c in range(tiles)],
                  pl.BlockSpec((tm, wd), lambda i, j: (i, j)),
                  _mod_spec(tm, t, wd, lambda i, j: j)],
        out_specs=out_specs,
        out_shape=out_shape,
        compiler_params=_cparams(("arbitrary", "arbitrary"), est),
        name=name + "_emit" if emit else name,
    )(a, *([w] * tiles), x, gt)


def _up_kernel(*refs, tps, nf, tiles, emit):
    h_ref, wu_refs, wv_refs = refs[0], refs[1:1 + tiles], refs[1 + tiles:1 + 2 * tiles]
    wc_ref, bc_ref, cs_ref, g_ref, tail_ref = refs[1 + 2 * tiles:6 + 2 * tiles]
    wub_ref, wvb_ref = refs[6 + 2 * tiles:8 + 2 * tiles] if emit else (None, None)
    u_scr, v_scr, carry_ref = refs[-3:]
    i = pl.program_id(0)
    j = pl.program_id(1)
    tm = u_scr.shape[0]
    tn = u_scr.shape[1] // tiles
    n_full, rem = divmod(nf, tiles)

    def ordered_after(x, token):
        bits = pltpu.bitcast(token, jnp.uint32)
        zero = pltpu.bitcast((bits >> 16) >> 16, _F32)
        return (x.reshape(tm // _SUBLANES, _SUBLANES, tn) + zero[None]).reshape(tm, tn)

    def matmul(k, token=None):
        h = h_ref[...]
        for c in range(k):
            lanes = slice(c * tn, (c + 1) * tn)
            u = jnp.dot(h, _bf16_weight(wu_refs[c], wub_ref), preferred_element_type=_F32)
            u_scr[:, lanes] = u if (token is None or c > 0) else ordered_after(u, token)
            v_scr[:, lanes] = jnp.dot(h, _bf16_weight(wv_refs[c], wvb_ref), preferred_element_type=_F32)

    def epilogue_tile(lanes):
        jp = j - 1
        u = u_scr[:, lanes]
        v = v_scr[:, lanes]
        cs = cs_ref[:, :, lanes]
        spt = cs.shape[0]
        rows = tm // spt
        c0 = cs[:, 0:1, :]
        c1 = cs[:, 1:2, :]
        if tps > 1:
            first = (i % tps) == 0
            prev = carry_ref[jp, :, lanes]
            c0 = jnp.where(first, c0, prev[_SUBLANES - 2:_SUBLANES - 1][None])
            c1 = jnp.where(first, c1, prev[_SUBLANES - 1:_SUBLANES][None])
            carry_ref[jp, :, lanes] = u[tm - _SUBLANES:tm]
        wc = wc_ref[:, lanes]
        bc = bc_ref[:, lanes]
        if spt == 1:
            r = lax.broadcasted_iota(jnp.int32, (tm, tn), 0)
            p1 = jnp.where(r == 0, c1[0], pltpu.roll(u, 1, 0))
            p2 = jnp.where(r == 0, c0[0], jnp.where(r == 1, c1[0], pltpu.roll(u, 2, 0)))
            uc = wc[0:1] * p2 + wc[1:2] * p1 + wc[2:3] * u + bc
            g = jax.nn.gelu(uc) * v
            tail_ref[0, 0, :, lanes] = u[tm - _SUBLANES:tm]
        else:
            u3 = u.reshape(spt, rows, tn)
            r = lax.broadcasted_iota(jnp.int32, (spt, rows, tn), 1)
            p1 = jnp.where(r == 0, c1, pltpu.roll(u3, 1, 1))
            p2 = jnp.where(r == 0, c0, jnp.where(r == 1, c1, pltpu.roll(u3, 2, 1)))
            uc = wc[0:1][None] * p2 + wc[1:2][None] * p1 + wc[2:3][None] * u3 + bc[None]
            g = (jax.nn.gelu(uc) * v.reshape(spt, rows, tn)).reshape(tm, tn)
            tail_ref[0, :, :, lanes] = u3[:, rows - _SUBLANES:rows, :]
        g_ref[:, lanes] = g.astype(g_ref.dtype)
        return jnp.max(g.reshape(tm // _SUBLANES, _SUBLANES, tn), axis=0)

    def epilogue(k):
        token = None
        for c in range(k):
            part = epilogue_tile(slice(c * tn, (c + 1) * tn))
            token = part if token is None else jnp.maximum(token, part)
        return token

    @pl.when(j == 0)
    def _first():
        if tps > 1:
            @pl.when(i == 0)
            def _init():
                carry_ref[...] = jnp.zeros_like(carry_ref)
        matmul(tiles)

    @pl.when((j > 0) & (j < n_full))
    def _steady():
        token = epilogue(tiles)
        matmul(tiles, None if emit else token)

    if rem:
        @pl.when(j == n_full)
        def _short():
            token = epilogue(tiles)
            matmul(rem, None if emit else token)

    @pl.when(j == n_full + (1 if rem else 0))
    def _drain():
        epilogue(rem if rem else tiles)


def _up_conv_geglu(h, w_u, w_v, w_conv, b_conv, conv_state, t):
    m, d = h.shape
    f = w_conv.shape[1]
    (wu_arr, wu_tile0), (wv_arr, wv_tile0) = w_u, w_v
    emit = wu_arr.dtype == _F32
    tm = _row_tile(m, t, _TM)
    tn = _TN_FF
    assert f % tn == 0, (f, tn)
    assert not emit or m == tm, "the bf16 copies are written by a single row tile"
    nf = f // tn
    tiles = 1 if emit else min(_UP_TILES, nf)
    n_groups = -(-nf // tiles)
    w = tiles * tn
    spt, tps = _seq_block(tm, t)
    rows = tm // spt
    assert rows >= _SUBLANES and conv_state.shape[1] == CONV_W - 1
    cur = lambda j: jnp.minimum(j, n_groups - 1)
    prev = lambda j: jnp.maximum(j - 1, 0)
    tile = lambda j, c: jnp.minimum(tiles * cur(j) + c, nf - 1)
    est = 2 * (_nbytes((tm, d), _BF16) + 2 * tiles * _nbytes((d, tn), wu_arr.dtype) + _nbytes((tm, w), _BF16)
               + _nbytes((spt, _SUBLANES, w), _F32)) + _nbytes((n_groups, _SUBLANES, w), _F32) + (
        2 * _nbytes((tm, w), _F32) + 10 * _nbytes((tm, tn), _F32))
    out_specs = [pl.BlockSpec((tm, w), lambda i, j: (i, prev(j))),
                 pl.BlockSpec((1, spt, _SUBLANES, w), lambda i, j: (i, 0, 0, prev(j)))]
    out_shape = [jax.ShapeDtypeStruct((m, f), _BF16),
                 jax.ShapeDtypeStruct((m // tm, spt, _SUBLANES, f), _F32)]
    if emit:
        for _ in range(2):
            copy_shape, copy_spec = _w_copy(d, f, tn, lambda i, j: cur(j))
            out_shape.append(copy_shape)
            out_specs.append(copy_spec)
        est += 4 * _nbytes((d, tn), _BF16)
    w_specs = [_w_spec(arr, d, tn, lambda i, j, c=c, t0=t0: t0 + tile(j, c))
               for arr, t0 in ((wu_arr, wu_tile0), (wv_arr, wv_tile0)) for c in range(tiles)]
    return pl.pallas_call(
        functools.partial(_up_kernel, tps=tps, nf=nf, tiles=tiles, emit=emit),
        grid=(m // tm, n_groups + 1),
        in_specs=[pl.BlockSpec((tm, d), lambda i, j: (i, 0)), *w_specs,
                  pl.BlockSpec((CONV_W, w), lambda i, j: (0, prev(j))),
                  pl.BlockSpec((1, w), lambda i, j: (0, prev(j))),
                  pl.BlockSpec((spt, CONV_W - 1, w), lambda i, j: (i // tps, 0, prev(j)))],
        out_specs=out_specs,
        out_shape=out_shape,
        scratch_shapes=[pltpu.VMEM((tm, w), _F32), pltpu.VMEM((tm, w), _F32),
                        pltpu.VMEM((n_groups, _SUBLANES, w), _F32)],
        compiler_params=_cparams(("arbitrary", "arbitrary"), est),
        name="up_conv_geglu_emit" if emit else "up_conv_geglu",
    )(h, *([wu_arr] * tiles), *([wv_arr] * tiles), w_conv, b_conv.reshape(1, f), conv_state)


def _layer_front(x, mod, pos0, ret_state, swa_kc, swa_vc, conv_state, w, side_cast=None):
    b, t, d = x.shape
    m = b * t
    x2 = x.reshape(m, d)
    sh1, sc1, gt1, sh2, sc2, gt2 = [v.reshape(b, 1, d) for v in jnp.split(mod, 6, axis=-1)]
    bf = {}

    h = _norm_mod(x2, w["g_mix"], sc1, sh1, t)
    bq, fo, *w_bf = _in_proj(h, w["w_in"], t, pos0, None if side_cast is None else w[side_cast])
    bf.update(zip(("w_in",) if side_cast is None else (side_cast,), w_bf))
    a_ret, s_new = _retention(bq, fo, ret_state, b, t)
    if swa_kc is None:
        o_swa = _swa_prompt(bq, fo, w["swa_sinks"], b, t)
        keep = min(WINDOW, t)
    else:
        o_swa = _swa_cache(bq, fo, swa_kc, swa_vc, w["swa_sinks"], b, t)
        keep = t
    merged, *w_bf = _merge(a_ret, o_swa, fo, w["w_ret_o"], w["w_swa_o"])
    bf.update(zip(("w_ret_o", "w_swa_o"), w_bf))
    x1, *w_bf = _proj_residual(merged, w["w_out"], x2, gt1, t, _TM, 2, "out_proj")
    bf.update(zip(("w_out",), w_bf))

    h2 = _norm_mod(x1, w["g_ffn"], sc2, sh2, t)
    f = w["b_conv"].shape[0]
    cs = jnp.zeros((b, CONV_W - 1, f), _F32) if conv_state is None else conv_state
    if "w_up_u" in w:
        w_u, w_v = (w["w_up_u"], 0), (w["w_up_v"], 0)
    else:
        w_u, w_v = (w["w_up"], 0), (w["w_up"], f // _TN_FF)
    g, tails, *w_bf = _up_conv_geglu(h2, w_u, w_v, w["w_conv"], w["b_conv"], cs, t)
    bf.update(zip(("w_up_u", "w_up_v"), w_bf))

    skv = fo.reshape(b, t, fo.shape[1])[:, t - keep:, fo.shape[1] - 2 * SWA_KV:]
    k_new = skv[:, :, :SWA_KV].reshape(b, keep, SWA_KV_HEADS, SWA_HD)
    v_new = skv[:, :, SWA_KV:].reshape(b, keep, SWA_KV_HEADS, SWA_HD)
    segs = tails.shape[0] * tails.shape[1] // b
    tails = tails.reshape(b, segs, _SUBLANES, f)
    conv_new = tails[:, segs - 1, _SUBLANES - (CONV_W - 1):, :]
    return (g, x1, gt2, (b, t, d)), (s_new, k_new, v_new, conv_new), bf


def _layer_back(front, w):
    g, x1, gt2, (b, t, d) = front
    x3, = _proj_residual(g, w["w_down"], x1, gt2, t, _TM_DOWN, 1, "down_proj")
    return _final_norm(x3, w["g_final"]).reshape(b, t, d)


def kernel(x_prompt, x_sample, cache_swa_k, cache_swa_v, state_ret, state_conv, c_prompt, c_sample,
           g_mix, g_ffn, w_ada, b_ada, w_in, swa_sinks, w_ret_o, w_swa_o, w_out,
           w_up, w_conv, b_conv, w_down, g_final):
    depth = g_mix.shape[0]
    assert depth == 1, "single-layer stack"
    bp = x_prompt.shape[0]
    bs = x_sample.shape[0]
    l = 0

    c_all = jnp.concatenate([c_prompt, c_sample], axis=0)
    rows = -(-c_all.shape[0] // 16) * 16
    c_all = jnp.pad(c_all, ((0, rows - c_all.shape[0]), (0, 0)))
    mod = _ada(c_all, w_ada[l], b_ada[l])

    w = {
        "g_mix": g_mix[l], "g_ffn": g_ffn[l], "g_final": g_final, "swa_sinks": swa_sinks[l],
        "w_in": w_in[l], "w_ret_o": w_ret_o[l], "w_swa_o": w_swa_o[l], "w_out": w_out[l], "w_up": w_up[l],
        "w_down": w_down[l],
        "w_conv": w_conv[l], "b_conv": b_conv[l],
    }

    front_s, (ss, ks, vs, cs), w_bf = _layer_front(x_sample, mod[bp:bp + bs], PAST_LEN, state_ret[l],
                                                   cache_swa_k[l], cache_swa_v[l], state_conv[l], w)
    w = {**w, **w_bf}
    front_p, (sp, kp, vp, cp), w_bf = _layer_front(x_prompt, mod[:bp], 0, None, None, None, None, w,
                                                   side_cast="w_down")
    w = {**w, **w_bf}
    ys = _layer_back(front_s, w)
    yp = _layer_back(front_p, w)
    stack = lambda a: a[None]
    return (yp, ys, stack(sp), stack(kp), stack(vp), stack(cp), stack(ss), stack(ks), stack(vs), stack(cs))
```
